```python
import jax, jax.numpy as jnp
from jax import lax
import numpy as np

D_MODEL = 1024
BATCH = 2
SEQ = 8192
DEPTH = 1

CTX_LEN = 256
GRID_W = 64
D_RNN = 1024
RNN_HEADS = 4
RNN_HEAD_DIM = D_RNN // RNN_HEADS
CONV_W = 4
LRU_C = 8.0
D_POOL = 1024
POOL_WINDOWS = (2, 4, 8, 16)
POOL_GROUP = D_POOL // len(POOL_WINDOWS)
IN_COLS = 2 * D_RNN + D_POOL + 2 * D_MODEL
N_EXPERTS = 32
TOP_K = 4
D_FF = 1024
SWIGLU_LIMIT = 7.0
SWIGLU_ALPHA = 1.702
EPS = 1e-6

kernel_name = 'hybrid_rglru_pool_moe_block'


def rmsnorm(x, g):
    xf = x.astype(jnp.float32)
    y = xf * lax.rsqrt(jnp.mean(xf * xf, axis=-1, keepdims=True) + EPS)
    return (y * g.astype(jnp.float32)).astype(x.dtype)


def modulate(u, shift, scale):
    return u * (1 + scale) + shift


def dwconv_centred(x, w, b):
    left = CONV_W // 2
    y = lax.conv_general_dilated(x, w[:, None, :], window_strides=(1,),
                                 padding=[(left, CONV_W - 1 - left)],
                                 dimension_numbers=('NWC', 'WIO', 'NWC'),
                                 feature_group_count=x.shape[-1])
    return y + b


def lru_coeffs(xc, wa, ba, wx, bx, lam):
    bsz, t = xc.shape[:2]
    xh = xc.reshape(bsz, t, RNN_HEADS, RNN_HEAD_DIM)
    r = jax.nn.sigmoid(jnp.einsum('bthi,hij->bthj', xh, wa).reshape(bsz, t, D_RNN) + ba)
    i = jax.nn.sigmoid(jnp.einsum('bthi,hij->bthj', xh, wx).reshape(bsz, t, D_RNN) + bx)
    log_a = -LRU_C * r.astype(jnp.float32) * jax.nn.softplus(-lam.astype(jnp.float32))
    a = jnp.exp(log_a)
    mult = jnp.sqrt(-jnp.expm1(2.0 * log_a))
    b = mult * (i * xc).astype(jnp.float32)
    return a, b


def _combine(left, right):
    a_l, b_l = left
    a_r, b_r = right
    return a_l * a_r, a_r * b_l + b_r


def linear_scan(a, b, h0):
    if h0 is not None:
        b = b.at[:, 0].add(a[:, 0] * h0)
    _, h = lax.associative_scan(_combine, (a, b), axis=1)
    return h


def recurrent_states(x_rnn, conv_w, conv_b, wa, ba, wx, bx, lam, h0_f, h0_b):
    xc = dwconv_centred(x_rnn, conv_w, conv_b)
    a_f, b_f = lru_coeffs(xc, wa[0], ba[0], wx[0], bx[0], lam[0])
    a_b, b_b = lru_coeffs(xc, wa[1], ba[1], wx[1], bx[1], lam[1])
    h_f = linear_scan(a_f, b_f, h0_f)
    h_b = jnp.flip(linear_scan(jnp.flip(a_b, 1), jnp.flip(b_b, 1), h0_b), 1)
    return h_f, h_b


def pool_mixer(xp, n_rows, w_grp, scale):
    bsz, t, _ = xp.shape
    line = t // n_rows
    xf = xp.reshape(bsz, n_rows, line, D_POOL).astype(jnp.float32)
    cs = jnp.pad(jnp.cumsum(xf, axis=2), ((0, 0), (0, 0), (1, 0), (0, 0)))
    pos = jnp.arange(line)
    diffs = []
    for gi, w in enumerate(POOL_WINDOWS):
        lo = jnp.clip(pos - w // 2, 0, line)
        hi = jnp.clip(pos + w - w // 2, 0, line)
        seg = cs[..., gi * POOL_GROUP:(gi + 1) * POOL_GROUP]
        mean = (jnp.take(seg, hi, axis=2) - jnp.take(seg, lo, axis=2)) / (hi - lo).astype(jnp.float32)[:, None]
        diffs.append(mean - xf[..., gi * POOL_GROUP:(gi + 1) * POOL_GROUP])
    d = jnp.stack(diffs, axis=-2).astype(xp.dtype)
    y = jnp.einsum('brlgi,gij->brlgj', d, w_grp)
    return y.reshape(bsz, t, D_POOL) * scale


def mixer(u, n_rows, h0_f, h0_b, w_in, conv_w, conv_b, wa, ba, wx, bx, lam,
          pool_w, pool_scale, w_rnn_proj, w_pool_proj, w_out):
    proj = u @ w_in
    x_rnn, g_rnn, x_pool, m_rnn, m_pool = jnp.split(
        proj, [D_RNN, 2 * D_RNN, 2 * D_RNN + D_POOL, 2 * D_RNN + D_POOL + D_MODEL], axis=-1)
    h_f, h_b = recurrent_states(x_rnn, conv_w, conv_b, wa, ba, wx, bx, lam, h0_f, h0_b)
    y_rnn = (h_f + h_b).astype(u.dtype) * jax.nn.gelu(g_rnn)
    y_pool = pool_mixer(x_pool, n_rows, pool_w, pool_scale)
    merged = jax.nn.sigmoid(m_rnn) * (y_rnn @ w_rnn_proj) + jax.nn.sigmoid(m_pool) * (y_pool @ w_pool_proj)
    return merged @ w_out, h_f, h_b


def moe(h, w_r, b_r, w1, b1, w2, b2):
    logits = (h @ w_r + b_r).astype(jnp.float32)
    top_v, top_i = lax.top_k(logits, TOP_K)
    top_w = jax.nn.softmax(top_v, axis=-1)
    comb = jnp.einsum('nk,nke->ne', top_w, jax.nn.one_hot(top_i, N_EXPERTS, dtype=jnp.float32)).astype(h.dtype)
    out = jnp.zeros_like(h)
    for e in range(N_EXPERTS):
        z = h @ w1[e] + b1[e]
        glu = jnp.minimum(z[:, :D_FF], SWIGLU_LIMIT)
        lin = jnp.clip(z[:, D_FF:], -SWIGLU_LIMIT, SWIGLU_LIMIT)
        act = glu * jax.nn.sigmoid(SWIGLU_ALPHA * glu) * (lin + 1)
        out = out + comb[:, e:e + 1] * (act @ w2[e] + b2[e])
    return out


def setup_inputs(seed: int = 0) -> dict:
    key = jax.random.key(seed)
    ks = iter(jax.random.split(key, 40))
    f32 = jnp.float32
    L = DEPTH

    def nrm(shape, scale):
        return jax.random.normal(next(ks), shape, f32) * scale

    def gain(shape):
        return 1.0 + 0.05 * jax.random.normal(next(ks), shape, f32)

    u = jax.random.uniform(next(ks), (L, 2, D_RNN), f32, minval=0.9, maxval=0.999)
    a0 = u ** (1.0 / LRU_C)
    lam = jnp.log(a0) - jnp.log1p(-a0)
    return {
        'x': nrm((BATCH, SEQ, D_MODEL), 1.0),
        'c': nrm((BATCH, D_MODEL), 1.0),
        'ctx': nrm((BATCH, CTX_LEN, D_MODEL), 1.0),
        'c_ctx': nrm((D_MODEL,), 1.0),
        'w_mod': nrm((L, D_MODEL, 6 * D_MODEL), 0.5 * D_MODEL ** -0.5),
        'b_mod': nrm((L, 6 * D_MODEL), 0.02),
        'norm_pre_mix': gain((L, D_MODEL)),
        'norm_post_mix': gain((L, D_MODEL)),
        'norm_pre_ffn': gain((L, D_MODEL)),
        'norm_post_ffn': gain((L, D_MODEL)),
        'w_in': nrm((L, D_MODEL, IN_COLS), D_MODEL ** -0.5),
        'conv_w': nrm((L, CONV_W, D_RNN), CONV_W ** -0.5),
        'conv_b': nrm((L, D_RNN), 0.02),
        'lru_wa': nrm((L, 2, RNN_HEADS, RNN_HEAD_DIM, RNN_HEAD_DIM), RNN_HEAD_DIM ** -0.5),
        'lru_ba': nrm((L, 2, D_RNN), 0.02),
        'lru_wx': nrm((L, 2, RNN_HEADS, RNN_HEAD_DIM, RNN_HEAD_DIM), RNN_HEAD_DIM ** -0.5),
        'lru_bx': nrm((L, 2, D_RNN), 0.02),
        'lru_lambda': lam,
        'pool_w': nrm((L, len(POOL_WINDOWS), POOL_GROUP, POOL_GROUP), POOL_GROUP ** -0.5),
        'pool_scale': 1.0 + 0.1 * jax.random.normal(next(ks), (L, D_POOL), f32),
        'w_rnn_proj': nrm((L, D_RNN, D_MODEL), D_RNN ** -0.5),
        'w_pool_proj': nrm((L, D_POOL, D_MODEL), D_POOL ** -0.5),
        'w_out': nrm((L, D_MODEL, D_MODEL), D_MODEL ** -0.5),
        'router_w': nrm((L, D_MODEL, N_EXPERTS), D_MODEL ** -0.5),
        'router_b': nrm((L, N_EXPERTS), 0.01),
        'exp_w1': nrm((L, N_EXPERTS, D_MODEL, 2 * D_FF), D_MODEL ** -0.5),
        'exp_b1': nrm((L, N_EXPERTS, 2 * D_FF), 0.02),
        'exp_w2': nrm((L, N_EXPERTS, D_FF, D_MODEL), D_FF ** -0.5),
        'exp_b2': nrm((L, N_EXPERTS, D_MODEL), 0.02),
    }


def reference(x, c, ctx, c_ctx, w_mod, b_mod, norm_pre_mix, norm_post_mix, norm_pre_ffn,
              norm_post_ffn, w_in, conv_w, conv_b, lru_wa, lru_ba, lru_wx, lru_bx, lru_lambda,
              pool_w, pool_scale, w_rnn_proj, w_pool_proj, w_out, router_w, router_b,
              exp_w1, exp_b1, exp_w2, exp_b2):
    bsz, seq, d = x.shape
    rows = seq // GRID_W
    for l in range(DEPTH):
        last = l == DEPTH - 1
        mod_lat = (jax.nn.silu(c) @ w_mod[l] + b_mod[l])[:, None, :]
        mod_ctx = jax.nn.silu(c_ctx) @ w_mod[l] + b_mod[l]
        sh1, sc1, g1, sh2, sc2, g2 = jnp.split(mod_lat, 6, axis=-1)
        csh1, csc1, cg1, csh2, csc2, cg2 = jnp.split(mod_ctx, 6, axis=-1)
        mix_w = (w_in[l], conv_w[l], conv_b[l], lru_wa[l], lru_ba[l], lru_wx[l], lru_bx[l],
                 lru_lambda[l], pool_w[l], pool_scale[l], w_rnn_proj[l], w_pool_proj[l], w_out[l])

        u_ctx = modulate(rmsnorm(ctx, norm_pre_mix[l]), csh1, csc1)
        if last:
            hc_f, hc_b = recurrent_states(u_ctx @ w_in[l][:, :D_RNN], conv_w[l], conv_b[l],
                                          lru_wa[l], lru_ba[l], lru_wx[l], lru_bx[l],
                                          lru_lambda[l], None, None)
        else:
            m_ctx, hc_f, hc_b = mixer(u_ctx, 1, None, None, *mix_w)

        u_lat = modulate(rmsnorm(x, norm_pre_mix[l]), sh1, sc1)
        m_lat, _, _ = mixer(u_lat, rows, hc_f[:, -1], hc_b[:, 0], *mix_w)
        x = x + g1 * rmsnorm(m_lat, norm_post_mix[l])
        if not last:
            ctx = ctx + cg1 * rmsnorm(m_ctx, norm_post_mix[l])

        moe_w = (router_w[l], router_b[l], exp_w1[l], exp_b1[l], exp_w2[l], exp_b2[l])
        v_lat = modulate(rmsnorm(x, norm_pre_ffn[l]), sh2, sc2).reshape(-1, d)
        if last:
            f_lat = moe(v_lat, *moe_w).reshape(bsz, seq, d)
        else:
            v_ctx = modulate(rmsnorm(ctx, norm_pre_ffn[l]), csh2, csc2).reshape(-1, d)
            f_all = moe(jnp.concatenate([v_lat, v_ctx], axis=0), *moe_w)
            f_lat = f_all[:bsz * seq].reshape(bsz, seq, d)
            f_ctx = f_all[bsz * seq:].reshape(ctx.shape)
            ctx = ctx + cg2 * rmsnorm(f_ctx, norm_post_ffn[l])
        x = x + g2 * rmsnorm(f_lat, norm_post_ffn[l])
    return x
```

```python
import functools

import numpy as np
import jax
import jax.numpy as jnp
from jax import lax
from jax.experimental import pallas as pl
from jax.experimental.pallas import tpu as pltpu

F32 = jnp.float32
BF16 = jnp.bfloat16
HIGHEST = lax.Precision.HIGHEST

RNN_HEADS = 4
CONV_W = 4
LRU_C = 8.0
POOL_WINDOWS = (2, 4, 8, 16)
GRID_W = 64
TOP_K = 4
SWIGLU_LIMIT = 7.0
SWIGLU_ALPHA = 1.702
EPS = 1e-6

SUBLANES = 8
LANES = 128
VMEM_LIMIT = 56 * 1024 * 1024

TT = 256
POOL_CHUNK = 256
TM = 256
TD = 128
PAD_BLOCK = 128


def _sigmoid(x):
    return 1.0 / (1.0 + jnp.exp(-x))


def _softplus(z):
    return jnp.maximum(z, 0.0) + jnp.log1p(jnp.exp(-jnp.abs(z)))


def _rmsnorm(x, g):
    ms = jnp.mean(x * x, axis=-1, keepdims=True)
    return (x * lax.rsqrt(ms + EPS)) * g


def _modulate(u, shift, scale):
    return u * (1.0 + scale) + shift


def _dot(a, b):
    return jnp.dot(a, b, preferred_element_type=F32)


def _conv(prev8, xr, next8, cw, cb):
    t = xr.shape[0]
    ext = jnp.concatenate([prev8, xr, next8], axis=0)
    acc = cb + cw[0:1, :] * ext[SUBLANES - 2:SUBLANES - 2 + t, :]
    for k in range(1, CONV_W):
        off = SUBLANES - 2 + k
        acc = acc + cw[k:k + 1, :] * ext[off:off + t, :]
    return acc


def _lru_coeffs(xc, wg_ref, ba, bx, sp):
    hd = xc.shape[1] // RNN_HEADS
    xcb = xc.astype(BF16)
    a_parts, b_parts = [], []
    for h in range(RNN_HEADS):
        cols = slice(h * hd, (h + 1) * hd)
        z = _dot(xcb[:, cols], wg_ref[h])
        r = _sigmoid(z[:, :hd] + ba[:, cols])
        i = _sigmoid(z[:, hd:] + bx[:, cols])
        a = jnp.exp((-LRU_C) * r * sp[:, cols])
        b = jnp.sqrt(1.0 - a * a) * (i * xc[:, cols])
        a_parts.append(a)
        b_parts.append(b)
    return jnp.concatenate(a_parts, axis=1), jnp.concatenate(b_parts, axis=1)


def _scan_tile(a, b, h_in, reverse, store):
    t, c = a.shape
    g = t // SUBLANES
    a3 = a.reshape(g, SUBLANES, c)
    b3 = b.reshape(g, SUBLANES, c)
    row = lax.broadcasted_iota(jnp.int32, (g, SUBLANES, c), 1)
    for s in (1, 2, 4):
        if reverse:
            shift, m = SUBLANES - s, row < SUBLANES - s
        else:
            shift, m = s, row >= s
        ra = pltpu.roll(a3, shift, axis=1)
        rb = pltpu.roll(b3, shift, axis=1)
        b3 = a3 * jnp.where(m, rb, 0.0) + b3
        a3 = a3 * jnp.where(m, ra, 1.0)
    h = h_in
    order = range(g - 1, -1, -1) if reverse else range(g)
    for gi in order:
        hg = a3[gi] * h + b3[gi]
        store(gi, hg)
        h = hg[0:1, :] if reverse else hg[SUBLANES - 1:SUBLANES, :]
    return h


def _mod_body(c_ref, w_ref, b_ref, o_ref):
    c = c_ref[...]
    s = c * _sigmoid(c)
    o_ref[...] = jnp.dot(s, w_ref[...], preferred_element_type=F32, precision=HIGHEST) + b_ref[...]


def _mod_call(cc, w_mod, b_mod):
    d = cc.shape[1]
    n = w_mod.shape[1] // d
    return pl.pallas_call(
        _mod_body,
        out_shape=jax.ShapeDtypeStruct((cc.shape[0], n * d), F32),
        grid=(n,),
        in_specs=[pl.BlockSpec(cc.shape, lambda j: (0, 0)),
                  pl.BlockSpec((d, d), lambda j: (0, j)),
                  pl.BlockSpec((1, d), lambda j: (0, j))],
        out_specs=pl.BlockSpec((cc.shape[0], d), lambda j: (0, j)),
        name="mod",
    )(cc, w_mod, b_mod)


def _ctx_body(ctx_ref, g_ref, sh_ref, sc_ref, w_ref, cw_ref, cb_ref, wgf_ref, wgb_ref,
              ba_ref, bx_ref, lam_ref, hf_ref, hb_ref):
    d = ctx_ref.shape[2]
    u = _modulate(_rmsnorm(ctx_ref[0], g_ref[...]), sh_ref[...], sc_ref[...])
    xr = _dot(u.astype(BF16), w_ref[...])
    z8 = jnp.zeros((SUBLANES, d), F32)
    xc = _conv(z8, xr, z8, cw_ref[...], cb_ref[...])
    h0 = jnp.zeros((1, d), F32)
    for di, (wg_ref, out_ref) in enumerate(((wgf_ref, hf_ref), (wgb_ref, hb_ref))):
        sp = _softplus(-lam_ref[di:di + 1, :])
        a, b = _lru_coeffs(xc, wg_ref, ba_ref[di:di + 1, :], bx_ref[di:di + 1, :], sp)
        out_ref[0] = _scan_tile(a, b, h0, di == 1, lambda gi, hg: None)


def _ctx_call(ctx, g, sh, sc, w_in_b, cw, cb, wgf, wgb, ba, bx, lam):
    bsz, tc, d = ctx.shape
    full = lambda shp: pl.BlockSpec(shp, lambda b: (0,) * len(shp))
    return pl.pallas_call(
        _ctx_body,
        out_shape=(jax.ShapeDtypeStruct((bsz, 1, d), F32),) * 2,
        grid=(bsz,),
        in_specs=[pl.BlockSpec((1, tc, d), lambda b: (b, 0, 0)),
                  full((1, d)), full((1, d)), full((1, d)),
                  pl.BlockSpec((d, d), lambda b: (0, 0)),
                  full(cw.shape), full((1, d)), full(wgf.shape), full(wgb.shape),
                  full(ba.shape), full(bx.shape), full(lam.shape)],
        out_specs=(pl.BlockSpec((1, 1, d), lambda b: (b, 0, 0)),) * 2,
        compiler_params=pltpu.CompilerParams(vmem_limit_bytes=VMEM_LIMIT),
        name="ctx",
    )(ctx, g, sh, sc, w_in_b, cw, cb, wgf, wgb, ba, bx, lam)


def _mix_fwd_body(x_ref, xn_ref, g_ref, sh_ref, sc_ref, win_ref, cw_ref, cb_ref, wgf_ref,
                  ba_ref, bx_ref, lam_ref, h0_ref,
                  hf_ref, xc_ref, gg_ref, xp_ref, mr_ref, mp_ref, tail_s, h_s):
    i = pl.program_id(1)
    last = pl.num_programs(1) - 1
    tt, d = x_ref.shape[1], x_ref.shape[2]

    @pl.when(i == 0)
    def _():
        tail_s[...] = jnp.zeros_like(tail_s)
        h_s[...] = h0_ref[0]

    g, sh, sc = g_ref[...], sh_ref[0], sc_ref[0]
    ub = _modulate(_rmsnorm(x_ref[0], g), sh, sc).astype(BF16)
    unb = _modulate(_rmsnorm(xn_ref[0], g), sh, sc).astype(BF16)
    w_rnn = win_ref[:, 0:d]
    xr = _dot(ub, w_rnn)
    xrn = jnp.where(i == last, 0.0, _dot(unb, w_rnn))
    xc = _conv(tail_s[...], xr, xrn, cw_ref[...], cb_ref[...])
    tail_s[...] = xr[tt - SUBLANES:tt, :]
    xc_ref[0] = xc.astype(BF16)
    for j, ref in enumerate((gg_ref, xp_ref, mr_ref, mp_ref)):
        ref[0] = _dot(ub, win_ref[:, (j + 1) * d:(j + 2) * d]).astype(BF16)

    sp = _softplus(-lam_ref[0:1, :])
    a, b = _lru_coeffs(xc, wgf_ref, ba_ref[0:1, :], bx_ref[0:1, :], sp)

    def store(gi, hg):
        hf_ref[0, gi * SUBLANES:(gi + 1) * SUBLANES, :] = hg

    h_s[...] = _scan_tile(a, b, h_s[...], False, store)


def _mix_fwd_call(x, g, sh, sc, w_in_b, cw, cb, wgf, ba, bx, lam, h0f):
    bsz, t, d = x.shape
    nt = t // TT
    nblk8 = t // SUBLANES
    per_b = pl.BlockSpec((1, 1, d), lambda b, i: (b, 0, 0))
    full = lambda shp: pl.BlockSpec(shp, lambda b, i: (0,) * len(shp))
    tile = pl.BlockSpec((1, TT, d), lambda b, i: (b, i, 0))
    return pl.pallas_call(
        _mix_fwd_body,
        out_shape=(jax.ShapeDtypeStruct((bsz, t, d), F32),) + (jax.ShapeDtypeStruct((bsz, t, d), BF16),) * 5,
        grid=(bsz, nt),
        in_specs=[tile,
                  pl.BlockSpec((1, SUBLANES, d),
                               lambda b, i: (b, jnp.minimum((i + 1) * (TT // SUBLANES), nblk8 - 1), 0)),
                  full((1, d)), per_b, per_b,
                  full(w_in_b.shape), full(cw.shape), full((1, d)), full(wgf.shape),
                  full(ba.shape), full(bx.shape), full(lam.shape), per_b],
        out_specs=(tile,) * 6,
        scratch_shapes=[pltpu.VMEM((SUBLANES, d), F32), pltpu.VMEM((1, d), F32)],
        compiler_params=pltpu.CompilerParams(
            dimension_semantics=("arbitrary", "arbitrary"), vmem_limit_bytes=VMEM_LIMIT),
        name="mix_fwd",
    )(x, x, g, sh, sc, w_in_b, cw, cb, wgf, ba, bx, lam, h0f)


def _pool_consts():
    p = np.arange(POOL_CHUNK)
    pos, line = p % GRID_W, p // GRID_W
    mats, cnts = [], []
    for w in POOL_WINDOWS:
        lo = np.clip(pos - w // 2, 0, GRID_W)
        hi = np.clip(pos + w - w // 2, 0, GRID_W)
        m = (line[:, None] == line[None, :]) & (pos[None, :] >= lo[:, None]) & (pos[None, :] < hi[:, None])
        mats.append(m.astype(np.float32))
        cnts.append((hi - lo).astype(np.float32)[:, None])
    return jnp.asarray(np.stack(mats), BF16), jnp.asarray(np.stack(cnts), F32)


def _mix_bwd_body(hf_ref, xc_ref, gg_ref, xp_ref, mr_ref, mp_ref, x_ref,
                  wgb_ref, ba_ref, bx_ref, lam_ref, h0_ref,
                  pm_ref, pc_ref, pw_ref, ps_ref, wr_ref, wp_ref, wo_ref,
                  npost_ref, npre_ref, g1_ref, sh2_ref, sc2_ref, rw_ref, rb_ref,
                  xmid_ref, v_ref, idx_ref, wts_ref, rank_ref, cnt_ref,
                  hb_s, h_s, cnt_s):
    b_id, i = pl.program_id(0), pl.program_id(1)
    tt, d = x_ref.shape[1], x_ref.shape[2]
    n_exp = rw_ref.shape[1]

    @pl.when(i == 0)
    def _():
        h_s[...] = h0_ref[0]

    @pl.when((i == 0) & (b_id == 0))
    def _():
        cnt_s[...] = jnp.zeros_like(cnt_s)

    xc = xc_ref[0].astype(F32)
    sp = _softplus(-lam_ref[1:2, :])
    a, b = _lru_coeffs(xc, wgb_ref, ba_ref[1:2, :], bx_ref[1:2, :], sp)

    def store(gi, hg):
        hb_s[gi * SUBLANES:(gi + 1) * SUBLANES, :] = hg

    h_s[...] = _scan_tile(a, b, h_s[...], True, store)

    gg = gg_ref[0].astype(F32)
    gelu = 0.5 * gg * (1.0 + jnp.tanh(0.7978845608028654 * (gg + 0.044715 * gg * gg * gg)))
    y_rnn = (hf_ref[0] + hb_s[...]) * gelu

    xpb = xp_ref[0]
    grp = d // len(POOL_WINDOWS)
    y_parts = []
    for gi in range(len(POOL_WINDOWS)):
        cols = slice(gi * grp, (gi + 1) * grp)
        rows = []
        for c0 in range(0, tt, POOL_CHUNK):
            xg = xpb[c0:c0 + POOL_CHUNK, cols]
            mean = _dot(pm_ref[gi], xg) / pc_ref[gi]
            rows.append((mean - xg.astype(F32)).astype(BF16))
        dg = rows[0] if len(rows) == 1 else jnp.concatenate(rows, axis=0)
        y_parts.append(_dot(dg, pw_ref[gi]) * ps_ref[:, cols])
    y_pool = jnp.concatenate(y_parts, axis=1)

    merged = (_sigmoid(mr_ref[0].astype(F32)) * _dot(y_rnn.astype(BF16), wr_ref[...])
              + _sigmoid(mp_ref[0].astype(F32)) * _dot(y_pool.astype(BF16), wp_ref[...]))
    m_lat = _dot(merged.astype(BF16), wo_ref[...])
    x_mid = x_ref[0] + g1_ref[0] * _rmsnorm(m_lat, npost_ref[...])
    xmid_ref[0] = x_mid
    v = _modulate(_rmsnorm(x_mid, npre_ref[...]), sh2_ref[0], sc2_ref[0])
    v_ref[0] = v

    logits = jnp.dot(v, rw_ref[...], preferred_element_type=F32, precision=HIGHEST) + rb_ref[...]
    lane = lax.broadcasted_iota(jnp.int32, (tt, n_exp), 1).astype(F32)
    work = logits
    vals, idxs, sels = [], [], []
    for _ in range(TOP_K):
        m = jnp.max(work, axis=-1, keepdims=True)
        idx = jnp.min(jnp.where(work == m, lane, float(n_exp)), axis=-1, keepdims=True)
        sel = lane == idx
        vals.append(m)
        idxs.append(idx)
        sels.append(sel)
        work = jnp.where(sel, -jnp.inf, work)
    exps = [jnp.exp(vk - vals[0]) for vk in vals]
    den = exps[0] + exps[1] + exps[2] + exps[3]
    anyf = jnp.zeros((tt, n_exp), F32)
    for sel in sels:
        anyf = anyf + sel.astype(F32)
    r_i = lax.broadcasted_iota(jnp.int32, (tt, tt), 0)
    c_i = lax.broadcasted_iota(jnp.int32, (tt, tt), 1)
    lower = (c_i < r_i).astype(BF16)
    before = _dot(lower, anyf.astype(BF16)) + cnt_s[...]
    cnt_new = cnt_s[...] + jnp.sum(anyf, axis=0, keepdims=True)
    cnt_s[...] = cnt_new
    cnt_ref[...] = cnt_new

    lane_o = lax.broadcasted_iota(jnp.int32, (tt, LANES), 1)
    idx_o = jnp.zeros((tt, LANES), jnp.int32)
    wts_o = jnp.zeros((tt, LANES), F32)
    rank_o = jnp.zeros((tt, LANES), jnp.int32)
    for k in range(TOP_K):
        rk = jnp.sum(jnp.where(sels[k], before, 0.0), axis=-1, keepdims=True)
        idx_o = jnp.where(lane_o == k, idxs[k].astype(jnp.int32), idx_o)
        wts_o = jnp.where(lane_o == k, exps[k] / den, wts_o)
        rank_o = jnp.where(lane_o == k, rk.astype(jnp.int32), rank_o)
    idx_ref[0] = idx_o
    wts_ref[0] = wts_o
    rank_ref[0] = rank_o


def _mix_bwd_call(hf, xc, gg, xp, mr, mp, x, wgb, ba, bx, lam, h0b, pm, pc, pw, ps, wr, wp, wo,
                  npost, npre, g1, sh2, sc2, rw, rb):
    bsz, t, d = x.shape
    nt = t // TT
    n_exp = rw.shape[1]
    rev = lambda b, i: (b, nt - 1 - i, 0)
    tile = pl.BlockSpec((1, TT, d), rev)
    small = pl.BlockSpec((1, TT, LANES), rev)
    per_b = pl.BlockSpec((1, 1, d), lambda b, i: (b, 0, 0))
    full = lambda shp: pl.BlockSpec(shp, lambda b, i: (0,) * len(shp))
    return pl.pallas_call(
        _mix_bwd_body,
        out_shape=(jax.ShapeDtypeStruct((bsz, t, d), F32), jax.ShapeDtypeStruct((bsz, t, d), F32),
                   jax.ShapeDtypeStruct((bsz, t, LANES), jnp.int32),
                   jax.ShapeDtypeStruct((bsz, t, LANES), F32),
                   jax.ShapeDtypeStruct((bsz, t, LANES), jnp.int32),
                   jax.ShapeDtypeStruct((1, n_exp), F32)),
        grid=(bsz, nt),
        in_specs=[tile] * 7 + [full(wgb.shape), full(ba.shape), full(bx.shape), full(lam.shape), per_b,
                               full(pm.shape), full(pc.shape), full(pw.shape), full((1, d)),
                               full((d, d)), full((d, d)), full((d, d)),
                               full((1, d)), full((1, d)), per_b, per_b, per_b,
                               full(rw.shape), full((1, n_exp))],
        out_specs=(tile, tile, small, small, small, full((1, n_exp))),
        scratch_shapes=[pltpu.VMEM((TT, d), F32), pltpu.VMEM((1, d), F32), pltpu.VMEM((1, n_exp), F32)],
        compiler_params=pltpu.CompilerParams(
            dimension_semantics=("arbitrary", "arbitrary"), vmem_limit_bytes=VMEM_LIMIT),
        name="mix_bwd",
    )(hf, xc, gg, xp, mr, mp, x, wgb, ba, bx, lam, h0b, pm, pc, pw, ps, wr, wp, wo,
      npost, npre, g1, sh2, sc2, rw, rb)


def _dispatch_body(n_pad_blocks, pos_ref, pad_ref, zero_ref, v_hbm, xs_hbm, sem):
    base = pl.program_id(0) * TD
    n_pad = pad_ref.shape[0]

    def row_copy(src_row, dst_row):
        return pltpu.make_async_copy(v_hbm.at[pl.ds(src_row, 1)], xs_hbm.at[pl.ds(dst_row, 1)], sem)

    def pad_copy(q):
        return pltpu.make_async_copy(zero_ref.at[pl.ds(0, 1)], xs_hbm.at[pl.ds(pad_ref[q], 1)], sem)

    def issue(t, carry):
        for k in range(TOP_K):
            row_copy(base + t, pos_ref[t * TOP_K + k]).start()
        return carry

    def issue_pad(q, carry):
        pad_copy(q).start()
        return carry

    has_pads = pl.program_id(0) < n_pad_blocks
    lax.fori_loop(0, TD, issue, 0)

    @pl.when(has_pads)
    def _():
        lax.fori_loop(0, n_pad, issue_pad, 0)

    def drain(t, carry):
        for k in range(TOP_K):
            row_copy(base + t, pos_ref[t * TOP_K + k]).wait()
        return carry

    def drain_pad(q, carry):
        pad_copy(q).wait()
        return carry

    lax.fori_loop(0, TD, drain, 0)

    @pl.when(has_pads)
    def _():
        lax.fori_loop(0, n_pad, drain_pad, 0)


def _dispatch_call(pos_flat, pad_slots, v2, n_slots):
    n, d = v2.shape
    steps = n // TD
    n_pad_blocks = pad_slots.shape[0] // PAD_BLOCK
    assert n_pad_blocks <= steps
    return pl.pallas_call(
        functools.partial(_dispatch_body, n_pad_blocks),
        out_shape=jax.ShapeDtypeStruct((n_slots, d), F32),
        grid=(steps,),
        in_specs=[pl.BlockSpec((TD * TOP_K,), lambda i: (i,), memory_space=pltpu.SMEM),
                  pl.BlockSpec((PAD_BLOCK,), lambda i: (jnp.minimum(i, n_pad_blocks - 1),),
                               memory_space=pltpu.SMEM),
                  pl.BlockSpec((SUBLANES, d), lambda i: (0, 0)),
                  pl.BlockSpec(memory_space=pl.ANY)],
        out_specs=pl.BlockSpec(memory_space=pl.ANY),
        scratch_shapes=[pltpu.SemaphoreType.DMA],
        compiler_params=pltpu.CompilerParams(dimension_semantics=("arbitrary",)),
        name="dispatch",
    )(pos_flat, pad_slots, jnp.zeros((SUBLANES, d), F32), v2)


def _expert_body(te_ref, blk_ref, nvalid_ref, xs_ref, w1_ref, b1_ref, w2_ref, b2_ref, y_ref, w1b_s, w2b_s):
    j = pl.program_id(0)
    ff = w2_ref.shape[1]
    prev = te_ref[jnp.maximum(j - 1, 0)]

    @pl.when((j == 0) | (te_ref[j] != prev))
    def _():
        w1b_s[...] = w1_ref[0].astype(BF16)
        w2b_s[...] = w2_ref[0].astype(BF16)

    @pl.when(j < nvalid_ref[0])
    def _():
        z = _dot(xs_ref[...].astype(BF16), w1b_s[...]) + b1_ref[0]
        glu = jnp.minimum(z[:, :ff], SWIGLU_LIMIT)
        lin = jnp.clip(z[:, ff:], -SWIGLU_LIMIT, SWIGLU_LIMIT)
        act = glu * _sigmoid(SWIGLU_ALPHA * glu) * (lin + 1.0)
        y_ref[...] = _dot(act.astype(BF16), w2b_s[...]) + b2_ref[0]

    @pl.when(j >= nvalid_ref[0])
    def _():
        y_ref[...] = jnp.zeros_like(y_ref)


def _expert_call(te, blk, nvalid, xs, w1, b1, w2, b2):
    n_slots, d = xs.shape
    n_exp, _, ff2 = w1.shape
    ff = w2.shape[1]
    grid_spec = pltpu.PrefetchScalarGridSpec(
        num_scalar_prefetch=3,
        grid=(n_slots // TM,),
        in_specs=[pl.BlockSpec((TM, d), lambda j, te, blk, nv: (blk[j], 0)),
                  pl.BlockSpec((1, d, ff2), lambda j, te, blk, nv: (te[j], 0, 0)),
                  pl.BlockSpec((1, 1, ff2), lambda j, te, blk, nv: (te[j], 0, 0)),
                  pl.BlockSpec((1, ff, d), lambda j, te, blk, nv: (te[j], 0, 0)),
                  pl.BlockSpec((1, 1, d), lambda j, te, blk, nv: (te[j], 0, 0))],
        out_specs=pl.BlockSpec((TM, d), lambda j, te, blk, nv: (j, 0)),
        scratch_shapes=[pltpu.VMEM((d, ff2), BF16), pltpu.VMEM((ff, d), BF16)],
    )
    return pl.pallas_call(
        _expert_body,
        out_shape=jax.ShapeDtypeStruct((n_slots, d), F32),
        grid_spec=grid_spec,
        compiler_params=pltpu.CompilerParams(
            dimension_semantics=("arbitrary",), vmem_limit_bytes=VMEM_LIMIT),
        name="experts",
    )(te, blk, nvalid, xs, w1, b1, w2, b2)


def _combine_body(pos_ref, wts_ref, xmid_ref, g2_ref, npost_ref, y_hbm, o_ref, buf, sem):
    def row_copy(t, k):
        return pltpu.make_async_copy(y_hbm.at[pl.ds(pos_ref[t * TOP_K + k], 1)],
                                     buf.at[k, pl.ds(t, 1)], sem)

    def issue(t, carry):
        for k in range(TOP_K):
            row_copy(t, k).start()
        return carry

    lax.fori_loop(0, TD, issue, 0)

    def drain(t, carry):
        for k in range(TOP_K):
            row_copy(t, k).wait()
        return carry

    lax.fori_loop(0, TD, drain, 0)

    w = wts_ref[...]
    f = w[:, 0:1] * buf[0]
    for k in range(1, TOP_K):
        f = f + w[:, k:k + 1] * buf[k]
    o_ref[...] = xmid_ref[...] + g2_ref[0] * _rmsnorm(f, npost_ref[...])


def _combine_call(pos_flat, wts, xmid2, g2, npost, ys, t_per_batch):
    n, d = xmid2.shape
    return pl.pallas_call(
        _combine_body,
        out_shape=jax.ShapeDtypeStruct((n, d), F32),
        grid=(n // TD,),
        in_specs=[pl.BlockSpec((TD * TOP_K,), lambda i: (i,), memory_space=pltpu.SMEM),
                  pl.BlockSpec((TD, LANES), lambda i: (i, 0)),
                  pl.BlockSpec((TD, d), lambda i: (i, 0)),
                  pl.BlockSpec((1, 1, d), lambda i: ((i * TD) // t_per_batch, 0, 0)),
                  pl.BlockSpec((1, d), lambda i: (0, 0)),
                  pl.BlockSpec(memory_space=pl.ANY)],
        out_specs=pl.BlockSpec((TD, d), lambda i: (i, 0)),
        scratch_shapes=[pltpu.VMEM((TOP_K, TD, d), F32), pltpu.SemaphoreType.DMA],
        compiler_params=pltpu.CompilerParams(dimension_semantics=("arbitrary",)),
        name="combine",
    )(pos_flat, wts, xmid2, g2, npost, ys)


def _count_le(ends, q):
    return jnp.sum((ends[None, :] <= q[:, None]).astype(jnp.int32), axis=1)


def _gate_weights(wa, wx):
    return jnp.concatenate([wa, wx], axis=-1).astype(BF16)


def kernel(x, c, ctx, c_ctx, w_mod, b_mod, norm_pre_mix, norm_post_mix, norm_pre_ffn, norm_post_ffn, w_in, conv_w, conv_b, lru_wa, lru_ba, lru_wx, lru_bx, lru_lambda, pool_w, pool_scale, w_rnn_proj, w_pool_proj, w_out, router_w, router_b, exp_w1, exp_b1, exp_w2, exp_b2):
    bsz, t, d = x.shape
    assert w_mod.shape[0] == 1, "single-layer block"
    assert t % TT == 0 and TT % POOL_CHUNK == 0 and POOL_CHUNK % GRID_W == 0 and t % TD == 0
    n = bsz * t
    n_exp = router_w.shape[-1]
    assert (n * TOP_K) % TM == 0 and (n_exp * TM) % PAD_BLOCK == 0
    row = lambda a: a.reshape(1, -1)

    pad = jnp.zeros((SUBLANES - (bsz + 1) % SUBLANES, d), F32)
    cc = jnp.concatenate([c, c_ctx[None, :], pad], axis=0)
    mod = _mod_call(cc, w_mod[0], row(b_mod[0]))
    lat = lambda j: mod[:bsz, j * d:(j + 1) * d].reshape(bsz, 1, d)
    sh1, sc1, g1, sh2, sc2, g2 = (lat(j) for j in range(6))
    csh1, csc1 = mod[bsz:bsz + 1, 0:d], mod[bsz:bsz + 1, d:2 * d]

    w_in_b = w_in[0].astype(BF16)
    wgf = _gate_weights(lru_wa[0, 0], lru_wx[0, 0])
    wgb = _gate_weights(lru_wa[0, 1], lru_wx[0, 1])
    ba, bx, lam = lru_ba[0], lru_bx[0], lru_lambda[0]
    cw, cb = conv_w[0], row(conv_b[0])
    npre_mix = row(norm_pre_mix[0])

    h0f, h0b = _ctx_call(ctx, npre_mix, csh1, csc1, w_in_b, cw, cb, wgf, wgb, ba, bx, lam)

    hf, xc, gg, xp, mr, mp = _mix_fwd_call(x, npre_mix, sh1, sc1, w_in_b, cw, cb, wgf, ba, bx, lam, h0f)

    pm, pc = _pool_consts()
    xmid, v, idx_o, wts_o, rank_o, cnt = _mix_bwd_call(
        hf, xc, gg, xp, mr, mp, x, wgb, ba, bx, lam, h0b,
        pm, pc, pool_w[0].astype(BF16), row(pool_scale[0]),
        w_rnn_proj[0].astype(BF16), w_pool_proj[0].astype(BF16), w_out[0].astype(BF16),
        row(norm_post_mix[0]), row(norm_pre_ffn[0]), g1, sh2, sc2, router_w[0], row(router_b[0]))

    idx = idx_o.reshape(n, LANES)[:, :TOP_K]
    rank = rank_o.reshape(n, LANES)[:, :TOP_K]
    counts = cnt[0].astype(jnp.int32)
    tiles_e = (counts + TM - 1) // TM
    tile_end = jnp.cumsum(tiles_e)
    offs = (tile_end - tiles_e) * TM
    pos_flat = (offs[idx] + rank).reshape(n * TOP_K)
    n_tiles = (n * TOP_K) // TM + n_exp
    n_valid = tile_end[-1]
    jj = jnp.arange(n_tiles, dtype=jnp.int32)
    blk = jnp.minimum(jj, n_valid - 1)
    te = jnp.minimum(_count_le(tile_end, blk), n_exp - 1)

    pad_e = tiles_e * TM - counts
    pad_end = jnp.cumsum(pad_e)
    qq = jnp.arange(n_exp * TM, dtype=jnp.int32)
    e_q = jnp.minimum(_count_le(pad_end, qq), n_exp - 1)
    slot_in = (offs + counts)[e_q] + qq - (pad_end - pad_e)[e_q]
    slot_tail = n_valid * TM + qq - pad_end[-1]
    pad_slots = jnp.where(qq < pad_end[-1], slot_in, slot_tail).astype(jnp.int32)

    xs = _dispatch_call(pos_flat, pad_slots, v.reshape(n, d), n_tiles * TM)
    ys = _expert_call(te, blk, n_valid.reshape(1), xs, exp_w1[0], exp_b1[0].reshape(n_exp, 1, -1),
                      exp_w2[0], exp_b2[0].reshape(n_exp, 1, -1))
    out = _combine_call(pos_flat, wts_o.reshape(n, LANES), xmid.reshape(n, d), g2, row(norm_post_ffn[0]), ys, t)
    return out.reshape(bsz, t, d)
```

```python
import functools

import numpy as np
import jax
import jax.numpy as jnp
from jax import lax
from jax.experimental import pallas as pl
from jax.experimental.pallas import tpu as pltpu

F32 = jnp.float32
BF16 = jnp.bfloat16
HIGHEST = lax.Precision.HIGHEST

RNN_HEADS = 4
CONV_W = 4
LRU_C = 8.0
POOL_WINDOWS = (2, 4, 8, 16)
GRID_W = 64
TOP_K = 4
SWIGLU_LIMIT = 7.0
SWIGLU_ALPHA = 1.702
EPS = 1e-6

SUBLANES = 8
LANES = 128
VMEM_LIMIT = 56 * 1024 * 1024

TT = 256
POOL_CHUNK = 256
TM = 256
TD = 128
PAD_BLOCK = 128


def _sigmoid(x):
    return 1.0 / (1.0 + jnp.exp(-x))


def _softplus(z):
    return jnp.maximum(z, 0.0) + jnp.log1p(jnp.exp(-jnp.abs(z)))


def _rmsnorm(x, g):
    ms = jnp.mean(x * x, axis=-1, keepdims=True)
    return (x * lax.rsqrt(ms + EPS)) * g


def _modulate(u, shift, scale):
    return u * (1.0 + scale) + shift


def _dot(a, b):
    return jnp.dot(a, b, preferred_element_type=F32)


def _conv(prev8, xr, next8, cw, cb):
    t = xr.shape[0]
    ext = jnp.concatenate([prev8, xr, next8], axis=0)
    acc = cb + cw[0:1, :] * ext[SUBLANES - 2:SUBLANES - 2 + t, :]
    for k in range(1, CONV_W):
        off = SUBLANES - 2 + k
        acc = acc + cw[k:k + 1, :] * ext[off:off + t, :]
    return acc


def _lru_coeffs(xc, wg_ref, ba, bx, sp):
    hd = xc.shape[1] // RNN_HEADS
    xcb = xc.astype(BF16)
    a_parts, b_parts = [], []
    for h in range(RNN_HEADS):
        cols = slice(h * hd, (h + 1) * hd)
        z = _dot(xcb[:, cols], wg_ref[h])
        r = _sigmoid(z[:, :hd] + ba[:, cols])
        i = _sigmoid(z[:, hd:] + bx[:, cols])
        a = jnp.exp((-LRU_C) * r * sp[:, cols])
        b = jnp.sqrt(1.0 - a * a) * (i * xc[:, cols])
        a_parts.append(a)
        b_parts.append(b)
    return jnp.concatenate(a_parts, axis=1), jnp.concatenate(b_parts, axis=1)


def _scan_tile(a, b, h_in, reverse, store):
    t, c = a.shape
    g = t // SUBLANES
    a3 = a.reshape(g, SUBLANES, c)
    b3 = b.reshape(g, SUBLANES, c)
    row = lax.broadcasted_iota(jnp.int32, (g, SUBLANES, c), 1)
    for s in (1, 2, 4):
        if reverse:
            shift, m = SUBLANES - s, row < SUBLANES - s
        else:
            shift, m = s, row >= s
        ra = pltpu.roll(a3, shift, axis=1)
        rb = pltpu.roll(b3, shift, axis=1)
        b3 = a3 * jnp.where(m, rb, 0.0) + b3
        a3 = a3 * jnp.where(m, ra, 1.0)
    h = h_in
    order = range(g - 1, -1, -1) if reverse else range(g)
    for gi in order:
        hg = a3[gi] * h + b3[gi]
        store(gi, hg)
        h = hg[0:1, :] if reverse else hg[SUBLANES - 1:SUBLANES, :]
    return h


def _mod_body(c_ref, w_ref, b_ref, o_ref):
    c = c_ref[...]
    s = c * _sigmoid(c)
    o_ref[...] = jnp.dot(s, w_ref[...], preferred_element_type=F32, precision=HIGHEST) + b_ref[...]


def _mod_call(cc, w_mod, b_mod):
    d = cc.shape[1]
    n = w_mod.shape[1] // d
    return pl.pallas_call(
        _mod_body,
        out_shape=jax.ShapeDtypeStruct((cc.shape[0], n * d), F32),
        grid=(n,),
        in_specs=[pl.BlockSpec(cc.shape, lambda j: (0, 0)),
                  pl.BlockSpec((d, d), lambda j: (0, j)),
                  pl.BlockSpec((1, d), lambda j: (0, j))],
        out_specs=pl.BlockSpec((cc.shape[0], d), lambda j: (0, j)),
        name="mod",
    )(cc, w_mod, b_mod)


def _ctx_body(ctx_ref, g_ref, sh_ref, sc_ref, w_ref, cw_ref, cb_ref, wgf_ref, wgb_ref,
              ba_ref, bx_ref, lam_ref, hf_ref, hb_ref):
    d = ctx_ref.shape[2]
    u = _modulate(_rmsnorm(ctx_ref[0], g_ref[...]), sh_ref[...], sc_ref[...])
    xr = _dot(u.astype(BF16), w_ref[...])
    z8 = jnp.zeros((SUBLANES, d), F32)
    xc = _conv(z8, xr, z8, cw_ref[...], cb_ref[...])
    h0 = jnp.zeros((1, d), F32)
    for di, (wg_ref, out_ref) in enumerate(((wgf_ref, hf_ref), (wgb_ref, hb_ref))):
        sp = _softplus(-lam_ref[di:di + 1, :])
        a, b = _lru_coeffs(xc, wg_ref, ba_ref[di:di + 1, :], bx_ref[di:di + 1, :], sp)
        out_ref[0] = _scan_tile(a, b, h0, di == 1, lambda gi, hg: None)


def _ctx_call(ctx, g, sh, sc, w_in_b, cw, cb, wgf, wgb, ba, bx, lam):
    bsz, tc, d = ctx.shape
    full = lambda shp: pl.BlockSpec(shp, lambda b: (0,) * len(shp))
    return pl.pallas_call(
        _ctx_body,
        out_shape=(jax.ShapeDtypeStruct((bsz, 1, d), F32),) * 2,
        grid=(bsz,),
        in_specs=[pl.BlockSpec((1, tc, d), lambda b: (b, 0, 0)),
                  full((1, d)), full((1, d)), full((1, d)),
                  pl.BlockSpec((d, d), lambda b: (0, 0)),
                  full(cw.shape), full((1, d)), full(wgf.shape), full(wgb.shape),
                  full(ba.shape), full(bx.shape), full(lam.shape)],
        out_specs=(pl.BlockSpec((1, 1, d), lambda b: (b, 0, 0)),) * 2,
        compiler_params=pltpu.CompilerParams(vmem_limit_bytes=VMEM_LIMIT),
        name="ctx",
    )(ctx, g, sh, sc, w_in_b, cw, cb, wgf, wgb, ba, bx, lam)


def _mix_fwd_body(x_ref, xn_ref, g_ref, sh_ref, sc_ref, win_ref, cw_ref, cb_ref, wgf_ref,
                  ba_ref, bx_ref, lam_ref, h0_ref,
                  hf_ref, xc_ref, gg_ref, xp_ref, mr_ref, mp_ref, tail_s, h_s):
    i = pl.program_id(1)
    last = pl.num_programs(1) - 1
    tt, d = x_ref.shape[1], x_ref.shape[2]

    @pl.when(i == 0)
    def _():
        tail_s[...] = jnp.zeros_like(tail_s)
        h_s[...] = h0_ref[0]

    g, sh, sc = g_ref[...], sh_ref[0], sc_ref[0]
    ub = _modulate(_rmsnorm(x_ref[0], g), sh, sc).astype(BF16)
    unb = _modulate(_rmsnorm(xn_ref[0], g), sh, sc).astype(BF16)
    w_rnn = win_ref[:, 0:d]
    xr = _dot(ub, w_rnn)
    xrn = jnp.where(i == last, 0.0, _dot(unb, w_rnn))
    xc = _conv(tail_s[...], xr, xrn, cw_ref[...], cb_ref[...])
    tail_s[...] = xr[tt - SUBLANES:tt, :]
    xc_ref[0] = xc.astype(BF16)
    for j, ref in enumerate((gg_ref, xp_ref, mr_ref, mp_ref)):
        ref[0] = _dot(ub, win_ref[:, (j + 1) * d:(j + 2) * d]).astype(BF16)

    sp = _softplus(-lam_ref[0:1, :])
    a, b = _lru_coeffs(xc, wgf_ref, ba_ref[0:1, :], bx_ref[0:1, :], sp)

    def store(gi, hg):
        hf_ref[0, gi * SUBLANES:(gi + 1) * SUBLANES, :] = hg

    h_s[...] = _scan_tile(a, b, h_s[...], False, store)


def _mix_fwd_call(x, g, sh, sc, w_in_b, cw, cb, wgf, ba, bx, lam, h0f):
    bsz, t, d = x.shape
    nt = t // TT
    nblk8 = t // SUBLANES
    per_b = pl.BlockSpec((1, 1, d), lambda b, i: (b, 0, 0))
    full = lambda shp: pl.BlockSpec(shp, lambda b, i: (0,) * len(shp))
    tile = pl.BlockSpec((1, TT, d), lambda b, i: (b, i, 0))
    return pl.pallas_call(
        _mix_fwd_body,
        out_shape=(jax.ShapeDtypeStruct((bsz, t, d), F32),) + (jax.ShapeDtypeStruct((bsz, t, d), BF16),) * 5,
        grid=(bsz, nt),
        in_specs=[tile,
                  pl.BlockSpec((1, SUBLANES, d),
                               lambda b, i: (b, jnp.minimum((i + 1) * (TT // SUBLANES), nblk8 - 1), 0)),
                  full((1, d)), per_b, per_b,
                  full(w_in_b.shape), full(cw.shape), full((1, d)), full(wgf.shape),
                  full(ba.shape), full(bx.shape), full(lam.shape), per_b],
        out_specs=(tile,) * 6,
        scratch_shapes=[pltpu.VMEM((SUBLANES, d), F32), pltpu.VMEM((1, d), F32)],
        compiler_params=pltpu.CompilerParams(
            dimension_semantics=("arbitrary", "arbitrary"), vmem_limit_bytes=VMEM_LIMIT),
        name="mix_fwd",
    )(x, x, g, sh, sc, w_in_b, cw, cb, wgf, ba, bx, lam, h0f)


def _pool_consts():
    p = np.arange(POOL_CHUNK)
    pos, line = p % GRID_W, p // GRID_W
    mats, cnts = [], []
    for w in POOL_WINDOWS:
        lo = np.clip(pos - w // 2, 0, GRID_W)
        hi = np.clip(pos + w - w // 2, 0, GRID_W)
        m = (line[:, None] == line[None, :]) & (pos[None, :] >= lo[:, None]) & (pos[None, :] < hi[:, None])
        mats.append(m.astype(np.float32))
        cnts.append((hi - lo).astype(np.float32)[:, None])
    return jnp.asarray(np.stack(mats), BF16), jnp.asarray(np.stack(cnts), F32)


def _mix_bwd_body(hf_ref, xc_ref, gg_ref, xp_ref, mr_ref, mp_ref, x_ref,
                  wgb_ref, ba_ref, bx_ref, lam_ref, h0_ref,
                  pm_ref, pc_ref, pw_ref, ps_ref, wr_ref, wp_ref, wo_ref,
                  npost_ref, npre_ref, g1_ref, sh2_ref, sc2_ref, rw_ref, rb_ref,
                  xmid_ref, v_ref, idx_ref, wts_ref, rank_ref, cnt_ref,
                  hb_s, h_s, cnt_s):
    b_id, i = pl.program_id(0), pl.program_id(1)
    tt, d = x_ref.shape[1], x_ref.shape[2]
    n_exp = rw_ref.shape[1]

    @pl.when(i == 0)
    def _():
        h_s[...] = h0_ref[0]

    @pl.when((i == 0) & (b_id == 0))
    def _():
        cnt_s[...] = jnp.zeros_like(cnt_s)

    xc = xc_ref[0].astype(F32)
    sp = _softplus(-lam_ref[1:2, :])
    a, b = _lru_coeffs(xc, wgb_ref, ba_ref[1:2, :], bx_ref[1:2, :], sp)

    def store(gi, hg):
        hb_s[gi * SUBLANES:(gi + 1) * SUBLANES, :] = hg

    h_s[...] = _scan_tile(a, b, h_s[...], True, store)

    gg = gg_ref[0].astype(F32)
    gelu = 0.5 * gg * (1.0 + jnp.tanh(0.7978845608028654 * (gg + 0.044715 * gg * gg * gg)))
    y_rnn = (hf_ref[0] + hb_s[...]) * gelu

    xpb = xp_ref[0]
    grp = d // len(POOL_WINDOWS)
    y_parts = []
    for gi in range(len(POOL_WINDOWS)):
        cols = slice(gi * grp, (gi + 1) * grp)
        rows = []
        for c0 in range(0, tt, POOL_CHUNK):
            xg = xpb[c0:c0 + POOL_CHUNK, cols]
            mean = _dot(pm_ref[gi], xg) / pc_ref[gi]
            rows.append((mean - xg.astype(F32)).astype(BF16))
        dg = rows[0] if len(rows) == 1 else jnp.concatenate(rows, axis=0)
        y_parts.append(_dot(dg, pw_ref[gi]) * ps_ref[:, cols])
    y_pool = jnp.concatenate(y_parts, axis=1)

    merged = (_sigmoid(mr_ref[0].astype(F32)) * _dot(y_rnn.astype(BF16), wr_ref[...])
              + _sigmoid(mp_ref[0].astype(F32)) * _dot(y_pool.astype(BF16), wp_ref[...]))
    m_lat = _dot(merged.astype(BF16), wo_ref[...])
    x_mid = x_ref[0] + g1_ref[0] * _rmsnorm(m_lat, npost_ref[...])
    xmid_ref[0] = x_mid
    v = _modulate(_rmsnorm(x_mid, npre_ref[...]), sh2_ref[0], sc2_ref[0])
    v_ref[0] = v

    logits = jnp.dot(v, rw_ref[...], preferred_element_type=F32, precision=HIGHEST) + rb_ref[...]
    lane = lax.broadcasted_iota(jnp.int32, (tt, n_exp), 1).astype(F32)
    work = logits
    vals, idxs, sels = [], [], []
    for _ in range(TOP_K):
        m = jnp.max(work, axis=-1, keepdims=True)
        idx = jnp.min(jnp.where(work == m, lane, float(n_exp)), axis=-1, keepdims=True)
        sel = lane == idx
        vals.append(m)
        idxs.append(idx)
        sels.append(sel)
        work = jnp.where(sel, -jnp.inf, work)
    exps = [jnp.exp(vk - vals[0]) for vk in vals]
    den = exps[0] + exps[1] + exps[2] + exps[3]
    anyf = jnp.zeros((tt, n_exp), F32)
    for sel in sels:
        anyf = anyf + sel.astype(F32)
    r_i = lax.broadcasted_iota(jnp.int32, (tt, tt), 0)
    c_i = lax.broadcasted_iota(jnp.int32, (tt, tt), 1)
    lower = (c_i < r_i).astype(BF16)
    before = _dot(lower, anyf.astype(BF16)) + cnt_s[...]
    cnt_new = cnt_s[...] + jnp.sum(anyf, axis=0, keepdims=True)
    cnt_s[...] = cnt_new
    cnt_ref[...] = cnt_new

    lane_o = lax.broadcasted_iota(jnp.int32, (tt, LANES), 1)
    idx_o = jnp.zeros((tt, LANES), jnp.int32)
    wts_o = jnp.zeros((tt, LANES), F32)
    rank_o = jnp.zeros((tt, LANES), jnp.int32)
    for k in range(TOP_K):
        rk = jnp.sum(jnp.where(sels[k], before, 0.0), axis=-1, keepdims=True)
        idx_o = jnp.where(lane_o == k, idxs[k].astype(jnp.int32), idx_o)
        wts_o = jnp.where(lane_o == k, exps[k] / den, wts_o)
        rank_o = jnp.where(lane_o == k, rk.astype(jnp.int32), rank_o)
    idx_ref[0] = idx_o
    wts_ref[0] = wts_o
    rank_ref[0] = rank_o


def _mix_bwd_call(hf, xc, gg, xp, mr, mp, x, wgb, ba, bx, lam, h0b, pm, pc, pw, ps, wr, wp, wo,
                  npost, npre, g1, sh2, sc2, rw, rb):
    bsz, t, d = x.shape
    nt = t // TT
    n_exp = rw.shape[1]
    rev = lambda b, i: (b, nt - 1 - i, 0)
    tile = pl.BlockSpec((1, TT, d), rev)
    small = pl.BlockSpec((1, TT, LANES), rev)
    per_b = pl.BlockSpec((1, 1, d), lambda b, i: (b, 0, 0))
    full = lambda shp: pl.BlockSpec(shp, lambda b, i: (0,) * len(shp))
    return pl.pallas_call(
        _mix_bwd_body,
        out_shape=(jax.ShapeDtypeStruct((bsz, t, d), F32), jax.ShapeDtypeStruct((bsz, t, d), F32),
                   jax.ShapeDtypeStruct((bsz, t, LANES), jnp.int32),
                   jax.ShapeDtypeStruct((bsz, t, LANES), F32),
                   jax.ShapeDtypeStruct((bsz, t, LANES), jnp.int32),
                   jax.ShapeDtypeStruct((1, n_exp), F32)),
        grid=(bsz, nt),
        in_specs=[tile] * 7 + [full(wgb.shape), full(ba.shape), full(bx.shape), full(lam.shape), per_b,
                               full(pm.shape), full(pc.shape), full(pw.shape), full((1, d)),
                               full((d, d)), full((d, d)), full((d, d)),
                               full((1, d)), full((1, d)), per_b, per_b, per_b,
                               full(rw.shape), full((1, n_exp))],
        out_specs=(tile, tile, small, small, small, full((1, n_exp))),
        scratch_shapes=[pltpu.VMEM((TT, d), F32), pltpu.VMEM((1, d), F32), pltpu.VMEM((1, n_exp), F32)],
        compiler_params=pltpu.CompilerParams(
            dimension_semantics=("arbitrary", "arbitrary"), vmem_limit_bytes=VMEM_LIMIT),
        name="mix_bwd",
    )(hf, xc, gg, xp, mr, mp, x, wgb, ba, bx, lam, h0b, pm, pc, pw, ps, wr, wp, wo,
      npost, npre, g1, sh2, sc2, rw, rb)


def _dispatch_body(n_pad_blocks, pos_ref, pad_ref, zero_ref, v_ref, xs_hbm, sem):
    n_pad = pad_ref.shape[0]

    def row_copy(src_row, dst_row):
        return pltpu.make_async_copy(v_ref.at[pl.ds(src_row, 1)], xs_hbm.at[pl.ds(dst_row, 1)], sem)

    def pad_copy(q):
        return pltpu.make_async_copy(zero_ref.at[pl.ds(0, 1)], xs_hbm.at[pl.ds(pad_ref[q], 1)], sem)

    def issue(t, carry):
        for k in range(TOP_K):
            row_copy(t, pos_ref[t * TOP_K + k]).start()
        return carry

    def issue_pad(q, carry):
        pad_copy(q).start()
        return carry

    has_pads = pl.program_id(0) < n_pad_blocks
    lax.fori_loop(0, TD, issue, 0)

    @pl.when(has_pads)
    def _():
        lax.fori_loop(0, n_pad, issue_pad, 0)

    def drain(t, carry):
        for k in range(TOP_K):
            row_copy(t, pos_ref[t * TOP_K + k]).wait()
        return carry

    def drain_pad(q, carry):
        pad_copy(q).wait()
        return carry

    lax.fori_loop(0, TD, drain, 0)

    @pl.when(has_pads)
    def _():
        lax.fori_loop(0, n_pad, drain_pad, 0)


def _dispatch_call(pos_flat, pad_slots, v2, n_slots):
    n, d = v2.shape
    steps = n // TD
    n_pad_blocks = pad_slots.shape[0] // PAD_BLOCK
    assert n_pad_blocks <= steps
    return pl.pallas_call(
        functools.partial(_dispatch_body, n_pad_blocks),
        out_shape=jax.ShapeDtypeStruct((n_slots, d), F32),
        grid=(steps,),
        in_specs=[pl.BlockSpec((TD * TOP_K,), lambda i: (i,), memory_space=pltpu.SMEM),
                  pl.BlockSpec((PAD_BLOCK,), lambda i: (jnp.minimum(i, n_pad_blocks - 1),),
                               memory_space=pltpu.SMEM),
                  pl.BlockSpec((SUBLANES, d), lambda i: (0, 0)),
                  pl.BlockSpec((TD, d), lambda i: (i, 0))],
        out_specs=pl.BlockSpec(memory_space=pl.ANY),
        scratch_shapes=[pltpu.SemaphoreType.DMA],
        compiler_params=pltpu.CompilerParams(dimension_semantics=("arbitrary",)),
        name="dispatch",
    )(pos_flat, pad_slots, jnp.zeros((SUBLANES, d), F32), v2)


def _expert_body(te_ref, blk_ref, nvalid_ref, xs_ref, w1_ref, b1_ref, w2_ref, b2_ref, y_ref, w1b_s, w2b_s):
    j = pl.program_id(0)
    ff = w2_ref.shape[1]
    prev = te_ref[jnp.maximum(j - 1, 0)]

    @pl.when((j == 0) | (te_ref[j] != prev))
    def _():
        w1b_s[...] = w1_ref[0].astype(BF16)
        w2b_s[...] = w2_ref[0].astype(BF16)

    @pl.when(j < nvalid_ref[0])
    def _():
        z = _dot(xs_ref[...].astype(BF16), w1b_s[...]) + b1_ref[0]
        glu = jnp.minimum(z[:, :ff], SWIGLU_LIMIT)
        lin = jnp.clip(z[:, ff:], -SWIGLU_LIMIT, SWIGLU_LIMIT)
        act = glu * _sigmoid(SWIGLU_ALPHA * glu) * (lin + 1.0)
        y_ref[...] = _dot(act.astype(BF16), w2b_s[...]) + b2_ref[0]

    @pl.when(j >= nvalid_ref[0])
    def _():
        y_ref[...] = jnp.zeros_like(y_ref)


def _expert_call(te, blk, nvalid, xs, w1, b1, w2, b2):
    n_slots, d = xs.shape
    n_exp, _, ff2 = w1.shape
    ff = w2.shape[1]
    grid_spec = pltpu.PrefetchScalarGridSpec(
        num_scalar_prefetch=3,
        grid=(n_slots // TM,),
        in_specs=[pl.BlockSpec((TM, d), lambda j, te, blk, nv: (blk[j], 0)),
                  pl.BlockSpec((1, d, ff2), lambda j, te, blk, nv: (te[j], 0, 0)),
                  pl.BlockSpec((1, 1, ff2), lambda j, te, blk, nv: (te[j], 0, 0)),
                  pl.BlockSpec((1, ff, d), lambda j, te, blk, nv: (te[j], 0, 0)),
                  pl.BlockSpec((1, 1, d), lambda j, te, blk, nv: (te[j], 0, 0))],
        out_specs=pl.BlockSpec((TM, d), lambda j, te, blk, nv: (j, 0)),
        scratch_shapes=[pltpu.VMEM((d, ff2), BF16), pltpu.VMEM((ff, d), BF16)],
    )
    return pl.pallas_call(
        _expert_body,
        out_shape=jax.ShapeDtypeStruct((n_slots, d), F32),
        grid_spec=grid_spec,
        compiler_params=pltpu.CompilerParams(
            dimension_semantics=("arbitrary",), vmem_limit_bytes=VMEM_LIMIT),
        name="experts",
    )(te, blk, nvalid, xs, w1, b1, w2, b2)


def _combine_body(pos_ref, wts_ref, xmid_ref, g2_ref, npost_ref, y_hbm, o_ref, buf, sem):
    def row_copy(t, k):
        return pltpu.make_async_copy(y_hbm.at[pl.ds(pos_ref[t * TOP_K + k], 1)],
                                     buf.at[k, pl.ds(t, 1)], sem)

    def issue(t, carry):
        for k in range(TOP_K):
            row_copy(t, k).start()
        return carry

    lax.fori_loop(0, TD, issue, 0)

    def drain(t, carry):
        for k in range(TOP_K):
            row_copy(t, k).wait()
        return carry

    lax.fori_loop(0, TD, drain, 0)

    w = wts_ref[...]
    f = w[:, 0:1] * buf[0]
    for k in range(1, TOP_K):
        f = f + w[:, k:k + 1] * buf[k]
    o_ref[...] = xmid_ref[...] + g2_ref[0] * _rmsnorm(f, npost_ref[...])


def _combine_call(pos_flat, wts, xmid2, g2, npost, ys, t_per_batch):
    n, d = xmid2.shape
    return pl.pallas_call(
        _combine_body,
        out_shape=jax.ShapeDtypeStruct((n, d), F32),
        grid=(n // TD,),
        in_specs=[pl.BlockSpec((TD * TOP_K,), lambda i: (i,), memory_space=pltpu.SMEM),
                  pl.BlockSpec((TD, LANES), lambda i: (i, 0)),
                  pl.BlockSpec((TD, d), lambda i: (i, 0)),
                  pl.BlockSpec((1, 1, d), lambda i: ((i * TD) // t_per_batch, 0, 0)),
                  pl.BlockSpec((1, d), lambda i: (0, 0)),
                  pl.BlockSpec(memory_space=pl.ANY)],
        out_specs=pl.BlockSpec((TD, d), lambda i: (i, 0)),
        scratch_shapes=[pltpu.VMEM((TOP_K, TD, d), F32), pltpu.SemaphoreType.DMA],
        compiler_params=pltpu.CompilerParams(dimension_semantics=("arbitrary",)),
        name="combine",
    )(pos_flat, wts, xmid2, g2, npost, ys)


def _count_le(ends, q):
    return jnp.sum((ends[None, :] <= q[:, None]).astype(jnp.int32), axis=1)


def _gate_weights(wa, wx):
    return jnp.concatenate([wa, wx], axis=-1).astype(BF16)


def kernel(x, c, ctx, c_ctx, w_mod, b_mod, norm_pre_mix, norm_post_mix, norm_pre_ffn, norm_post_ffn, w_in, conv_w, conv_b, lru_wa, lru_ba, lru_wx, lru_bx, lru_lambda, pool_w, pool_scale, w_rnn_proj, w_pool_proj, w_out, router_w, router_b, exp_w1, exp_b1, exp_w2, exp_b2):
    bsz, t, d = x.shape
    assert w_mod.shape[0] == 1, "single-layer block"
    assert t % TT == 0 and TT % POOL_CHUNK == 0 and POOL_CHUNK % GRID_W == 0 and t % TD == 0
    n = bsz * t
    n_exp = router_w.shape[-1]
    assert (n * TOP_K) % TM == 0 and (n_exp * TM) % PAD_BLOCK == 0
    row = lambda a: a.reshape(1, -1)

    pad = jnp.zeros((SUBLANES - (bsz + 1) % SUBLANES, d), F32)
    cc = jnp.concatenate([c, c_ctx[None, :], pad], axis=0)
    mod = _mod_call(cc, w_mod[0], row(b_mod[0]))
    lat = lambda j: mod[:bsz, j * d:(j + 1) * d].reshape(bsz, 1, d)
    sh1, sc1, g1, sh2, sc2, g2 = (lat(j) for j in range(6))
    csh1, csc1 = mod[bsz:bsz + 1, 0:d], mod[bsz:bsz + 1, d:2 * d]

    w_in_b = w_in[0].astype(BF16)
    wgf = _gate_weights(lru_wa[0, 0], lru_wx[0, 0])
    wgb = _gate_weights(lru_wa[0, 1], lru_wx[0, 1])
    ba, bx, lam = lru_ba[0], lru_bx[0], lru_lambda[0]
    cw, cb = conv_w[0], row(conv_b[0])
    npre_mix = row(norm_pre_mix[0])

    h0f, h0b = _ctx_call(ctx, npre_mix, csh1, csc1, w_in_b, cw, cb, wgf, wgb, ba, bx, lam)

    hf, xc, gg, xp, mr, mp = _mix_fwd_call(x, npre_mix, sh1, sc1, w_in_b, cw, cb, wgf, ba, bx, lam, h0f)

    pm, pc = _pool_consts()
    xmid, v, idx_o, wts_o, rank_o, cnt = _mix_bwd_call(
        hf, xc, gg, xp, mr, mp, x, wgb, ba, bx, lam, h0b,
        pm, pc, pool_w[0].astype(BF16), row(pool_scale[0]),
        w_rnn_proj[0].astype(BF16), w_pool_proj[0].astype(BF16), w_out[0].astype(BF16),
        row(norm_post_mix[0]), row(norm_pre_ffn[0]), g1, sh2, sc2, router_w[0], row(router_b[0]))

    idx = idx_o.reshape(n, LANES)[:, :TOP_K]
    rank = rank_o.reshape(n, LANES)[:, :TOP_K]
    counts = cnt[0].astype(jnp.int32)
    tiles_e = (counts + TM - 1) // TM
    tile_end = jnp.cumsum(tiles_e)
    offs = (tile_end - tiles_e) * TM
    pos_flat = (offs[idx] + rank).reshape(n * TOP_K)
    n_tiles = (n * TOP_K) // TM + n_exp
    n_valid = tile_end[-1]
    jj = jnp.arange(n_tiles, dtype=jnp.int32)
    blk = jnp.minimum(jj, n_valid - 1)
    te = jnp.minimum(_count_le(tile_end, blk), n_exp - 1)

    pad_e = tiles_e * TM - counts
    pad_end = jnp.cumsum(pad_e)
    qq = jnp.arange(n_exp * TM, dtype=jnp.int32)
    e_q = jnp.minimum(_count_le(pad_end, qq), n_exp - 1)
    slot_in = (offs + counts)[e_q] + qq - (pad_end - pad_e)[e_q]
    slot_tail = n_valid * TM + qq - pad_end[-1]
    pad_slots = jnp.where(qq < pad_end[-1], slot_in, slot_tail).astype(jnp.int32)

    xs = _dispatch_call(pos_flat, pad_slots, v.reshape(n, d), n_tiles * TM)
    ys = _expert_call(te, blk, n_valid.reshape(1), xs, exp_w1[0], exp_b1[0].reshape(n_exp, 1, -1),
                      exp_w2[0], exp_b2[0].reshape(n_exp, 1, -1))
    out = _combine_call(pos_flat, wts_o.reshape(n, LANES), xmid.reshape(n, d), g2, row(norm_post_ffn[0]), ys, t)
    return out.reshape(bsz, t, d)
```

```python
import functools

import numpy as np
import jax
import jax.numpy as jnp
from jax import lax
from jax.experimental import pallas as pl
from jax.experimental.pallas import tpu as pltpu

F32 = jnp.float32
BF16 = jnp.bfloat16

RNN_HEADS = 4
CONV_W = 4
LRU_C = 8.0
POOL_WINDOWS = (2, 4, 8, 16)
GRID_W = 64
TOP_K = 4
SWIGLU_LIMIT = 7.0
SWIGLU_ALPHA = 1.702
EPS = 1e-6

SUBLANES = 8
LANES = 128
VMEM_LIMIT = 56 * 1024 * 1024

TT = 256
POOL_CHUNK = 256
TM = 256
TD = 128
PAD_BLOCK = 128
ISSUE_UNROLL = 8


def _sigmoid(x):
    return 1.0 / (1.0 + jnp.exp(-x))


def _softplus(z):
    return jnp.maximum(z, 0.0) + jnp.log1p(jnp.exp(-jnp.abs(z)))


def _rmsnorm(x, g):
    ms = jnp.mean(x * x, axis=-1, keepdims=True)
    return (x * lax.rsqrt(ms + EPS)) * g


def _modulate(u, shift, scale):
    return u * (1.0 + scale) + shift


def _dot(a, b):
    return jnp.dot(a, b, preferred_element_type=F32)


def _split(x):
    hi = x.astype(BF16)
    return hi, (x - hi.astype(F32)).astype(BF16)


def _dot3(a, b):
    ah, al = _split(a)
    bh, bl = _split(b)
    return _dot(ah, bh) + (_dot(al, bh) + _dot(ah, bl))


def _conv(prev8, xr, next8, cw, cb):
    t = xr.shape[0]
    ext = jnp.concatenate([prev8, xr, next8], axis=0)
    acc = cb + cw[0:1, :] * ext[SUBLANES - 2:SUBLANES - 2 + t, :]
    for k in range(1, CONV_W):
        off = SUBLANES - 2 + k
        acc = acc + cw[k:k + 1, :] * ext[off:off + t, :]
    return acc


def _lru_coeffs(xc, wg_ref, ba, bx, sp):
    hd = xc.shape[1] // RNN_HEADS
    xcb = xc.astype(BF16)
    a_parts, b_parts = [], []
    for h in range(RNN_HEADS):
        cols = slice(h * hd, (h + 1) * hd)
        z = _dot(xcb[:, cols], wg_ref[h])
        r = _sigmoid(z[:, :hd] + ba[:, cols])
        i = _sigmoid(z[:, hd:] + bx[:, cols])
        a = jnp.exp((-LRU_C) * r * sp[:, cols])
        b = jnp.sqrt(1.0 - a * a) * (i * xc[:, cols])
        a_parts.append(a)
        b_parts.append(b)
    return jnp.concatenate(a_parts, axis=1), jnp.concatenate(b_parts, axis=1)


def _scan_tile(a, b, h_in, reverse, store):
    t, c = a.shape
    g = t // SUBLANES
    a3 = a.reshape(g, SUBLANES, c)
    b3 = b.reshape(g, SUBLANES, c)
    row = lax.broadcasted_iota(jnp.int32, (g, SUBLANES, c), 1)
    for s in (1, 2, 4):
        if reverse:
            shift, m = SUBLANES - s, row < SUBLANES - s
        else:
            shift, m = s, row >= s
        ra = pltpu.roll(a3, shift, axis=1)
        rb = pltpu.roll(b3, shift, axis=1)
        b3 = a3 * jnp.where(m, rb, 0.0) + b3
        a3 = a3 * jnp.where(m, ra, 1.0)
    h = h_in
    order = range(g - 1, -1, -1) if reverse else range(g)
    for gi in order:
        hg = a3[gi] * h + b3[gi]
        store(gi, hg)
        h = hg[0:1, :] if reverse else hg[SUBLANES - 1:SUBLANES, :]
    return h


def _mod_body(c_ref, w_ref, b_ref, o_ref):
    c = c_ref[...]
    s = c * _sigmoid(c)
    o_ref[...] = _dot3(s, w_ref[...]) + b_ref[...]


def _mod_call(cc, w_mod, b_mod):
    d = cc.shape[1]
    n = w_mod.shape[1] // d
    return pl.pallas_call(
        _mod_body,
        out_shape=jax.ShapeDtypeStruct((cc.shape[0], n * d), F32),
        grid=(n,),
        in_specs=[pl.BlockSpec(cc.shape, lambda j: (0, 0)),
                  pl.BlockSpec((d, d), lambda j: (0, j)),
                  pl.BlockSpec((1, d), lambda j: (0, j))],
        out_specs=pl.BlockSpec((cc.shape[0], d), lambda j: (0, j)),
        name="mod",
    )(cc, w_mod, b_mod)


def _ctx_body(ctx_ref, g_ref, sh_ref, sc_ref, w_ref, cw_ref, cb_ref, wgf_ref, wgb_ref,
              ba_ref, bx_ref, lam_ref, hf_ref, hb_ref):
    d = ctx_ref.shape[2]
    u = _modulate(_rmsnorm(ctx_ref[0], g_ref[...]), sh_ref[...], sc_ref[...])
    xr = _dot(u.astype(BF16), w_ref[...])
    z8 = jnp.zeros((SUBLANES, d), F32)
    xc = _conv(z8, xr, z8, cw_ref[...], cb_ref[...])
    h0 = jnp.zeros((1, d), F32)
    for di, (wg_ref, out_ref) in enumerate(((wgf_ref, hf_ref), (wgb_ref, hb_ref))):
        sp = _softplus(-lam_ref[di:di + 1, :])
        a, b = _lru_coeffs(xc, wg_ref, ba_ref[di:di + 1, :], bx_ref[di:di + 1, :], sp)
        out_ref[0] = _scan_tile(a, b, h0, di == 1, lambda gi, hg: None)


def _ctx_call(ctx, g, sh, sc, w_in_b, cw, cb, wgf, wgb, ba, bx, lam):
    bsz, tc, d = ctx.shape
    full = lambda shp: pl.BlockSpec(shp, lambda b: (0,) * len(shp))
    return pl.pallas_call(
        _ctx_body,
        out_shape=(jax.ShapeDtypeStruct((bsz, 1, d), F32),) * 2,
        grid=(bsz,),
        in_specs=[pl.BlockSpec((1, tc, d), lambda b: (b, 0, 0)),
                  full((1, d)), full((1, d)), full((1, d)),
                  pl.BlockSpec((d, d), lambda b: (0, 0)),
                  full(cw.shape), full((1, d)), full(wgf.shape), full(wgb.shape),
                  full(ba.shape), full(bx.shape), full(lam.shape)],
        out_specs=(pl.BlockSpec((1, 1, d), lambda b: (b, 0, 0)),) * 2,
        compiler_params=pltpu.CompilerParams(vmem_limit_bytes=VMEM_LIMIT),
        name="ctx",
    )(ctx, g, sh, sc, w_in_b, cw, cb, wgf, wgb, ba, bx, lam)


def _mix_fwd_body(x_ref, xn_ref, g_ref, sh_ref, sc_ref, win_ref, cw_ref, cb_ref, wgf_ref,
                  ba_ref, bx_ref, lam_ref, h0_ref,
                  hf_ref, xc_ref, gg_ref, xp_ref, mr_ref, mp_ref, tail_s, h_s):
    i = pl.program_id(1)
    last = pl.num_programs(1) - 1
    tt, d = x_ref.shape[1], x_ref.shape[2]

    @pl.when(i == 0)
    def _():
        tail_s[...] = jnp.zeros_like(tail_s)
        h_s[...] = h0_ref[0]

    g, sh, sc = g_ref[...], sh_ref[0], sc_ref[0]
    ub = _modulate(_rmsnorm(x_ref[0], g), sh, sc).astype(BF16)
    unb = _modulate(_rmsnorm(xn_ref[0], g), sh, sc).astype(BF16)
    w_rnn = win_ref[:, 0:d]
    xr = _dot(ub, w_rnn)
    xrn = jnp.where(i == last, 0.0, _dot(unb, w_rnn))
    xc = _conv(tail_s[...], xr, xrn, cw_ref[...], cb_ref[...])
    tail_s[...] = xr[tt - SUBLANES:tt, :]
    xc_ref[0] = xc.astype(BF16)
    for j, ref in enumerate((gg_ref, xp_ref, mr_ref, mp_ref)):
        ref[0] = _dot(ub, win_ref[:, (j + 1) * d:(j + 2) * d]).astype(BF16)

    sp = _softplus(-lam_ref[0:1, :])
    a, b = _lru_coeffs(xc, wgf_ref, ba_ref[0:1, :], bx_ref[0:1, :], sp)

    def store(gi, hg):
        hf_ref[0, gi * SUBLANES:(gi + 1) * SUBLANES, :] = hg

    h_s[...] = _scan_tile(a, b, h_s[...], False, store)


def _mix_fwd_call(x, g, sh, sc, w_in_b, cw, cb, wgf, ba, bx, lam, h0f):
    bsz, t, d = x.shape
    nt = t // TT
    nblk8 = t // SUBLANES
    per_b = pl.BlockSpec((1, 1, d), lambda b, i: (b, 0, 0))
    full = lambda shp: pl.BlockSpec(shp, lambda b, i: (0,) * len(shp))
    tile = pl.BlockSpec((1, TT, d), lambda b, i: (b, i, 0))
    return pl.pallas_call(
        _mix_fwd_body,
        out_shape=(jax.ShapeDtypeStruct((bsz, t, d), F32),) + (jax.ShapeDtypeStruct((bsz, t, d), BF16),) * 5,
        grid=(bsz, nt),
        in_specs=[tile,
                  pl.BlockSpec((1, SUBLANES, d),
                               lambda b, i: (b, jnp.minimum((i + 1) * (TT // SUBLANES), nblk8 - 1), 0)),
                  full((1, d)), per_b, per_b,
                  full(w_in_b.shape), full(cw.shape), full((1, d)), full(wgf.shape),
                  full(ba.shape), full(bx.shape), full(lam.shape), per_b],
        out_specs=(tile,) * 6,
        scratch_shapes=[pltpu.VMEM((SUBLANES, d), F32), pltpu.VMEM((1, d), F32)],
        compiler_params=pltpu.CompilerParams(
            dimension_semantics=("arbitrary", "arbitrary"), vmem_limit_bytes=VMEM_LIMIT),
        name="mix_fwd",
    )(x, x, g, sh, sc, w_in_b, cw, cb, wgf, ba, bx, lam, h0f)


def _pool_consts():
    p = np.arange(POOL_CHUNK)
    pos, line = p % GRID_W, p // GRID_W
    mats, cnts = [], []
    for w in POOL_WINDOWS:
        lo = np.clip(pos - w // 2, 0, GRID_W)
        hi = np.clip(pos + w - w // 2, 0, GRID_W)
        m = (line[:, None] == line[None, :]) & (pos[None, :] >= lo[:, None]) & (pos[None, :] < hi[:, None])
        mats.append(m.astype(np.float32))
        cnts.append((hi - lo).astype(np.float32)[:, None])
    return jnp.asarray(np.stack(mats), BF16), jnp.asarray(np.stack(cnts), F32)


def _mix_bwd_body(hf_ref, xc_ref, gg_ref, xp_ref, mr_ref, mp_ref, x_ref,
                  wgb_ref, ba_ref, bx_ref, lam_ref, h0_ref,
                  pm_ref, pc_ref, pw_ref, ps_ref, wr_ref, wp_ref, wo_ref,
                  npost_ref, npre_ref, g1_ref, sh2_ref, sc2_ref, rw_ref, rb_ref,
                  xmid_ref, v_ref, idx_ref, wts_ref, rank_ref, cnt_ref,
                  hb_s, h_s, cnt_s):
    b_id, i = pl.program_id(0), pl.program_id(1)
    tt, d = x_ref.shape[1], x_ref.shape[2]
    n_exp = rw_ref.shape[1]

    @pl.when(i == 0)
    def _():
        h_s[...] = h0_ref[0]

    @pl.when((i == 0) & (b_id == 0))
    def _():
        cnt_s[...] = jnp.zeros_like(cnt_s)

    xc = xc_ref[0].astype(F32)
    sp = _softplus(-lam_ref[1:2, :])
    a, b = _lru_coeffs(xc, wgb_ref, ba_ref[1:2, :], bx_ref[1:2, :], sp)

    def store(gi, hg):
        hb_s[gi * SUBLANES:(gi + 1) * SUBLANES, :] = hg

    h_s[...] = _scan_tile(a, b, h_s[...], True, store)

    gg = gg_ref[0].astype(F32)
    gelu = 0.5 * gg * (1.0 + jnp.tanh(0.7978845608028654 * (gg + 0.044715 * gg * gg * gg)))
    y_rnn = (hf_ref[0] + hb_s[...]) * gelu

    xpb = xp_ref[0]
    grp = d // len(POOL_WINDOWS)
    y_parts = []
    for gi in range(len(POOL_WINDOWS)):
        cols = slice(gi * grp, (gi + 1) * grp)
        rows = []
        for c0 in range(0, tt, POOL_CHUNK):
            xg = xpb[c0:c0 + POOL_CHUNK, cols]
            mean = _dot(pm_ref[gi], xg) / pc_ref[gi]
            rows.append((mean - xg.astype(F32)).astype(BF16))
        dg = rows[0] if len(rows) == 1 else jnp.concatenate(rows, axis=0)
        y_parts.append(_dot(dg, pw_ref[gi]) * ps_ref[:, cols])
    y_pool = jnp.concatenate(y_parts, axis=1)

    merged = (_sigmoid(mr_ref[0].astype(F32)) * _dot(y_rnn.astype(BF16), wr_ref[...])
              + _sigmoid(mp_ref[0].astype(F32)) * _dot(y_pool.astype(BF16), wp_ref[...]))
    m_lat = _dot(merged.astype(BF16), wo_ref[...])
    x_mid = x_ref[0] + g1_ref[0] * _rmsnorm(m_lat, npost_ref[...])
    xmid_ref[0] = x_mid
    v = _modulate(_rmsnorm(x_mid, npre_ref[...]), sh2_ref[0], sc2_ref[0])
    v_ref[0] = v.reshape(tt, SUBLANES, LANES)

    logits = _dot3(v, rw_ref[...]) + rb_ref[...]
    lane = lax.broadcasted_iota(jnp.int32, (tt, n_exp), 1).astype(F32)
    work = logits
    vals, idxs, sels = [], [], []
    for _ in range(TOP_K):
        m = jnp.max(work, axis=-1, keepdims=True)
        idx = jnp.min(jnp.where(work == m, lane, float(n_exp)), axis=-1, keepdims=True)
        sel = lane == idx
        vals.append(m)
        idxs.append(idx)
        sels.append(sel)
        work = jnp.where(sel, -jnp.inf, work)
    exps = [jnp.exp(vk - vals[0]) for vk in vals]
    den = exps[0] + exps[1] + exps[2] + exps[3]
    anyf = jnp.zeros((tt, n_exp), F32)
    for sel in sels:
        anyf = anyf + sel.astype(F32)
    r_i = lax.broadcasted_iota(jnp.int32, (tt, tt), 0)
    c_i = lax.broadcasted_iota(jnp.int32, (tt, tt), 1)
    lower = (c_i < r_i).astype(BF16)
    before = _dot(lower, anyf.astype(BF16)) + cnt_s[...]
    cnt_new = cnt_s[...] + jnp.sum(anyf, axis=0, keepdims=True)
    cnt_s[...] = cnt_new
    cnt_ref[...] = cnt_new

    lane_o = lax.broadcasted_iota(jnp.int32, (tt, LANES), 1)
    idx_o = jnp.zeros((tt, LANES), jnp.int32)
    wts_o = jnp.zeros((tt, LANES), F32)
    rank_o = jnp.zeros((tt, LANES), jnp.int32)
    for k in range(TOP_K):
        rk = jnp.sum(jnp.where(sels[k], before, 0.0), axis=-1, keepdims=True)
        idx_o = jnp.where(lane_o == k, idxs[k].astype(jnp.int32), idx_o)
        wts_o = jnp.where(lane_o == k, exps[k] / den, wts_o)
        rank_o = jnp.where(lane_o == k, rk.astype(jnp.int32), rank_o)
    idx_ref[0] = idx_o
    wts_ref[0] = wts_o
    rank_ref[0] = rank_o


def _mix_bwd_call(hf, xc, gg, xp, mr, mp, x, wgb, ba, bx, lam, h0b, pm, pc, pw, ps, wr, wp, wo,
                  npost, npre, g1, sh2, sc2, rw, rb):
    bsz, t, d = x.shape
    nt = t // TT
    n_exp = rw.shape[1]
    rev = lambda b, i: (b, nt - 1 - i, 0)
    tile = pl.BlockSpec((1, TT, d), rev)
    small = pl.BlockSpec((1, TT, LANES), rev)
    per_b = pl.BlockSpec((1, 1, d), lambda b, i: (b, 0, 0))
    full = lambda shp: pl.BlockSpec(shp, lambda b, i: (0,) * len(shp))
    return pl.pallas_call(
        _mix_bwd_body,
        out_shape=(jax.ShapeDtypeStruct((bsz, t, d), F32),
                   jax.ShapeDtypeStruct((bsz, t, SUBLANES, LANES), F32),
                   jax.ShapeDtypeStruct((bsz, t, LANES), jnp.int32),
                   jax.ShapeDtypeStruct((bsz, t, LANES), F32),
                   jax.ShapeDtypeStruct((bsz, t, LANES), jnp.int32),
                   jax.ShapeDtypeStruct((1, n_exp), F32)),
        grid=(bsz, nt),
        in_specs=[tile] * 7 + [full(wgb.shape), full(ba.shape), full(bx.shape), full(lam.shape), per_b,
                               full(pm.shape), full(pc.shape), full(pw.shape), full((1, d)),
                               full((d, d)), full((d, d)), full((d, d)),
                               full((1, d)), full((1, d)), per_b, per_b, per_b,
                               full(rw.shape), full((1, n_exp))],
        out_specs=(tile, pl.BlockSpec((1, TT, SUBLANES, LANES), lambda b, i: (b, nt - 1 - i, 0, 0)),
                   small, small, small, full((1, n_exp))),
        scratch_shapes=[pltpu.VMEM((TT, d), F32), pltpu.VMEM((1, d), F32), pltpu.VMEM((1, n_exp), F32)],
        compiler_params=pltpu.CompilerParams(
            dimension_semantics=("arbitrary", "arbitrary"), vmem_limit_bytes=VMEM_LIMIT),
        name="mix_bwd",
    )(hf, xc, gg, xp, mr, mp, x, wgb, ba, bx, lam, h0b, pm, pc, pw, ps, wr, wp, wo,
      npost, npre, g1, sh2, sc2, rw, rb)


def _dispatch_body(n_pad_blocks, pos_ref, pad_ref, zero_ref, v_ref, xs_hbm, sem):
    n_pad = pad_ref.shape[0]

    def row_copy(src_row, dst_row):
        return pltpu.make_async_copy(v_ref.at[src_row], xs_hbm.at[dst_row], sem)

    def pad_copy(q):
        return pltpu.make_async_copy(zero_ref.at[0], xs_hbm.at[pad_ref[q]], sem)

    def issue(t, carry):
        for k in range(TOP_K):
            row_copy(t, pos_ref[t * TOP_K + k]).start()
        return carry

    def issue_pad(q, carry):
        pad_copy(q).start()
        return carry

    has_pads = pl.program_id(0) < n_pad_blocks
    lax.fori_loop(0, TD, issue, 0, unroll=ISSUE_UNROLL)

    @pl.when(has_pads)
    def _():
        lax.fori_loop(0, n_pad, issue_pad, 0, unroll=ISSUE_UNROLL)

    def drain(t, carry):
        for k in range(TOP_K):
            row_copy(t, pos_ref[t * TOP_K + k]).wait()
        return carry

    def drain_pad(q, carry):
        pad_copy(q).wait()
        return carry

    lax.fori_loop(0, TD, drain, 0, unroll=ISSUE_UNROLL)

    @pl.when(has_pads)
    def _():
        lax.fori_loop(0, n_pad, drain_pad, 0, unroll=ISSUE_UNROLL)


def _dispatch_call(pos_flat, pad_slots, v3, n_slots):
    n = v3.shape[0]
    slab = v3.shape[1:]
    steps = n // TD
    n_pad_blocks = pad_slots.shape[0] // PAD_BLOCK
    assert n_pad_blocks <= steps
    return pl.pallas_call(
        functools.partial(_dispatch_body, n_pad_blocks),
        out_shape=jax.ShapeDtypeStruct((n_slots,) + slab, F32),
        grid=(steps,),
        in_specs=[pl.BlockSpec((TD * TOP_K,), lambda i: (i,), memory_space=pltpu.SMEM),
                  pl.BlockSpec((PAD_BLOCK,), lambda i: (jnp.minimum(i, n_pad_blocks - 1),),
                               memory_space=pltpu.SMEM),
                  pl.BlockSpec((1,) + slab, lambda i: (0, 0, 0)),
                  pl.BlockSpec((TD,) + slab, lambda i: (i, 0, 0))],
        out_specs=pl.BlockSpec(memory_space=pl.ANY),
        scratch_shapes=[pltpu.SemaphoreType.DMA],
        compiler_params=pltpu.CompilerParams(dimension_semantics=("arbitrary",)),
        name="dispatch",
    )(pos_flat, pad_slots, jnp.zeros((1,) + slab, F32), v3)


def _expert_body(te_ref, blk_ref, nvalid_ref, xs_ref, w1_ref, b1_ref, w2_ref, b2_ref, y_ref, w1b_s, w2b_s):
    j = pl.program_id(0)
    ff = w2_ref.shape[1]
    prev = te_ref[jnp.maximum(j - 1, 0)]

    @pl.when((j == 0) | (te_ref[j] != prev))
    def _():
        w1b_s[...] = w1_ref[0].astype(BF16)
        w2b_s[...] = w2_ref[0].astype(BF16)

    @pl.when(j < nvalid_ref[0])
    def _():
        tm = xs_ref.shape[0]
        x = xs_ref[...].reshape(tm, w1_ref.shape[1])
        z = _dot(x.astype(BF16), w1b_s[...]) + b1_ref[0]
        glu = jnp.minimum(z[:, :ff], SWIGLU_LIMIT)
        lin = jnp.clip(z[:, ff:], -SWIGLU_LIMIT, SWIGLU_LIMIT)
        act = glu * _sigmoid(SWIGLU_ALPHA * glu) * (lin + 1.0)
        y = _dot(act.astype(BF16), w2b_s[...]) + b2_ref[0]
        y_ref[...] = y.reshape(y_ref.shape)

    @pl.when(j >= nvalid_ref[0])
    def _():
        y_ref[...] = jnp.zeros_like(y_ref)


def _expert_call(te, blk, nvalid, xs, w1, b1, w2, b2):
    n_slots = xs.shape[0]
    slab = xs.shape[1:]
    n_exp, d, ff2 = w1.shape
    ff = w2.shape[1]
    grid_spec = pltpu.PrefetchScalarGridSpec(
        num_scalar_prefetch=3,
        grid=(n_slots // TM,),
        in_specs=[pl.BlockSpec((TM,) + slab, lambda j, te, blk, nv: (blk[j], 0, 0)),
                  pl.BlockSpec((1, d, ff2), lambda j, te, blk, nv: (te[j], 0, 0)),
                  pl.BlockSpec((1, 1, ff2), lambda j, te, blk, nv: (te[j], 0, 0)),
                  pl.BlockSpec((1, ff, d), lambda j, te, blk, nv: (te[j], 0, 0)),
                  pl.BlockSpec((1, 1, d), lambda j, te, blk, nv: (te[j], 0, 0))],
        out_specs=pl.BlockSpec((TM,) + slab, lambda j, te, blk, nv: (j, 0, 0)),
        scratch_shapes=[pltpu.VMEM((d, ff2), BF16), pltpu.VMEM((ff, d), BF16)],
    )
    return pl.pallas_call(
        _expert_body,
        out_shape=jax.ShapeDtypeStruct((n_slots,) + slab, F32),
        grid_spec=grid_spec,
        compiler_params=pltpu.CompilerParams(
            dimension_semantics=("arbitrary",), vmem_limit_bytes=VMEM_LIMIT),
        name="experts",
    )(te, blk, nvalid, xs, w1, b1, w2, b2)


def _combine_body(pos_ref, wts_ref, xmid_ref, g2_ref, npost_ref, y_hbm, o_ref, buf, sem):
    def row_copy(t, k):
        return pltpu.make_async_copy(y_hbm.at[pos_ref[t * TOP_K + k]], buf.at[k, t], sem)

    def issue(t, carry):
        for k in range(TOP_K):
            row_copy(t, k).start()
        return carry

    lax.fori_loop(0, TD, issue, 0, unroll=ISSUE_UNROLL)

    def drain(t, carry):
        for k in range(TOP_K):
            row_copy(t, k).wait()
        return carry

    lax.fori_loop(0, TD, drain, 0, unroll=ISSUE_UNROLL)

    w = wts_ref[...]
    f = w[:, 0:1] * buf[0].reshape(xmid_ref.shape)
    for k in range(1, TOP_K):
        f = f + w[:, k:k + 1] * buf[k].reshape(xmid_ref.shape)
    o_ref[...] = xmid_ref[...] + g2_ref[0] * _rmsnorm(f, npost_ref[...])


def _combine_call(pos_flat, wts, xmid2, g2, npost, ys, t_per_batch):
    n, d = xmid2.shape
    return pl.pallas_call(
        _combine_body,
        out_shape=jax.ShapeDtypeStruct((n, d), F32),
        grid=(n // TD,),
        in_specs=[pl.BlockSpec((TD * TOP_K,), lambda i: (i,), memory_space=pltpu.SMEM),
                  pl.BlockSpec((TD, LANES), lambda i: (i, 0)),
                  pl.BlockSpec((TD, d), lambda i: (i, 0)),
                  pl.BlockSpec((1, 1, d), lambda i: ((i * TD) // t_per_batch, 0, 0)),
                  pl.BlockSpec((1, d), lambda i: (0, 0)),
                  pl.BlockSpec(memory_space=pl.ANY)],
        out_specs=pl.BlockSpec((TD, d), lambda i: (i, 0)),
        scratch_shapes=[pltpu.VMEM((TOP_K, TD) + ys.shape[1:], F32), pltpu.SemaphoreType.DMA],
        compiler_params=pltpu.CompilerParams(dimension_semantics=("arbitrary",)),
        name="combine",
    )(pos_flat, wts, xmid2, g2, npost, ys)


def _count_le(ends, q):
    return jnp.sum((ends[None, :] <= q[:, None]).astype(jnp.int32), axis=1)


def _gate_weights(wa, wx):
    return jnp.concatenate([wa, wx], axis=-1).astype(BF16)


def kernel(x, c, ctx, c_ctx, w_mod, b_mod, norm_pre_mix, norm_post_mix, norm_pre_ffn, norm_post_ffn, w_in, conv_w, conv_b, lru_wa, lru_ba, lru_wx, lru_bx, lru_lambda, pool_w, pool_scale, w_rnn_proj, w_pool_proj, w_out, router_w, router_b, exp_w1, exp_b1, exp_w2, exp_b2):
    bsz, t, d = x.shape
    assert w_mod.shape[0] == 1, "single-layer block"
    assert t % TT == 0 and TT % POOL_CHUNK == 0 and POOL_CHUNK % GRID_W == 0 and t % TD == 0
    assert d == SUBLANES * LANES, "a token row must be exactly one (8, 128) slab"
    n = bsz * t
    n_exp = router_w.shape[-1]
    assert (n * TOP_K) % TM == 0 and (n_exp * TM) % PAD_BLOCK == 0
    row = lambda a: a.reshape(1, -1)

    pad = jnp.zeros((SUBLANES - (bsz + 1) % SUBLANES, d), F32)
    cc = jnp.concatenate([c, c_ctx[None, :], pad], axis=0)
    mod = _mod_call(cc, w_mod[0], row(b_mod[0]))
    lat = lambda j: mod[:bsz, j * d:(j + 1) * d].reshape(bsz, 1, d)
    sh1, sc1, g1, sh2, sc2, g2 = (lat(j) for j in range(6))
    csh1, csc1 = mod[bsz:bsz + 1, 0:d], mod[bsz:bsz + 1, d:2 * d]

    w_in_b = w_in[0].astype(BF16)
    wgf = _gate_weights(lru_wa[0, 0], lru_wx[0, 0])
    wgb = _gate_weights(lru_wa[0, 1], lru_wx[0, 1])
    ba, bx, lam = lru_ba[0], lru_bx[0], lru_lambda[0]
    cw, cb = conv_w[0], row(conv_b[0])
    npre_mix = row(norm_pre_mix[0])

    h0f, h0b = _ctx_call(ctx, npre_mix, csh1, csc1, w_in_b, cw, cb, wgf, wgb, ba, bx, lam)

    hf, xc, gg, xp, mr, mp = _mix_fwd_call(x, npre_mix, sh1, sc1, w_in_b, cw, cb, wgf, ba, bx, lam, h0f)

    pm, pc = _pool_consts()
    xmid, v, idx_o, wts_o, rank_o, cnt = _mix_bwd_call(
        hf, xc, gg, xp, mr, mp, x, wgb, ba, bx, lam, h0b,
        pm, pc, pool_w[0].astype(BF16), row(pool_scale[0]),
        w_rnn_proj[0].astype(BF16), w_pool_proj[0].astype(BF16), w_out[0].astype(BF16),
        row(norm_post_mix[0]), row(norm_pre_ffn[0]), g1, sh2, sc2, router_w[0], row(router_b[0]))

    idx = idx_o.reshape(n, LANES)[:, :TOP_K]
    rank = rank_o.reshape(n, LANES)[:, :TOP_K]
    counts = cnt[0].astype(jnp.int32)
    tiles_e = (counts + TM - 1) // TM
    tile_end = jnp.cumsum(tiles_e)
    offs = (tile_end - tiles_e) * TM
    pos_flat = (offs[idx] + rank).reshape(n * TOP_K)
    n_tiles = (n * TOP_K) // TM + n_exp
    n_valid = tile_end[-1]
    jj = jnp.arange(n_tiles, dtype=jnp.int32)
    blk = jnp.minimum(jj, n_valid - 1)
    te = jnp.minimum(_count_le(tile_end, blk), n_exp - 1)

    pad_e = tiles_e * TM - counts
    pad_end = jnp.cumsum(pad_e)
    qq = jnp.arange(n_exp * TM, dtype=jnp.int32)
    e_q = jnp.minimum(_count_le(pad_end, qq), n_exp - 1)
    slot_in = (offs + counts)[e_q] + qq - (pad_end - pad_e)[e_q]
    slot_tail = n_valid * TM + qq - pad_end[-1]
    pad_slots = jnp.where(qq < pad_end[-1], slot_in, slot_tail).astype(jnp.int32)

    xs = _dispatch_call(pos_flat, pad_slots, v.reshape((n,) + v.shape[2:]), n_tiles * TM)
    ys = _expert_call(te, blk, n_valid.reshape(1), xs, exp_w1[0], exp_b1[0].reshape(n_exp, 1, -1),
                      exp_w2[0], exp_b2[0].reshape(n_exp, 1, -1))
    out = _combine_call(pos_flat, wts_o.reshape(n, LANES), xmid.reshape(n, d), g2, row(norm_post_ffn[0]), ys, t)
    return out.reshape(bsz, t, d)
```

```python
import functools

import numpy as np
import jax
import jax.numpy as jnp
from jax import lax
from jax.experimental import pallas as pl
from jax.experimental.pallas import tpu as pltpu

F32 = jnp.float32
BF16 = jnp.bfloat16

RNN_HEADS = 4
CONV_W = 4
LRU_C = 8.0
POOL_WINDOWS = (2, 4, 8, 16)
GRID_W = 64
TOP_K = 4
SWIGLU_LIMIT = 7.0
SWIGLU_ALPHA = 1.702
EPS = 1e-6

SUBLANES = 8
LANES = 128
VMEM_LIMIT = 56 * 1024 * 1024

TT = 256
POOL_CHUNK = 256
TM = 256
TD = 128
PAD_BLOCK = 128
ISSUE_UNROLL = 8
DMA_PRIORITIES = 2


def _sigmoid(x):
    return 1.0 / (1.0 + jnp.exp(-x))


def _softplus(z):
    return jnp.maximum(z, 0.0) + jnp.log1p(jnp.exp(-jnp.abs(z)))


def _rmsnorm(x, g):
    ms = jnp.mean(x * x, axis=-1, keepdims=True)
    return (x * lax.rsqrt(ms + EPS)) * g


def _modulate(u, shift, scale):
    return u * (1.0 + scale) + shift


def _dot(a, b):
    return jnp.dot(a, b, preferred_element_type=F32)


def _split(x):
    hi = x.astype(BF16)
    return hi, (x - hi.astype(F32)).astype(BF16)


def _dot3(a, b):
    ah, al = _split(a)
    bh, bl = _split(b)
    return _dot(ah, bh) + (_dot(al, bh) + _dot(ah, bl))


def _conv(prev8, xr, next8, cw, cb):
    t = xr.shape[0]
    ext = jnp.concatenate([prev8, xr, next8], axis=0)
    acc = cb + cw[0:1, :] * ext[SUBLANES - 2:SUBLANES - 2 + t, :]
    for k in range(1, CONV_W):
        off = SUBLANES - 2 + k
        acc = acc + cw[k:k + 1, :] * ext[off:off + t, :]
    return acc


def _lru_coeffs(xc, wg_ref, ba, bx, sp):
    hd = xc.shape[1] // RNN_HEADS
    xcb = xc.astype(BF16)
    a_parts, b_parts = [], []
    for h in range(RNN_HEADS):
        cols = slice(h * hd, (h + 1) * hd)
        z = _dot(xcb[:, cols], wg_ref[h])
        r = _sigmoid(z[:, :hd] + ba[:, cols])
        i = _sigmoid(z[:, hd:] + bx[:, cols])
        a = jnp.exp((-LRU_C) * r * sp[:, cols])
        b = jnp.sqrt(1.0 - a * a) * (i * xc[:, cols])
        a_parts.append(a)
        b_parts.append(b)
    return jnp.concatenate(a_parts, axis=1), jnp.concatenate(b_parts, axis=1)


def _scan_tile(a, b, h_in, reverse, store):
    t, c = a.shape
    g = t // SUBLANES
    a3 = a.reshape(g, SUBLANES, c)
    b3 = b.reshape(g, SUBLANES, c)
    row = lax.broadcasted_iota(jnp.int32, (g, SUBLANES, c), 1)
    for s in (1, 2, 4):
        if reverse:
            shift, m = SUBLANES - s, row < SUBLANES - s
        else:
            shift, m = s, row >= s
        ra = pltpu.roll(a3, shift, axis=1)
        rb = pltpu.roll(b3, shift, axis=1)
        b3 = a3 * jnp.where(m, rb, 0.0) + b3
        a3 = a3 * jnp.where(m, ra, 1.0)
    h = h_in
    order = range(g - 1, -1, -1) if reverse else range(g)
    for gi in order:
        hg = a3[gi] * h + b3[gi]
        store(gi, hg)
        h = hg[0:1, :] if reverse else hg[SUBLANES - 1:SUBLANES, :]
    return h


def _mod_body(c_ref, w_ref, b_ref, o_ref):
    c = c_ref[...]
    s = c * _sigmoid(c)
    o_ref[...] = _dot3(s, w_ref[...]) + b_ref[...]


def _mod_call(cc, w_mod, b_mod):
    d = cc.shape[1]
    n = w_mod.shape[1] // d
    return pl.pallas_call(
        _mod_body,
        out_shape=jax.ShapeDtypeStruct((cc.shape[0], n * d), F32),
        grid=(n,),
        in_specs=[pl.BlockSpec(cc.shape, lambda j: (0, 0)),
                  pl.BlockSpec((d, d), lambda j: (0, j)),
                  pl.BlockSpec((1, d), lambda j: (0, j))],
        out_specs=pl.BlockSpec((cc.shape[0], d), lambda j: (0, j)),
        name="mod",
    )(cc, w_mod, b_mod)


def _ctx_body(ctx_ref, g_ref, sh_ref, sc_ref, w_ref, cw_ref, cb_ref, wgf_ref, wgb_ref,
              ba_ref, bx_ref, lam_ref, hf_ref, hb_ref):
    d = ctx_ref.shape[2]
    u = _modulate(_rmsnorm(ctx_ref[0], g_ref[...]), sh_ref[...], sc_ref[...])
    xr = _dot(u.astype(BF16), w_ref[...])
    z8 = jnp.zeros((SUBLANES, d), F32)
    xc = _conv(z8, xr, z8, cw_ref[...], cb_ref[...])
    h0 = jnp.zeros((1, d), F32)
    for di, (wg_ref, out_ref) in enumerate(((wgf_ref, hf_ref), (wgb_ref, hb_ref))):
        sp = _softplus(-lam_ref[di:di + 1, :])
        a, b = _lru_coeffs(xc, wg_ref, ba_ref[di:di + 1, :], bx_ref[di:di + 1, :], sp)
        out_ref[0] = _scan_tile(a, b, h0, di == 1, lambda gi, hg: None)


def _ctx_call(ctx, g, sh, sc, w_in_b, cw, cb, wgf, wgb, ba, bx, lam):
    bsz, tc, d = ctx.shape
    full = lambda shp: pl.BlockSpec(shp, lambda b: (0,) * len(shp))
    return pl.pallas_call(
        _ctx_body,
        out_shape=(jax.ShapeDtypeStruct((bsz, 1, d), F32),) * 2,
        grid=(bsz,),
        in_specs=[pl.BlockSpec((1, tc, d), lambda b: (b, 0, 0)),
                  full((1, d)), full((1, d)), full((1, d)),
                  pl.BlockSpec((d, d), lambda b: (0, 0)),
                  full(cw.shape), full((1, d)), full(wgf.shape), full(wgb.shape),
                  full(ba.shape), full(bx.shape), full(lam.shape)],
        out_specs=(pl.BlockSpec((1, 1, d), lambda b: (b, 0, 0)),) * 2,
        compiler_params=pltpu.CompilerParams(vmem_limit_bytes=VMEM_LIMIT),
        name="ctx",
    )(ctx, g, sh, sc, w_in_b, cw, cb, wgf, wgb, ba, bx, lam)


def _mix_fwd_body(x_ref, xn_ref, g_ref, sh_ref, sc_ref, win_ref, cw_ref, cb_ref, wgf_ref,
                  ba_ref, bx_ref, lam_ref, h0_ref,
                  hf_ref, xc_ref, gg_ref, xp_ref, mr_ref, mp_ref, tail_s, h_s):
    i = pl.program_id(1)
    last = pl.num_programs(1) - 1
    tt, d = x_ref.shape[1], x_ref.shape[2]

    @pl.when(i == 0)
    def _():
        tail_s[...] = jnp.zeros_like(tail_s)
        h_s[...] = h0_ref[0]

    g, sh, sc = g_ref[...], sh_ref[0], sc_ref[0]
    ub = _modulate(_rmsnorm(x_ref[0], g), sh, sc).astype(BF16)
    unb = _modulate(_rmsnorm(xn_ref[0], g), sh, sc).astype(BF16)
    w_rnn = win_ref[:, 0:d]
    xr = _dot(ub, w_rnn)
    xrn = jnp.where(i == last, 0.0, _dot(unb, w_rnn))
    xc = _conv(tail_s[...], xr, xrn, cw_ref[...], cb_ref[...])
    tail_s[...] = xr[tt - SUBLANES:tt, :]
    xc_ref[0] = xc.astype(BF16)
    for j, ref in enumerate((gg_ref, xp_ref, mr_ref, mp_ref)):
        ref[0] = _dot(ub, win_ref[:, (j + 1) * d:(j + 2) * d]).astype(BF16)

    sp = _softplus(-lam_ref[0:1, :])
    a, b = _lru_coeffs(xc, wgf_ref, ba_ref[0:1, :], bx_ref[0:1, :], sp)

    def store(gi, hg):
        hf_ref[0, gi * SUBLANES:(gi + 1) * SUBLANES, :] = hg

    h_s[...] = _scan_tile(a, b, h_s[...], False, store)


def _mix_fwd_call(x, g, sh, sc, w_in_b, cw, cb, wgf, ba, bx, lam, h0f):
    bsz, t, d = x.shape
    nt = t // TT
    nblk8 = t // SUBLANES
    per_b = pl.BlockSpec((1, 1, d), lambda b, i: (b, 0, 0))
    full = lambda shp: pl.BlockSpec(shp, lambda b, i: (0,) * len(shp))
    tile = pl.BlockSpec((1, TT, d), lambda b, i: (b, i, 0))
    return pl.pallas_call(
        _mix_fwd_body,
        out_shape=(jax.ShapeDtypeStruct((bsz, t, d), F32),) + (jax.ShapeDtypeStruct((bsz, t, d), BF16),) * 5,
        grid=(bsz, nt),
        in_specs=[tile,
                  pl.BlockSpec((1, SUBLANES, d),
                               lambda b, i: (b, jnp.minimum((i + 1) * (TT // SUBLANES), nblk8 - 1), 0)),
                  full((1, d)), per_b, per_b,
                  full(w_in_b.shape), full(cw.shape), full((1, d)), full(wgf.shape),
                  full(ba.shape), full(bx.shape), full(lam.shape), per_b],
        out_specs=(tile,) * 6,
        scratch_shapes=[pltpu.VMEM((SUBLANES, d), F32), pltpu.VMEM((1, d), F32)],
        compiler_params=pltpu.CompilerParams(
            dimension_semantics=("arbitrary", "arbitrary"), vmem_limit_bytes=VMEM_LIMIT),
        name="mix_fwd",
    )(x, x, g, sh, sc, w_in_b, cw, cb, wgf, ba, bx, lam, h0f)


def _pool_consts():
    p = np.arange(POOL_CHUNK)
    pos, line = p % GRID_W, p // GRID_W
    mats, cnts = [], []
    for w in POOL_WINDOWS:
        lo = np.clip(pos - w // 2, 0, GRID_W)
        hi = np.clip(pos + w - w // 2, 0, GRID_W)
        m = (line[:, None] == line[None, :]) & (pos[None, :] >= lo[:, None]) & (pos[None, :] < hi[:, None])
        mats.append(m.astype(np.float32))
        cnts.append((hi - lo).astype(np.float32)[:, None])
    return jnp.asarray(np.stack(mats), BF16), jnp.asarray(np.stack(cnts), F32)


def _mix_bwd_body(hf_ref, xc_ref, gg_ref, xp_ref, mr_ref, mp_ref, x_ref,
                  wgb_ref, ba_ref, bx_ref, lam_ref, h0_ref,
                  pm_ref, pc_ref, pw_ref, ps_ref, wr_ref, wp_ref, wo_ref,
                  npost_ref, npre_ref, g1_ref, sh2_ref, sc2_ref, rw_ref, rb_ref,
                  xmid_ref, v_ref, idx_ref, wts_ref, rank_ref, cnt_ref,
                  hb_s, h_s, cnt_s):
    b_id, i = pl.program_id(0), pl.program_id(1)
    tt, d = x_ref.shape[1], x_ref.shape[2]
    n_exp = rw_ref.shape[1]

    @pl.when(i == 0)
    def _():
        h_s[...] = h0_ref[0]

    @pl.when((i == 0) & (b_id == 0))
    def _():
        cnt_s[...] = jnp.zeros_like(cnt_s)

    xc = xc_ref[0].astype(F32)
    sp = _softplus(-lam_ref[1:2, :])
    a, b = _lru_coeffs(xc, wgb_ref, ba_ref[1:2, :], bx_ref[1:2, :], sp)

    def store(gi, hg):
        hb_s[gi * SUBLANES:(gi + 1) * SUBLANES, :] = hg

    h_s[...] = _scan_tile(a, b, h_s[...], True, store)

    gg = gg_ref[0].astype(F32)
    gelu = 0.5 * gg * (1.0 + jnp.tanh(0.7978845608028654 * (gg + 0.044715 * gg * gg * gg)))
    y_rnn = (hf_ref[0] + hb_s[...]) * gelu

    xpb = xp_ref[0]
    grp = d // len(POOL_WINDOWS)
    y_parts = []
    for gi in range(len(POOL_WINDOWS)):
        cols = slice(gi * grp, (gi + 1) * grp)
        rows = []
        for c0 in range(0, tt, POOL_CHUNK):
            xg = xpb[c0:c0 + POOL_CHUNK, cols]
            mean = _dot(pm_ref[gi], xg) / pc_ref[gi]
            rows.append((mean - xg.astype(F32)).astype(BF16))
        dg = rows[0] if len(rows) == 1 else jnp.concatenate(rows, axis=0)
        y_parts.append(_dot(dg, pw_ref[gi]) * ps_ref[:, cols])
    y_pool = jnp.concatenate(y_parts, axis=1)

    merged = (_sigmoid(mr_ref[0].astype(F32)) * _dot(y_rnn.astype(BF16), wr_ref[...])
              + _sigmoid(mp_ref[0].astype(F32)) * _dot(y_pool.astype(BF16), wp_ref[...]))
    m_lat = _dot(merged.astype(BF16), wo_ref[...])
    x_mid = x_ref[0] + g1_ref[0] * _rmsnorm(m_lat, npost_ref[...])
    xmid_ref[0] = x_mid
    v = _modulate(_rmsnorm(x_mid, npre_ref[...]), sh2_ref[0], sc2_ref[0])
    v_ref[0] = v.reshape(tt, SUBLANES, LANES)

    logits = _dot3(v, rw_ref[...]) + rb_ref[...]
    lane = lax.broadcasted_iota(jnp.int32, (tt, n_exp), 1).astype(F32)
    work = logits
    vals, idxs, sels = [], [], []
    for _ in range(TOP_K):
        m = jnp.max(work, axis=-1, keepdims=True)
        idx = jnp.min(jnp.where(work == m, lane, float(n_exp)), axis=-1, keepdims=True)
        sel = lane == idx
        vals.append(m)
        idxs.append(idx)
        sels.append(sel)
        work = jnp.where(sel, -jnp.inf, work)
    exps = [jnp.exp(vk - vals[0]) for vk in vals]
    den = exps[0] + exps[1] + exps[2] + exps[3]
    anyf = jnp.zeros((tt, n_exp), F32)
    for sel in sels:
        anyf = anyf + sel.astype(F32)
    r_i = lax.broadcasted_iota(jnp.int32, (tt, tt), 0)
    c_i = lax.broadcasted_iota(jnp.int32, (tt, tt), 1)
    lower = (c_i < r_i).astype(BF16)
    before = _dot(lower, anyf.astype(BF16)) + cnt_s[...]
    cnt_new = cnt_s[...] + jnp.sum(anyf, axis=0, keepdims=True)
    cnt_s[...] = cnt_new
    cnt_ref[...] = cnt_new

    lane_o = lax.broadcasted_iota(jnp.int32, (tt, LANES), 1)
    idx_o = jnp.zeros((tt, LANES), jnp.int32)
    wts_o = jnp.zeros((tt, LANES), F32)
    rank_o = jnp.zeros((tt, LANES), jnp.int32)
    for k in range(TOP_K):
        rk = jnp.sum(jnp.where(sels[k], before, 0.0), axis=-1, keepdims=True)
        idx_o = jnp.where(lane_o == k, idxs[k].astype(jnp.int32), idx_o)
        wts_o = jnp.where(lane_o == k, exps[k] / den, wts_o)
        rank_o = jnp.where(lane_o == k, rk.astype(jnp.int32), rank_o)
    idx_ref[0] = idx_o
    wts_ref[0] = wts_o
    rank_ref[0] = rank_o


def _mix_bwd_call(hf, xc, gg, xp, mr, mp, x, wgb, ba, bx, lam, h0b, pm, pc, pw, ps, wr, wp, wo,
                  npost, npre, g1, sh2, sc2, rw, rb):
    bsz, t, d = x.shape
    nt = t // TT
    n_exp = rw.shape[1]
    rev = lambda b, i: (b, nt - 1 - i, 0)
    tile = pl.BlockSpec((1, TT, d), rev)
    small = pl.BlockSpec((1, TT, LANES), rev)
    per_b = pl.BlockSpec((1, 1, d), lambda b, i: (b, 0, 0))
    full = lambda shp: pl.BlockSpec(shp, lambda b, i: (0,) * len(shp))
    return pl.pallas_call(
        _mix_bwd_body,
        out_shape=(jax.ShapeDtypeStruct((bsz, t, d), F32),
                   jax.ShapeDtypeStruct((bsz, t, SUBLANES, LANES), F32),
                   jax.ShapeDtypeStruct((bsz, t, LANES), jnp.int32),
                   jax.ShapeDtypeStruct((bsz, t, LANES), F32),
                   jax.ShapeDtypeStruct((bsz, t, LANES), jnp.int32),
                   jax.ShapeDtypeStruct((1, n_exp), F32)),
        grid=(bsz, nt),
        in_specs=[tile] * 7 + [full(wgb.shape), full(ba.shape), full(bx.shape), full(lam.shape), per_b,
                               full(pm.shape), full(pc.shape), full(pw.shape), full((1, d)),
                               full((d, d)), full((d, d)), full((d, d)),
                               full((1, d)), full((1, d)), per_b, per_b, per_b,
                               full(rw.shape), full((1, n_exp))],
        out_specs=(tile, pl.BlockSpec((1, TT, SUBLANES, LANES), lambda b, i: (b, nt - 1 - i, 0, 0)),
                   small, small, small, full((1, n_exp))),
        scratch_shapes=[pltpu.VMEM((TT, d), F32), pltpu.VMEM((1, d), F32), pltpu.VMEM((1, n_exp), F32)],
        compiler_params=pltpu.CompilerParams(
            dimension_semantics=("arbitrary", "arbitrary"), vmem_limit_bytes=VMEM_LIMIT),
        name="mix_bwd",
    )(hf, xc, gg, xp, mr, mp, x, wgb, ba, bx, lam, h0b, pm, pc, pw, ps, wr, wp, wo,
      npost, npre, g1, sh2, sc2, rw, rb)


def _dispatch_body(n_pad_blocks, pos_ref, pad_ref, zero_ref, v_ref, xs_hbm, sem):
    n_pad = pad_ref.shape[0]

    def row_copy(src_row, dst_row):
        return pltpu.make_async_copy(v_ref.at[src_row], xs_hbm.at[dst_row], sem)

    def pad_copy(q):
        return pltpu.make_async_copy(zero_ref.at[0], xs_hbm.at[pad_ref[q]], sem)

    def issue(t, carry):
        for k in range(TOP_K):
            row_copy(t, pos_ref[t * TOP_K + k]).start(priority=k % DMA_PRIORITIES)
        return carry

    def issue_pad(q, carry):
        pad_copy(q).start()
        return carry

    has_pads = pl.program_id(0) < n_pad_blocks
    lax.fori_loop(0, TD, issue, 0, unroll=ISSUE_UNROLL)

    @pl.when(has_pads)
    def _():
        lax.fori_loop(0, n_pad, issue_pad, 0, unroll=ISSUE_UNROLL)

    def drain(t, carry):
        for k in range(TOP_K):
            row_copy(t, pos_ref[t * TOP_K + k]).wait()
        return carry

    def drain_pad(q, carry):
        pad_copy(q).wait()
        return carry

    lax.fori_loop(0, TD, drain, 0, unroll=ISSUE_UNROLL)

    @pl.when(has_pads)
    def _():
        lax.fori_loop(0, n_pad, drain_pad, 0, unroll=ISSUE_UNROLL)


def _dispatch_call(pos_flat, pad_slots, v3, n_slots):
    n = v3.shape[0]
    slab = v3.shape[1:]
    steps = n // TD
    n_pad_blocks = pad_slots.shape[0] // PAD_BLOCK
    assert n_pad_blocks <= steps
    return pl.pallas_call(
        functools.partial(_dispatch_body, n_pad_blocks),
        out_shape=jax.ShapeDtypeStruct((n_slots,) + slab, F32),
        grid=(steps,),
        in_specs=[pl.BlockSpec((TD * TOP_K,), lambda i: (i,), memory_space=pltpu.SMEM),
                  pl.BlockSpec((PAD_BLOCK,), lambda i: (jnp.minimum(i, n_pad_blocks - 1),),
                               memory_space=pltpu.SMEM),
                  pl.BlockSpec((1,) + slab, lambda i: (0, 0, 0)),
                  pl.BlockSpec((TD,) + slab, lambda i: (i, 0, 0))],
        out_specs=pl.BlockSpec(memory_space=pl.ANY),
        scratch_shapes=[pltpu.SemaphoreType.DMA],
        compiler_params=pltpu.CompilerParams(dimension_semantics=("arbitrary",)),
        name="dispatch",
    )(pos_flat, pad_slots, jnp.zeros((1,) + slab, F32), v3)


def _expert_body(te_ref, blk_ref, nvalid_ref, xs_ref, w1_ref, b1_ref, w2_ref, b2_ref, y_ref, w1b_s, w2b_s):
    j = pl.program_id(0)
    ff = w2_ref.shape[1]
    prev = te_ref[jnp.maximum(j - 1, 0)]

    @pl.when((j == 0) | (te_ref[j] != prev))
    def _():
        w1b_s[...] = w1_ref[0].astype(BF16)
        w2b_s[...] = w2_ref[0].astype(BF16)

    @pl.when(j < nvalid_ref[0])
    def _():
        tm = xs_ref.shape[0]
        x = xs_ref[...].reshape(tm, w1_ref.shape[1])
        z = _dot(x.astype(BF16), w1b_s[...]) + b1_ref[0]
        glu = jnp.minimum(z[:, :ff], SWIGLU_LIMIT)
        lin = jnp.clip(z[:, ff:], -SWIGLU_LIMIT, SWIGLU_LIMIT)
        act = glu * _sigmoid(SWIGLU_ALPHA * glu) * (lin + 1.0)
        y = _dot(act.astype(BF16), w2b_s[...]) + b2_ref[0]
        y_ref[...] = y.reshape(y_ref.shape)

    @pl.when(j >= nvalid_ref[0])
    def _():
        y_ref[...] = jnp.zeros_like(y_ref)


def _expert_call(te, blk, nvalid, xs, w1, b1, w2, b2):
    n_slots = xs.shape[0]
    slab = xs.shape[1:]
    n_exp, d, ff2 = w1.shape
    ff = w2.shape[1]
    grid_spec = pltpu.PrefetchScalarGridSpec(
        num_scalar_prefetch=3,
        grid=(n_slots // TM,),
        in_specs=[pl.BlockSpec((TM,) + slab, lambda j, te, blk, nv: (blk[j], 0, 0)),
                  pl.BlockSpec((1, d, ff2), lambda j, te, blk, nv: (te[j], 0, 0)),
                  pl.BlockSpec((1, 1, ff2), lambda j, te, blk, nv: (te[j], 0, 0)),
                  pl.BlockSpec((1, ff, d), lambda j, te, blk, nv: (te[j], 0, 0)),
                  pl.BlockSpec((1, 1, d), lambda j, te, blk, nv: (te[j], 0, 0))],
        out_specs=pl.BlockSpec((TM,) + slab, lambda j, te, blk, nv: (j, 0, 0)),
        scratch_shapes=[pltpu.VMEM((d, ff2), BF16), pltpu.VMEM((ff, d), BF16)],
    )
    return pl.pallas_call(
        _expert_body,
        out_shape=jax.ShapeDtypeStruct((n_slots,) + slab, F32),
        grid_spec=grid_spec,
        compiler_params=pltpu.CompilerParams(
            dimension_semantics=("arbitrary",), vmem_limit_bytes=VMEM_LIMIT),
        name="experts",
    )(te, blk, nvalid, xs, w1, b1, w2, b2)


def _combine_body(pos_ref, posn_ref, wts_ref, xmid_ref, g2_ref, npost_ref, y_hbm, o_ref, buf, sem):
    i = pl.program_id(0)
    cur = lax.rem(i, 2)

    def row_copy(p_ref, slot, t, k):
        return pltpu.make_async_copy(y_hbm.at[p_ref[t * TOP_K + k]], buf.at[slot, k, t], sem.at[slot])

    def issue(p_ref, slot):
        def body(t, carry):
            for k in range(TOP_K):
                row_copy(p_ref, slot, t, k).start(priority=k % DMA_PRIORITIES)
            return carry

        lax.fori_loop(0, TD, body, 0, unroll=ISSUE_UNROLL)

    @pl.when(i == 0)
    def _():
        issue(pos_ref, 0)

    @pl.when(i + 1 < pl.num_programs(0))
    def _():
        issue(posn_ref, 1 - cur)

    def drain(t, carry):
        for k in range(TOP_K):
            row_copy(pos_ref, cur, t, k).wait()
        return carry

    lax.fori_loop(0, TD, drain, 0, unroll=ISSUE_UNROLL)

    w = wts_ref[...]
    f = w[:, 0:1] * buf[cur, 0].reshape(xmid_ref.shape)
    for k in range(1, TOP_K):
        f = f + w[:, k:k + 1] * buf[cur, k].reshape(xmid_ref.shape)
    o_ref[...] = xmid_ref[...] + g2_ref[0] * _rmsnorm(f, npost_ref[...])


def _combine_call(pos_flat, wts, xmid2, g2, npost, ys, t_per_batch):
    n, d = xmid2.shape
    steps = n // TD
    pos_spec = lambda f: pl.BlockSpec((TD * TOP_K,), f, memory_space=pltpu.SMEM)
    return pl.pallas_call(
        _combine_body,
        out_shape=jax.ShapeDtypeStruct((n, d), F32),
        grid=(steps,),
        in_specs=[pos_spec(lambda i: (i,)),
                  pos_spec(lambda i: (jnp.minimum(i + 1, steps - 1),)),
                  pl.BlockSpec((TD, LANES), lambda i: (i, 0)),
                  pl.BlockSpec((TD, d), lambda i: (i, 0)),
                  pl.BlockSpec((1, 1, d), lambda i: ((i * TD) // t_per_batch, 0, 0)),
                  pl.BlockSpec((1, d), lambda i: (0, 0)),
                  pl.BlockSpec(memory_space=pl.ANY)],
        out_specs=pl.BlockSpec((TD, d), lambda i: (i, 0)),
        scratch_shapes=[pltpu.VMEM((2, TOP_K, TD) + ys.shape[1:], F32), pltpu.SemaphoreType.DMA((2,))],
        compiler_params=pltpu.CompilerParams(dimension_semantics=("arbitrary",)),
        name="combine",
    )(pos_flat, pos_flat, wts, xmid2, g2, npost, ys)


def _count_le(ends, q):
    return jnp.sum((ends[None, :] <= q[:, None]).astype(jnp.int32), axis=1)


def _gate_weights(wa, wx):
    return jnp.concatenate([wa, wx], axis=-1).astype(BF16)


def kernel(x, c, ctx, c_ctx, w_mod, b_mod, norm_pre_mix, norm_post_mix, norm_pre_ffn, norm_post_ffn, w_in, conv_w, conv_b, lru_wa, lru_ba, lru_wx, lru_bx, lru_lambda, pool_w, pool_scale, w_rnn_proj, w_pool_proj, w_out, router_w, router_b, exp_w1, exp_b1, exp_w2, exp_b2):
    bsz, t, d = x.shape
    assert w_mod.shape[0] == 1, "single-layer block"
    assert t % TT == 0 and TT % POOL_CHUNK == 0 and POOL_CHUNK % GRID_W == 0 and t % TD == 0
    assert d == SUBLANES * LANES, "a token row must be exactly one (8, 128) slab"
    n = bsz * t
    n_exp = router_w.shape[-1]
    assert (n * TOP_K) % TM == 0 and (n_exp * TM) % PAD_BLOCK == 0
    row = lambda a: a.reshape(1, -1)

    pad = jnp.zeros((SUBLANES - (bsz + 1) % SUBLANES, d), F32)
    cc = jnp.concatenate([c, c_ctx[None, :], pad], axis=0)
    mod = _mod_call(cc, w_mod[0], row(b_mod[0]))
    lat = lambda j: mod[:bsz, j * d:(j + 1) * d].reshape(bsz, 1, d)
    sh1, sc1, g1, sh2, sc2, g2 = (lat(j) for j in range(6))
    csh1, csc1 = mod[bsz:bsz + 1, 0:d], mod[bsz:bsz + 1, d:2 * d]

    w_in_b = w_in[0].astype(BF16)
    wgf = _gate_weights(lru_wa[0, 0], lru_wx[0, 0])
    wgb = _gate_weights(lru_wa[0, 1], lru_wx[0, 1])
    ba, bx, lam = lru_ba[0], lru_bx[0], lru_lambda[0]
    cw, cb = conv_w[0], row(conv_b[0])
    npre_mix = row(norm_pre_mix[0])

    h0f, h0b = _ctx_call(ctx, npre_mix, csh1, csc1, w_in_b, cw, cb, wgf, wgb, ba, bx, lam)

    hf, xc, gg, xp, mr, mp = _mix_fwd_call(x, npre_mix, sh1, sc1, w_in_b, cw, cb, wgf, ba, bx, lam, h0f)

    pm, pc = _pool_consts()
    xmid, v, idx_o, wts_o, rank_o, cnt = _mix_bwd_call(
        hf, xc, gg, xp, mr, mp, x, wgb, ba, bx, lam, h0b,
        pm, pc, pool_w[0].astype(BF16), row(pool_scale[0]),
        w_rnn_proj[0].astype(BF16), w_pool_proj[0].astype(BF16), w_out[0].astype(BF16),
        row(norm_post_mix[0]), row(norm_pre_ffn[0]), g1, sh2, sc2, router_w[0], row(router_b[0]))

    idx = idx_o.reshape(n, LANES)[:, :TOP_K]
    rank = rank_o.reshape(n, LANES)[:, :TOP_K]
    counts = cnt[0].astype(jnp.int32)
    tiles_e = (counts + TM - 1) // TM
    tile_end = jnp.cumsum(tiles_e)
    offs = (tile_end - tiles_e) * TM
    pos_flat = (offs[idx] + rank).reshape(n * TOP_K)
    n_tiles = (n * TOP_K) // TM + n_exp
    n_valid = tile_end[-1]
    jj = jnp.arange(n_tiles, dtype=jnp.int32)
    blk = jnp.minimum(jj, n_valid - 1)
    te = jnp.minimum(_count_le(tile_end, blk), n_exp - 1)

    pad_e = tiles_e * TM - counts
    pad_end = jnp.cumsum(pad_e)
    qq = jnp.arange(n_exp * TM, dtype=jnp.int32)
    e_q = jnp.minimum(_count_le(pad_end, qq), n_exp - 1)
    slot_in = (offs + counts)[e_q] + qq - (pad_end - pad_e)[e_q]
    slot_tail = n_valid * TM + qq - pad_end[-1]
    pad_slots = jnp.where(qq < pad_end[-1], slot_in, slot_tail).astype(jnp.int32)

    xs = _dispatch_call(pos_flat, pad_slots, v.reshape((n,) + v.shape[2:]), n_tiles * TM)
    ys = _expert_call(te, blk, n_valid.reshape(1), xs, exp_w1[0], exp_b1[0].reshape(n_exp, 1, -1),
                      exp_w2[0], exp_b2[0].reshape(n_exp, 1, -1))
    out = _combine_call(pos_flat, wts_o.reshape(n, LANES), xmid.reshape(n, d), g2, row(norm_post_ffn[0]), ys, t)
    return out.reshape(bsz, t, d)
```

```python
import functools

import numpy as np
import jax
import jax.numpy as jnp
from jax import lax
from jax.experimental import pallas as pl
from jax.experimental.pallas import tpu as pltpu

F32 = jnp.float32
BF16 = jnp.bfloat16

RNN_HEADS = 4
CONV_W = 4
LRU_C = 8.0
POOL_WINDOWS = (2, 4, 8, 16)
GRID_W = 64
TOP_K = 4
SWIGLU_LIMIT = 7.0
SWIGLU_ALPHA = 1.702
EPS = 1e-6

SUBLANES = 8
LANES = 128
VMEM_LIMIT = 56 * 1024 * 1024

TT = 256
POOL_CHUNK = 256
TM = 256
TD = 128
PAD_BLOCK = 128
ISSUE_UNROLL = 8
DMA_PRIORITIES = 2
RANK_RADIX = 1 << 16


def _sigmoid(x):
    return 1.0 / (1.0 + jnp.exp(-x))


def _softplus(z):
    return jnp.maximum(z, 0.0) + jnp.log1p(jnp.exp(-jnp.abs(z)))


def _rmsnorm(x, g):
    ms = jnp.mean(x * x, axis=-1, keepdims=True)
    return (x * lax.rsqrt(ms + EPS)) * g


def _modulate(u, shift, scale):
    return u * (1.0 + scale) + shift


def _dot(a, b):
    return jnp.dot(a, b, preferred_element_type=F32)


def _split(x):
    hi = x.astype(BF16)
    return hi, (x - hi.astype(F32)).astype(BF16)


def _dot3(a, b):
    ah, al = _split(a)
    bh, bl = _split(b)
    return _dot(ah, bh) + (_dot(al, bh) + _dot(ah, bl))


def _conv(prev8, xr, next8, cw, cb):
    t = xr.shape[0]
    ext = jnp.concatenate([prev8, xr, next8], axis=0)
    acc = cb + cw[0:1, :] * ext[SUBLANES - 2:SUBLANES - 2 + t, :]
    for k in range(1, CONV_W):
        off = SUBLANES - 2 + k
        acc = acc + cw[k:k + 1, :] * ext[off:off + t, :]
    return acc


def _lru_coeffs(xc, wg_ref, ba, bx, sp):
    hd = xc.shape[1] // RNN_HEADS
    xcb = xc.astype(BF16)
    a_parts, b_parts = [], []
    for h in range(RNN_HEADS):
        cols = slice(h * hd, (h + 1) * hd)
        z = _dot(xcb[:, cols], wg_ref[h])
        r = _sigmoid(z[:, :hd] + ba[:, cols])
        i = _sigmoid(z[:, hd:] + bx[:, cols])
        a = jnp.exp((-LRU_C) * r * sp[:, cols])
        b = jnp.sqrt(1.0 - a * a) * (i * xc[:, cols])
        a_parts.append(a)
        b_parts.append(b)
    return jnp.concatenate(a_parts, axis=1), jnp.concatenate(b_parts, axis=1)


def _scan_tile(a, b, h_in, reverse, store):
    t, c = a.shape
    g = t // SUBLANES
    a3 = a.reshape(g, SUBLANES, c)
    b3 = b.reshape(g, SUBLANES, c)
    row = lax.broadcasted_iota(jnp.int32, (g, SUBLANES, c), 1)
    for s in (1, 2, 4):
        if reverse:
            shift, m = SUBLANES - s, row < SUBLANES - s
        else:
            shift, m = s, row >= s
        ra = pltpu.roll(a3, shift, axis=1)
        rb = pltpu.roll(b3, shift, axis=1)
        b3 = a3 * jnp.where(m, rb, 0.0) + b3
        a3 = a3 * jnp.where(m, ra, 1.0)
    h = h_in
    order = range(g - 1, -1, -1) if reverse else range(g)
    for gi in order:
        hg = a3[gi] * h + b3[gi]
        store(gi, hg)
        h = hg[0:1, :] if reverse else hg[SUBLANES - 1:SUBLANES, :]
    return h


def _mod_body(c_ref, w_ref, b_ref, o_ref):
    c = c_ref[...]
    s = c * _sigmoid(c)
    o_ref[...] = _dot3(s, w_ref[...]) + b_ref[...]


def _mod_call(cc, w_mod, b_mod):
    d = cc.shape[1]
    n = w_mod.shape[1] // d
    return pl.pallas_call(
        _mod_body,
        out_shape=jax.ShapeDtypeStruct((cc.shape[0], n * d), F32),
        grid=(n,),
        in_specs=[pl.BlockSpec(cc.shape, lambda j: (0, 0)),
                  pl.BlockSpec((d, d), lambda j: (0, j)),
                  pl.BlockSpec((1, d), lambda j: (0, j))],
        out_specs=pl.BlockSpec((cc.shape[0], d), lambda j: (0, j)),
        name="mod",
    )(cc, w_mod, b_mod)


def _ctx_body(ctx_ref, g_ref, sh_ref, sc_ref, w_ref, cw_ref, cb_ref, wgf_ref, wgb_ref,
              ba_ref, bx_ref, lam_ref, hf_ref, hb_ref):
    d = ctx_ref.shape[2]
    u = _modulate(_rmsnorm(ctx_ref[0], g_ref[...]), sh_ref[...], sc_ref[...])
    xr = _dot(u.astype(BF16), w_ref[...])
    z8 = jnp.zeros((SUBLANES, d), F32)
    xc = _conv(z8, xr, z8, cw_ref[...], cb_ref[...])
    h0 = jnp.zeros((1, d), F32)
    for di, (wg_ref, out_ref) in enumerate(((wgf_ref, hf_ref), (wgb_ref, hb_ref))):
        sp = _softplus(-lam_ref[di:di + 1, :])
        a, b = _lru_coeffs(xc, wg_ref, ba_ref[di:di + 1, :], bx_ref[di:di + 1, :], sp)
        out_ref[0] = _scan_tile(a, b, h0, di == 1, lambda gi, hg: None)


def _ctx_call(ctx, g, sh, sc, w_in_b, cw, cb, wgf, wgb, ba, bx, lam):
    bsz, tc, d = ctx.shape
    full = lambda shp: pl.BlockSpec(shp, lambda b: (0,) * len(shp))
    return pl.pallas_call(
        _ctx_body,
        out_shape=(jax.ShapeDtypeStruct((bsz, 1, d), F32),) * 2,
        grid=(bsz,),
        in_specs=[pl.BlockSpec((1, tc, d), lambda b: (b, 0, 0)),
                  full((1, d)), full((1, d)), full((1, d)),
                  pl.BlockSpec((d, d), lambda b: (0, 0)),
                  full(cw.shape), full((1, d)), full(wgf.shape), full(wgb.shape),
                  full(ba.shape), full(bx.shape), full(lam.shape)],
        out_specs=(pl.BlockSpec((1, 1, d), lambda b: (b, 0, 0)),) * 2,
        compiler_params=pltpu.CompilerParams(vmem_limit_bytes=VMEM_LIMIT),
        name="ctx",
    )(ctx, g, sh, sc, w_in_b, cw, cb, wgf, wgb, ba, bx, lam)


def _mix_fwd_body(x_ref, xn_ref, g_ref, sh_ref, sc_ref, win_ref, cw_ref, cb_ref, wgf_ref,
                  ba_ref, bx_ref, lam_ref, h0_ref,
                  hf_ref, xc_ref, gg_ref, xp_ref, mr_ref, mp_ref, tail_s, h_s):
    i = pl.program_id(1)
    last = pl.num_programs(1) - 1
    tt, d = x_ref.shape[1], x_ref.shape[2]

    @pl.when(i == 0)
    def _():
        tail_s[...] = jnp.zeros_like(tail_s)
        h_s[...] = h0_ref[0]

    g, sh, sc = g_ref[...], sh_ref[0], sc_ref[0]
    ub = _modulate(_rmsnorm(x_ref[0], g), sh, sc).astype(BF16)
    unb = _modulate(_rmsnorm(xn_ref[0], g), sh, sc).astype(BF16)
    w_rnn = win_ref[:, 0:d]
    xr = _dot(ub, w_rnn)
    xrn = jnp.where(i == last, 0.0, _dot(unb, w_rnn))
    xc = _conv(tail_s[...], xr, xrn, cw_ref[...], cb_ref[...])
    tail_s[...] = xr[tt - SUBLANES:tt, :]
    xc_ref[0] = xc.astype(BF16)
    for j, ref in enumerate((gg_ref, xp_ref, mr_ref, mp_ref)):
        ref[0] = _dot(ub, win_ref[:, (j + 1) * d:(j + 2) * d]).astype(BF16)

    sp = _softplus(-lam_ref[0:1, :])
    a, b = _lru_coeffs(xc, wgf_ref, ba_ref[0:1, :], bx_ref[0:1, :], sp)

    def store(gi, hg):
        hf_ref[0, gi * SUBLANES:(gi + 1) * SUBLANES, :] = hg

    h_s[...] = _scan_tile(a, b, h_s[...], False, store)


def _mix_fwd_call(x, g, sh, sc, w_in_b, cw, cb, wgf, ba, bx, lam, h0f):
    bsz, t, d = x.shape
    nt = t // TT
    nblk8 = t // SUBLANES
    per_b = pl.BlockSpec((1, 1, d), lambda b, i: (b, 0, 0))
    full = lambda shp: pl.BlockSpec(shp, lambda b, i: (0,) * len(shp))
    tile = pl.BlockSpec((1, TT, d), lambda b, i: (b, i, 0))
    return pl.pallas_call(
        _mix_fwd_body,
        out_shape=(jax.ShapeDtypeStruct((bsz, t, d), F32),) + (jax.ShapeDtypeStruct((bsz, t, d), BF16),) * 5,
        grid=(bsz, nt),
        in_specs=[tile,
                  pl.BlockSpec((1, SUBLANES, d),
                               lambda b, i: (b, jnp.minimum((i + 1) * (TT // SUBLANES), nblk8 - 1), 0)),
                  full((1, d)), per_b, per_b,
                  full(w_in_b.shape), full(cw.shape), full((1, d)), full(wgf.shape),
                  full(ba.shape), full(bx.shape), full(lam.shape), per_b],
        out_specs=(tile,) * 6,
        scratch_shapes=[pltpu.VMEM((SUBLANES, d), F32), pltpu.VMEM((1, d), F32)],
        compiler_params=pltpu.CompilerParams(
            dimension_semantics=("arbitrary", "arbitrary"), vmem_limit_bytes=VMEM_LIMIT),
        name="mix_fwd",
    )(x, x, g, sh, sc, w_in_b, cw, cb, wgf, ba, bx, lam, h0f)


def _pool_consts():
    p = np.arange(POOL_CHUNK)
    pos, line = p % GRID_W, p // GRID_W
    mats, cnts = [], []
    for w in POOL_WINDOWS:
        lo = np.clip(pos - w // 2, 0, GRID_W)
        hi = np.clip(pos + w - w // 2, 0, GRID_W)
        m = (line[:, None] == line[None, :]) & (pos[None, :] >= lo[:, None]) & (pos[None, :] < hi[:, None])
        mats.append(m.astype(np.float32))
        cnts.append((hi - lo).astype(np.float32)[:, None])
    return jnp.asarray(np.stack(mats), BF16), jnp.asarray(np.stack(cnts), F32)


def _mix_bwd_body(hf_ref, xc_ref, gg_ref, xp_ref, mr_ref, mp_ref, x_ref,
                  wgb_ref, ba_ref, bx_ref, lam_ref, h0_ref,
                  pm_ref, pc_ref, pw_ref, ps_ref, wr_ref, wp_ref, wo_ref,
                  npost_ref, npre_ref, g1_ref, sh2_ref, sc2_ref, rw_ref, rb_ref,
                  xmid_ref, v_ref, code_ref, wts_ref, cnt_ref,
                  hb_s, h_s, cnt_s):
    b_id, i = pl.program_id(0), pl.program_id(1)
    tt, d = x_ref.shape[1], x_ref.shape[2]
    n_exp = rw_ref.shape[1]

    @pl.when(i == 0)
    def _():
        h_s[...] = h0_ref[0]

    @pl.when((i == 0) & (b_id == 0))
    def _():
        cnt_s[...] = jnp.zeros_like(cnt_s)

    xc = xc_ref[0].astype(F32)
    sp = _softplus(-lam_ref[1:2, :])
    a, b = _lru_coeffs(xc, wgb_ref, ba_ref[1:2, :], bx_ref[1:2, :], sp)

    def store(gi, hg):
        hb_s[gi * SUBLANES:(gi + 1) * SUBLANES, :] = hg

    h_s[...] = _scan_tile(a, b, h_s[...], True, store)

    gg = gg_ref[0].astype(F32)
    gelu = 0.5 * gg * (1.0 + jnp.tanh(0.7978845608028654 * (gg + 0.044715 * gg * gg * gg)))
    y_rnn = (hf_ref[0] + hb_s[...]) * gelu

    xpb = xp_ref[0]
    grp = d // len(POOL_WINDOWS)
    y_parts = []
    for gi in range(len(POOL_WINDOWS)):
        cols = slice(gi * grp, (gi + 1) * grp)
        rows = []
        for c0 in range(0, tt, POOL_CHUNK):
            xg = xpb[c0:c0 + POOL_CHUNK, cols]
            mean = _dot(pm_ref[gi], xg) / pc_ref[gi]
            rows.append((mean - xg.astype(F32)).astype(BF16))
        dg = rows[0] if len(rows) == 1 else jnp.concatenate(rows, axis=0)
        y_parts.append(_dot(dg, pw_ref[gi]) * ps_ref[:, cols])
    y_pool = jnp.concatenate(y_parts, axis=1)

    merged = (_sigmoid(mr_ref[0].astype(F32)) * _dot(y_rnn.astype(BF16), wr_ref[...])
              + _sigmoid(mp_ref[0].astype(F32)) * _dot(y_pool.astype(BF16), wp_ref[...]))
    m_lat = _dot(merged.astype(BF16), wo_ref[...])
    x_mid = x_ref[0] + g1_ref[0] * _rmsnorm(m_lat, npost_ref[...])
    xmid_ref[0] = x_mid
    v = _modulate(_rmsnorm(x_mid, npre_ref[...]), sh2_ref[0], sc2_ref[0])
    v_ref[0] = v.reshape(tt, SUBLANES, LANES)

    logits = _dot3(v, rw_ref[...]) + rb_ref[...]
    lane = lax.broadcasted_iota(jnp.int32, (tt, n_exp), 1).astype(F32)
    work = logits
    vals, idxs, sels = [], [], []
    for _ in range(TOP_K):
        m = jnp.max(work, axis=-1, keepdims=True)
        idx = jnp.min(jnp.where(work == m, lane, float(n_exp)), axis=-1, keepdims=True)
        sel = lane == idx
        vals.append(m)
        idxs.append(idx)
        sels.append(sel)
        work = jnp.where(sel, -jnp.inf, work)
    exps = [jnp.exp(vk - vals[0]) for vk in vals]
    den = exps[0] + exps[1] + exps[2] + exps[3]
    anyf = jnp.zeros((tt, n_exp), F32)
    for sel in sels:
        anyf = anyf + sel.astype(F32)
    r_i = lax.broadcasted_iota(jnp.int32, (tt, tt), 0)
    c_i = lax.broadcasted_iota(jnp.int32, (tt, tt), 1)
    lower = (c_i < r_i).astype(BF16)
    before = _dot(lower, anyf.astype(BF16)) + cnt_s[...]
    cnt_new = cnt_s[...] + jnp.sum(anyf, axis=0, keepdims=True)
    cnt_s[...] = cnt_new
    cnt_ref[...] = cnt_new

    lane_o = lax.broadcasted_iota(jnp.int32, (tt, LANES), 1)
    code_o = jnp.zeros((tt, LANES), jnp.int32)
    wts_o = jnp.zeros((tt, LANES), F32)
    for k in range(TOP_K):
        rk = jnp.sum(jnp.where(sels[k], before, 0.0), axis=-1, keepdims=True)
        code = idxs[k].astype(jnp.int32) * RANK_RADIX + rk.astype(jnp.int32)
        code_o = jnp.where(lane_o == k, code, code_o)
        wts_o = jnp.where(lane_o == k, exps[k] / den, wts_o)
    code_ref[0] = code_o
    wts_ref[0] = wts_o


def _mix_bwd_call(hf, xc, gg, xp, mr, mp, x, wgb, ba, bx, lam, h0b, pm, pc, pw, ps, wr, wp, wo,
                  npost, npre, g1, sh2, sc2, rw, rb):
    bsz, t, d = x.shape
    nt = t // TT
    n_exp = rw.shape[1]
    rev = lambda b, i: (b, nt - 1 - i, 0)
    tile = pl.BlockSpec((1, TT, d), rev)
    small = pl.BlockSpec((1, TT, LANES), rev)
    per_b = pl.BlockSpec((1, 1, d), lambda b, i: (b, 0, 0))
    full = lambda shp: pl.BlockSpec(shp, lambda b, i: (0,) * len(shp))
    return pl.pallas_call(
        _mix_bwd_body,
        out_shape=(jax.ShapeDtypeStruct((bsz, t, d), F32),
                   jax.ShapeDtypeStruct((bsz, t, SUBLANES, LANES), F32),
                   jax.ShapeDtypeStruct((bsz, t, LANES), jnp.int32),
                   jax.ShapeDtypeStruct((bsz, t, LANES), F32),
                   jax.ShapeDtypeStruct((1, n_exp), F32)),
        grid=(bsz, nt),
        in_specs=[tile] * 7 + [full(wgb.shape), full(ba.shape), full(bx.shape), full(lam.shape), per_b,
                               full(pm.shape), full(pc.shape), full(pw.shape), full((1, d)),
                               full((d, d)), full((d, d)), full((d, d)),
                               full((1, d)), full((1, d)), per_b, per_b, per_b,
                               full(rw.shape), full((1, n_exp))],
        out_specs=(tile, pl.BlockSpec((1, TT, SUBLANES, LANES), lambda b, i: (b, nt - 1 - i, 0, 0)),
                   small, small, full((1, n_exp))),
        scratch_shapes=[pltpu.VMEM((TT, d), F32), pltpu.VMEM((1, d), F32), pltpu.VMEM((1, n_exp), F32)],
        compiler_params=pltpu.CompilerParams(
            dimension_semantics=("arbitrary", "arbitrary"), vmem_limit_bytes=VMEM_LIMIT),
        name="mix_bwd",
    )(hf, xc, gg, xp, mr, mp, x, wgb, ba, bx, lam, h0b, pm, pc, pw, ps, wr, wp, wo,
      npost, npre, g1, sh2, sc2, rw, rb)


def _dispatch_body(n_pad_blocks, pos_ref, pad_ref, zero_ref, v_ref, xs_hbm, sem):
    n_pad = pad_ref.shape[0]

    def row_copy(src_row, dst_row):
        return pltpu.make_async_copy(v_ref.at[src_row], xs_hbm.at[dst_row], sem)

    def pad_copy(q):
        return pltpu.make_async_copy(zero_ref.at[0], xs_hbm.at[pad_ref[q]], sem)

    def issue(t, carry):
        for k in range(TOP_K):
            row_copy(t, pos_ref[t * TOP_K + k]).start(priority=k % DMA_PRIORITIES)
        return carry

    def issue_pad(q, carry):
        pad_copy(q).start()
        return carry

    has_pads = pl.program_id(0) < n_pad_blocks
    lax.fori_loop(0, TD, issue, 0, unroll=ISSUE_UNROLL)

    @pl.when(has_pads)
    def _():
        lax.fori_loop(0, n_pad, issue_pad, 0, unroll=ISSUE_UNROLL)

    def drain(t, carry):
        for k in range(TOP_K):
            row_copy(t, pos_ref[t * TOP_K + k]).wait()
        return carry

    def drain_pad(q, carry):
        pad_copy(q).wait()
        return carry

    lax.fori_loop(0, TD, drain, 0, unroll=ISSUE_UNROLL)

    @pl.when(has_pads)
    def _():
        lax.fori_loop(0, n_pad, drain_pad, 0, unroll=ISSUE_UNROLL)


def _dispatch_call(pos_flat, pad_slots, v3, n_slots):
    n = v3.shape[0]
    slab = v3.shape[1:]
    steps = n // TD
    n_pad_blocks = pad_slots.shape[0] // PAD_BLOCK
    assert n_pad_blocks <= steps
    return pl.pallas_call(
        functools.partial(_dispatch_body, n_pad_blocks),
        out_shape=jax.ShapeDtypeStruct((n_slots,) + slab, F32),
        grid=(steps,),
        in_specs=[pl.BlockSpec((TD * TOP_K,), lambda i: (i,), memory_space=pltpu.SMEM),
                  pl.BlockSpec((PAD_BLOCK,), lambda i: (jnp.minimum(i, n_pad_blocks - 1),),
                               memory_space=pltpu.SMEM),
                  pl.BlockSpec((1,) + slab, lambda i: (0, 0, 0)),
                  pl.BlockSpec((TD,) + slab, lambda i: (i, 0, 0))],
        out_specs=pl.BlockSpec(memory_space=pl.ANY),
        scratch_shapes=[pltpu.SemaphoreType.DMA],
        compiler_params=pltpu.CompilerParams(dimension_semantics=("arbitrary",)),
        name="dispatch",
    )(pos_flat, pad_slots, jnp.zeros((1,) + slab, F32), v3)


def _expert_body(te_ref, blk_ref, nvalid_ref, slot_ref, nxt_ref,
                 xs_ref, w1_hbm, b1_ref, w2_hbm, b2_ref, y_ref, w1f_s, w2f_s, w1b_s, w2b_s, sem):
    j = pl.program_id(0)
    n_exp = w1_hbm.shape[0]
    ff = w2_hbm.shape[1]
    e = te_ref[j]
    prev = te_ref[jnp.maximum(j - 1, 0)]

    def weight_copies(expert, slot):
        return (pltpu.make_async_copy(w1_hbm.at[expert], w1f_s.at[slot], sem.at[slot, 0]),
                pltpu.make_async_copy(w2_hbm.at[expert], w2f_s.at[slot], sem.at[slot, 1]))

    @pl.when(j == 0)
    def _():
        for c in weight_copies(e, slot_ref[e]):
            c.start()

    @pl.when((j == 0) | (e != prev))
    def _():
        slot = slot_ref[e]
        for c in weight_copies(e, slot):
            c.wait()
        w1b_s[...] = w1f_s[slot].astype(BF16)
        w2b_s[...] = w2f_s[slot].astype(BF16)

        @pl.when(nxt_ref[e] < n_exp)
        def _():
            for c in weight_copies(nxt_ref[e], 1 - slot):
                c.start()

    @pl.when(j < nvalid_ref[0])
    def _():
        tm = xs_ref.shape[0]
        x = xs_ref[...].reshape(tm, w1_hbm.shape[1])
        z = _dot(x.astype(BF16), w1b_s[...]) + b1_ref[0]
        glu = jnp.minimum(z[:, :ff], SWIGLU_LIMIT)
        lin = jnp.clip(z[:, ff:], -SWIGLU_LIMIT, SWIGLU_LIMIT)
        act = glu * _sigmoid(SWIGLU_ALPHA * glu) * (lin + 1.0)
        y = _dot(act.astype(BF16), w2b_s[...]) + b2_ref[0]
        y_ref[...] = y.reshape(y_ref.shape)

    @pl.when(j >= nvalid_ref[0])
    def _():
        y_ref[...] = jnp.zeros_like(y_ref)


def _expert_call(te, blk, nvalid, slot, nxt, xs, w1, b1, w2, b2):
    n_slots = xs.shape[0]
    slab = xs.shape[1:]
    n_exp, d, ff2 = w1.shape
    ff = w2.shape[1]
    grid_spec = pltpu.PrefetchScalarGridSpec(
        num_scalar_prefetch=5,
        grid=(n_slots // TM,),
        in_specs=[pl.BlockSpec((TM,) + slab, lambda j, te, blk, *_: (blk[j], 0, 0)),
                  pl.BlockSpec(memory_space=pl.ANY),
                  pl.BlockSpec((1, 1, ff2), lambda j, te, *_: (te[j], 0, 0)),
                  pl.BlockSpec(memory_space=pl.ANY),
                  pl.BlockSpec((1, 1, d), lambda j, te, *_: (te[j], 0, 0))],
        out_specs=pl.BlockSpec((TM,) + slab, lambda j, *_: (j, 0, 0)),
        scratch_shapes=[pltpu.VMEM((2, d, ff2), F32), pltpu.VMEM((2, ff, d), F32),
                        pltpu.VMEM((d, ff2), BF16), pltpu.VMEM((ff, d), BF16),
                        pltpu.SemaphoreType.DMA((2, 2))],
    )
    return pl.pallas_call(
        _expert_body,
        out_shape=jax.ShapeDtypeStruct((n_slots,) + slab, F32),
        grid_spec=grid_spec,
        compiler_params=pltpu.CompilerParams(
            dimension_semantics=("arbitrary",), vmem_limit_bytes=VMEM_LIMIT),
        name="experts",
    )(te, blk, nvalid, slot, nxt, xs, w1, b1, w2, b2)


def _combine_body(pos_ref, posn_ref, wts_ref, xmid_ref, g2_ref, npost_ref, y_hbm, o_ref, buf, sem):
    i = pl.program_id(0)
    cur = lax.rem(i, 2)

    def row_copy(p_ref, slot, t, k):
        return pltpu.make_async_copy(y_hbm.at[p_ref[t * TOP_K + k]], buf.at[slot, k, t], sem.at[slot])

    def issue(p_ref, slot):
        def body(t, carry):
            for k in range(TOP_K):
                row_copy(p_ref, slot, t, k).start(priority=k % DMA_PRIORITIES)
            return carry

        lax.fori_loop(0, TD, body, 0, unroll=ISSUE_UNROLL)

    @pl.when(i == 0)
    def _():
        issue(pos_ref, 0)

    @pl.when(i + 1 < pl.num_programs(0))
    def _():
        issue(posn_ref, 1 - cur)

    def drain(t, carry):
        for k in range(TOP_K):
            row_copy(pos_ref, cur, t, k).wait()
        return carry

    lax.fori_loop(0, TD, drain, 0, unroll=ISSUE_UNROLL)

    w = wts_ref[...]
    f = w[:, 0:1] * buf[cur, 0].reshape(xmid_ref.shape)
    for k in range(1, TOP_K):
        f = f + w[:, k:k + 1] * buf[cur, k].reshape(xmid_ref.shape)
    o_ref[...] = xmid_ref[...] + g2_ref[0] * _rmsnorm(f, npost_ref[...])


def _combine_call(pos_flat, wts, xmid2, g2, npost, ys, t_per_batch):
    n, d = xmid2.shape
    steps = n // TD
    pos_spec = lambda f: pl.BlockSpec((TD * TOP_K,), f, memory_space=pltpu.SMEM)
    return pl.pallas_call(
        _combine_body,
        out_shape=jax.ShapeDtypeStruct((n, d), F32),
        grid=(steps,),
        in_specs=[pos_spec(lambda i: (i,)),
                  pos_spec(lambda i: (jnp.minimum(i + 1, steps - 1),)),
                  pl.BlockSpec((TD, LANES), lambda i: (i, 0)),
                  pl.BlockSpec((TD, d), lambda i: (i, 0)),
                  pl.BlockSpec((1, 1, d), lambda i: ((i * TD) // t_per_batch, 0, 0)),
                  pl.BlockSpec((1, d), lambda i: (0, 0)),
                  pl.BlockSpec(memory_space=pl.ANY)],
        out_specs=pl.BlockSpec((TD, d), lambda i: (i, 0)),
        scratch_shapes=[pltpu.VMEM((2, TOP_K, TD) + ys.shape[1:], F32), pltpu.SemaphoreType.DMA((2,))],
        compiler_params=pltpu.CompilerParams(dimension_semantics=("arbitrary",)),
        name="combine",
    )(pos_flat, pos_flat, wts, xmid2, g2, npost, ys)


def _count_le(ends, q):
    return jnp.sum((ends[None, :] <= q[:, None]).astype(jnp.int32), axis=1)


def _gate_weights(wa, wx):
    return jnp.concatenate([wa, wx], axis=-1).astype(BF16)


def kernel(x, c, ctx, c_ctx, w_mod, b_mod, norm_pre_mix, norm_post_mix, norm_pre_ffn, norm_post_ffn, w_in, conv_w, conv_b, lru_wa, lru_ba, lru_wx, lru_bx, lru_lambda, pool_w, pool_scale, w_rnn_proj, w_pool_proj, w_out, router_w, router_b, exp_w1, exp_b1, exp_w2, exp_b2):
    bsz, t, d = x.shape
    assert w_mod.shape[0] == 1, "single-layer block"
    assert t % TT == 0 and TT % POOL_CHUNK == 0 and POOL_CHUNK % GRID_W == 0 and t % TD == 0
    assert d == SUBLANES * LANES, "a token row must be exactly one (8, 128) slab"
    n = bsz * t
    n_exp = router_w.shape[-1]
    assert (n * TOP_K) % TM == 0 and (n_exp * TM) % PAD_BLOCK == 0 and n <= RANK_RADIX
    row = lambda a: a.reshape(1, -1)

    pad = jnp.zeros((SUBLANES - (bsz + 1) % SUBLANES, d), F32)
    cc = jnp.concatenate([c, c_ctx[None, :], pad], axis=0)
    mod = _mod_call(cc, w_mod[0], row(b_mod[0]))
    lat = lambda j: mod[:bsz, j * d:(j + 1) * d].reshape(bsz, 1, d)
    sh1, sc1, g1, sh2, sc2, g2 = (lat(j) for j in range(6))
    csh1, csc1 = mod[bsz:bsz + 1, 0:d], mod[bsz:bsz + 1, d:2 * d]

    w_in_b = w_in[0].astype(BF16)
    wgf = _gate_weights(lru_wa[0, 0], lru_wx[0, 0])
    wgb = _gate_weights(lru_wa[0, 1], lru_wx[0, 1])
    ba, bx, lam = lru_ba[0], lru_bx[0], lru_lambda[0]
    cw, cb = conv_w[0], row(conv_b[0])
    npre_mix = row(norm_pre_mix[0])

    h0f, h0b = _ctx_call(ctx, npre_mix, csh1, csc1, w_in_b, cw, cb, wgf, wgb, ba, bx, lam)

    hf, xc, gg, xp, mr, mp = _mix_fwd_call(x, npre_mix, sh1, sc1, w_in_b, cw, cb, wgf, ba, bx, lam, h0f)

    pm, pc = _pool_consts()
    xmid, v, code_o, wts_o, cnt = _mix_bwd_call(
        hf, xc, gg, xp, mr, mp, x, wgb, ba, bx, lam, h0b,
        pm, pc, pool_w[0].astype(BF16), row(pool_scale[0]),
        w_rnn_proj[0].astype(BF16), w_pool_proj[0].astype(BF16), w_out[0].astype(BF16),
        row(norm_post_mix[0]), row(norm_pre_ffn[0]), g1, sh2, sc2, router_w[0], row(router_b[0]))

    code = code_o.reshape(n, LANES)[:, :TOP_K]
    idx, rank = code // RANK_RADIX, code % RANK_RADIX
    counts = cnt[0].astype(jnp.int32)
    tiles_e = (counts + TM - 1) // TM
    tile_end = jnp.cumsum(tiles_e)
    offs = (tile_end - tiles_e) * TM
    e_ids = jnp.arange(n_exp, dtype=jnp.int32)
    off_of = jnp.sum(jnp.where(idx[..., None] == e_ids, offs, 0), axis=-1)
    pos_flat = (off_of + rank).reshape(n * TOP_K)
    n_tiles = (n * TOP_K) // TM + n_exp
    n_valid = tile_end[-1]
    jj = jnp.arange(n_tiles, dtype=jnp.int32)
    blk = jnp.minimum(jj, n_valid - 1)
    te = jnp.minimum(_count_le(tile_end, blk), n_exp - 1)

    pad_e = tiles_e * TM - counts
    pad_end = jnp.cumsum(pad_e)
    qq = jnp.arange(n_exp * TM, dtype=jnp.int32)
    e_q = jnp.minimum(_count_le(pad_end, qq), n_exp - 1)
    slot_in = (offs + counts)[e_q] + qq - (pad_end - pad_e)[e_q]
    slot_tail = n_valid * TM + qq - pad_end[-1]
    pad_slots = jnp.where(qq < pad_end[-1], slot_in, slot_tail).astype(jnp.int32)

    xs = _dispatch_call(pos_flat, pad_slots, v.reshape((n,) + v.shape[2:]), n_tiles * TM)
    has = tiles_e > 0
    w_slot = ((jnp.cumsum(has.astype(jnp.int32)) - has.astype(jnp.int32)) % 2).astype(jnp.int32)
    later = (e_ids[None, :] > e_ids[:, None]) & has[None, :]
    w_next = jnp.min(jnp.where(later, e_ids[None, :], n_exp), axis=1).astype(jnp.int32)
    ys = _expert_call(te, blk, n_valid.reshape(1), w_slot, w_next, xs, exp_w1[0],
                      exp_b1[0].reshape(n_exp, 1, -1), exp_w2[0], exp_b2[0].reshape(n_exp, 1, -1))
    out = _combine_call(pos_flat, wts_o.reshape(n, LANES), xmid.reshape(n, d), g2, row(norm_post_ffn[0]), ys, t)
    return out.reshape(bsz, t, d)
```

```python
import functools

import numpy as np
import jax
import jax.numpy as jnp
from jax import lax
from jax.experimental import pallas as pl
from jax.experimental.pallas import tpu as pltpu

F32 = jnp.float32
BF16 = jnp.bfloat16

RNN_HEADS = 4
CONV_W = 4
LRU_C = 8.0
POOL_WINDOWS = (2, 4, 8, 16)
GRID_W = 64
TOP_K = 4
SWIGLU_LIMIT = 7.0
SWIGLU_ALPHA = 1.702
EPS = 1e-6
SQRT_FLOOR = 1e-30

SUBLANES = 8
LANES = 128
VMEM_LIMIT = 56 * 1024 * 1024

TT = 256
POOL_CHUNK = 256
TM = 256
TD = 128
PAD_BLOCK = 128
DISPATCH_BUFS = 3
ISSUE_UNROLL = 8
DMA_PRIORITIES = 2
RANK_RADIX = 1 << 16


def _sigmoid(x):
    return 1.0 / (1.0 + jnp.exp(-x))


def _softplus(z):
    return jnp.maximum(z, 0.0) + jnp.log1p(jnp.exp(-jnp.abs(z)))


def _sqrt_nonneg(x):
    return x * lax.rsqrt(jnp.maximum(x, SQRT_FLOOR))


def _rmsnorm(x, g):
    ms = jnp.mean(x * x, axis=-1, keepdims=True)
    return (x * lax.rsqrt(ms + EPS)) * g


def _modulate(u, shift, scale):
    return u * (1.0 + scale) + shift


def _dot(a, b):
    return jnp.dot(a, b, preferred_element_type=F32)


def _split(x):
    hi = x.astype(BF16)
    return hi, (x - hi.astype(F32)).astype(BF16)


def _dot3(a, b):
    ah, al = _split(a)
    bh, bl = _split(b)
    return _dot(ah, bh) + (_dot(al, bh) + _dot(ah, bl))


def _conv(prev8, xr, next8, cw, cb):
    t = xr.shape[0]
    ext = jnp.concatenate([prev8, xr, next8], axis=0)
    acc = cb + cw[0:1, :] * ext[SUBLANES - 2:SUBLANES - 2 + t, :]
    for k in range(1, CONV_W):
        off = SUBLANES - 2 + k
        acc = acc + cw[k:k + 1, :] * ext[off:off + t, :]
    return acc


def _lru_coeffs(xc, wg_ref, ba, bx, sp):
    hd = xc.shape[1] // RNN_HEADS
    xcb = xc.astype(BF16)
    a_parts, b_parts = [], []
    for h in range(RNN_HEADS):
        cols = slice(h * hd, (h + 1) * hd)
        z = _dot(xcb[:, cols], wg_ref[h])
        r = _sigmoid(z[:, :hd] + ba[:, cols])
        i = _sigmoid(z[:, hd:] + bx[:, cols])
        a = jnp.exp((-LRU_C) * r * sp[:, cols])
        b = _sqrt_nonneg(1.0 - a * a) * (i * xc[:, cols])
        a_parts.append(a)
        b_parts.append(b)
    return jnp.concatenate(a_parts, axis=1), jnp.concatenate(b_parts, axis=1)


def _scan_tile(a, b, h_in, reverse, store):
    t, c = a.shape
    g = t // SUBLANES
    a3 = a.reshape(g, SUBLANES, c)
    b3 = b.reshape(g, SUBLANES, c)
    row = lax.broadcasted_iota(jnp.int32, (g, SUBLANES, c), 1)
    for s in (1, 2, 4):
        if reverse:
            shift, m = SUBLANES - s, row < SUBLANES - s
        else:
            shift, m = s, row >= s
        ra = pltpu.roll(a3, shift, axis=1)
        rb = pltpu.roll(b3, shift, axis=1)
        b3 = a3 * jnp.where(m, rb, 0.0) + b3
        a3 = a3 * jnp.where(m, ra, 1.0)
    h = h_in
    order = range(g - 1, -1, -1) if reverse else range(g)
    for gi in order:
        hg = a3[gi] * h + b3[gi]
        store(gi, hg)
        h = hg[0:1, :] if reverse else hg[SUBLANES - 1:SUBLANES, :]
    return h


def _mod_body(c_ref, w_ref, b_ref, o_ref):
    c = c_ref[...]
    s = c * _sigmoid(c)
    o_ref[...] = _dot3(s, w_ref[...]) + b_ref[...]


def _mod_call(cc, w_mod, b_mod):
    d = cc.shape[1]
    n = w_mod.shape[1] // d
    return pl.pallas_call(
        _mod_body,
        out_shape=jax.ShapeDtypeStruct((cc.shape[0], n * d), F32),
        grid=(n,),
        in_specs=[pl.BlockSpec(cc.shape, lambda j: (0, 0)),
                  pl.BlockSpec((d, d), lambda j: (0, j)),
                  pl.BlockSpec((1, d), lambda j: (0, j))],
        out_specs=pl.BlockSpec((cc.shape[0], d), lambda j: (0, j)),
        name="mod",
    )(cc, w_mod, b_mod)


def _ctx_body(ctx_ref, g_ref, sh_ref, sc_ref, w_ref, cw_ref, cb_ref, wgf_ref, wgb_ref,
              ba_ref, bx_ref, lam_ref, hf_ref, hb_ref):
    d = ctx_ref.shape[2]
    u = _modulate(_rmsnorm(ctx_ref[0], g_ref[...]), sh_ref[...], sc_ref[...])
    xr = _dot(u.astype(BF16), w_ref[...])
    z8 = jnp.zeros((SUBLANES, d), F32)
    xc = _conv(z8, xr, z8, cw_ref[...], cb_ref[...])
    h0 = jnp.zeros((1, d), F32)
    for di, (wg_ref, out_ref) in enumerate(((wgf_ref, hf_ref), (wgb_ref, hb_ref))):
        sp = _softplus(-lam_ref[di:di + 1, :])
        a, b = _lru_coeffs(xc, wg_ref, ba_ref[di:di + 1, :], bx_ref[di:di + 1, :], sp)
        out_ref[0] = _scan_tile(a, b, h0, di == 1, lambda gi, hg: None)


def _ctx_call(ctx, g, sh, sc, w_in_b, cw, cb, wgf, wgb, ba, bx, lam):
    bsz, tc, d = ctx.shape
    full = lambda shp: pl.BlockSpec(shp, lambda b: (0,) * len(shp))
    return pl.pallas_call(
        _ctx_body,
        out_shape=(jax.ShapeDtypeStruct((bsz, 1, d), F32),) * 2,
        grid=(bsz,),
        in_specs=[pl.BlockSpec((1, tc, d), lambda b: (b, 0, 0)),
                  full((1, d)), full((1, d)), full((1, d)),
                  pl.BlockSpec((d, d), lambda b: (0, 0)),
                  full(cw.shape), full((1, d)), full(wgf.shape), full(wgb.shape),
                  full(ba.shape), full(bx.shape), full(lam.shape)],
        out_specs=(pl.BlockSpec((1, 1, d), lambda b: (b, 0, 0)),) * 2,
        compiler_params=pltpu.CompilerParams(vmem_limit_bytes=VMEM_LIMIT),
        name="ctx",
    )(ctx, g, sh, sc, w_in_b, cw, cb, wgf, wgb, ba, bx, lam)


def _mix_fwd_body(x_ref, xn_ref, g_ref, sh_ref, sc_ref, win_ref, cw_ref, cb_ref, wgf_ref,
                  ba_ref, bx_ref, lam_ref, h0_ref,
                  hf_ref, xc_ref, gg_ref, xp_ref, mr_ref, mp_ref, tail_s, h_s):
    i = pl.program_id(1)
    last = pl.num_programs(1) - 1
    tt, d = x_ref.shape[1], x_ref.shape[2]

    @pl.when(i == 0)
    def _():
        tail_s[...] = jnp.zeros_like(tail_s)
        h_s[...] = h0_ref[0]

    g, sh, sc = g_ref[...], sh_ref[0], sc_ref[0]
    ub = _modulate(_rmsnorm(x_ref[0], g), sh, sc).astype(BF16)
    unb = _modulate(_rmsnorm(xn_ref[0], g), sh, sc).astype(BF16)
    w_rnn = win_ref[:, 0:d]
    xr = _dot(ub, w_rnn)
    xrn = jnp.where(i == last, 0.0, _dot(unb, w_rnn))
    xc = _conv(tail_s[...], xr, xrn, cw_ref[...], cb_ref[...])
    tail_s[...] = xr[tt - SUBLANES:tt, :]
    xc_ref[0] = xc.astype(BF16)
    for j, ref in enumerate((gg_ref, xp_ref, mr_ref, mp_ref)):
        ref[0] = _dot(ub, win_ref[:, (j + 1) * d:(j + 2) * d]).astype(BF16)

    sp = _softplus(-lam_ref[0:1, :])
    a, b = _lru_coeffs(xc, wgf_ref, ba_ref[0:1, :], bx_ref[0:1, :], sp)

    def store(gi, hg):
        hf_ref[0, gi * SUBLANES:(gi + 1) * SUBLANES, :] = hg

    h_s[...] = _scan_tile(a, b, h_s[...], False, store)


def _mix_fwd_call(x, g, sh, sc, w_in_b, cw, cb, wgf, ba, bx, lam, h0f):
    bsz, t, d = x.shape
    nt = t // TT
    nblk8 = t // SUBLANES
    per_b = pl.BlockSpec((1, 1, d), lambda b, i: (b, 0, 0))
    full = lambda shp: pl.BlockSpec(shp, lambda b, i: (0,) * len(shp))
    tile = pl.BlockSpec((1, TT, d), lambda b, i: (b, i, 0))
    return pl.pallas_call(
        _mix_fwd_body,
        out_shape=(jax.ShapeDtypeStruct((bsz, t, d), F32),) + (jax.ShapeDtypeStruct((bsz, t, d), BF16),) * 5,
        grid=(bsz, nt),
        in_specs=[tile,
                  pl.BlockSpec((1, SUBLANES, d),
                               lambda b, i: (b, jnp.minimum((i + 1) * (TT // SUBLANES), nblk8 - 1), 0)),
                  full((1, d)), per_b, per_b,
                  full(w_in_b.shape), full(cw.shape), full((1, d)), full(wgf.shape),
                  full(ba.shape), full(bx.shape), full(lam.shape), per_b],
        out_specs=(tile,) * 6,
        scratch_shapes=[pltpu.VMEM((SUBLANES, d), F32), pltpu.VMEM((1, d), F32)],
        compiler_params=pltpu.CompilerParams(
            dimension_semantics=("arbitrary", "arbitrary"), vmem_limit_bytes=VMEM_LIMIT),
        name="mix_fwd",
    )(x, x, g, sh, sc, w_in_b, cw, cb, wgf, ba, bx, lam, h0f)


def _pool_consts():
    p = np.arange(POOL_CHUNK)
    pos, line = p % GRID_W, p // GRID_W
    mats, cnts = [], []
    for w in POOL_WINDOWS:
        lo = np.clip(pos - w // 2, 0, GRID_W)
        hi = np.clip(pos + w - w // 2, 0, GRID_W)
        m = (line[:, None] == line[None, :]) & (pos[None, :] >= lo[:, None]) & (pos[None, :] < hi[:, None])
        mats.append(m.astype(np.float32))
        cnts.append((hi - lo).astype(np.float32)[:, None])
    return jnp.asarray(np.stack(mats), BF16), jnp.asarray(np.stack(cnts), F32)


def _mix_bwd_body(hf_ref, xc_ref, gg_ref, xp_ref, mr_ref, mp_ref, x_ref,
                  wgb_ref, ba_ref, bx_ref, lam_ref, h0_ref,
                  pm_ref, pc_ref, pw_ref, ps_ref, wr_ref, wp_ref, wo_ref,
                  npost_ref, npre_ref, g1_ref, sh2_ref, sc2_ref, rw_ref, rb_ref,
                  xmid_ref, v_ref, code_ref, wts_ref, cnt_ref,
                  hb_s, h_s, cnt_s):
    b_id, i = pl.program_id(0), pl.program_id(1)
    tt, d = x_ref.shape[1], x_ref.shape[2]
    n_exp = rw_ref.shape[1]

    @pl.when(i == 0)
    def _():
        h_s[...] = h0_ref[0]

    @pl.when((i == 0) & (b_id == 0))
    def _():
        cnt_s[...] = jnp.zeros_like(cnt_s)

    xc = xc_ref[0].astype(F32)
    sp = _softplus(-lam_ref[1:2, :])
    a, b = _lru_coeffs(xc, wgb_ref, ba_ref[1:2, :], bx_ref[1:2, :], sp)

    def store(gi, hg):
        hb_s[gi * SUBLANES:(gi + 1) * SUBLANES, :] = hg

    h_s[...] = _scan_tile(a, b, h_s[...], True, store)

    gg = gg_ref[0].astype(F32)
    gelu = gg * _sigmoid(gg * (1.5957691216057308 + 0.07135481627260025 * (gg * gg)))
    y_rnn = (hf_ref[0] + hb_s[...]) * gelu

    xpb = xp_ref[0]
    grp = d // len(POOL_WINDOWS)
    y_parts = []
    for gi in range(len(POOL_WINDOWS)):
        cols = slice(gi * grp, (gi + 1) * grp)
        rows = []
        for c0 in range(0, tt, POOL_CHUNK):
            xg = xpb[c0:c0 + POOL_CHUNK, cols]
            mean = _dot(pm_ref[gi], xg) / pc_ref[gi]
            rows.append((mean - xg.astype(F32)).astype(BF16))
        dg = rows[0] if len(rows) == 1 else jnp.concatenate(rows, axis=0)
        y_parts.append(_dot(dg, pw_ref[gi]) * ps_ref[:, cols])
    y_pool = jnp.concatenate(y_parts, axis=1)

    merged = (_sigmoid(mr_ref[0].astype(F32)) * _dot(y_rnn.astype(BF16), wr_ref[...])
              + _sigmoid(mp_ref[0].astype(F32)) * _dot(y_pool.astype(BF16), wp_ref[...]))
    m_lat = _dot(merged.astype(BF16), wo_ref[...])
    x_mid = x_ref[0] + g1_ref[0] * _rmsnorm(m_lat, npost_ref[...])
    xmid_ref[0] = x_mid
    v = _modulate(_rmsnorm(x_mid, npre_ref[...]), sh2_ref[0], sc2_ref[0])
    v_ref[0] = v.reshape(tt, SUBLANES, LANES)

    logits = _dot3(v, rw_ref[...]) + rb_ref[...]
    lane = lax.broadcasted_iota(jnp.int32, (tt, n_exp), 1).astype(F32)
    work = logits
    vals, idxs, sels = [], [], []
    for _ in range(TOP_K):
        m = jnp.max(work, axis=-1, keepdims=True)
        idx = jnp.min(jnp.where(work == m, lane, float(n_exp)), axis=-1, keepdims=True)
        sel = lane == idx
        vals.append(m)
        idxs.append(idx)
        sels.append(sel)
        work = jnp.where(sel, -jnp.inf, work)
    exps = [jnp.exp(vk - vals[0]) for vk in vals]
    den = exps[0] + exps[1] + exps[2] + exps[3]
    anyf = jnp.zeros((tt, n_exp), F32)
    for sel in sels:
        anyf = anyf + sel.astype(F32)
    r_i = lax.broadcasted_iota(jnp.int32, (tt, tt), 0)
    c_i = lax.broadcasted_iota(jnp.int32, (tt, tt), 1)
    lower = (c_i < r_i).astype(BF16)
    before = _dot(lower, anyf.astype(BF16)) + cnt_s[...]
    cnt_new = cnt_s[...] + jnp.sum(anyf, axis=0, keepdims=True)
    cnt_s[...] = cnt_new
    cnt_ref[...] = cnt_new

    lane_o = lax.broadcasted_iota(jnp.int32, (tt, LANES), 1)
    code_o = jnp.zeros((tt, LANES), jnp.int32)
    wts_o = jnp.zeros((tt, LANES), F32)
    for k in range(TOP_K):
        rk = jnp.sum(jnp.where(sels[k], before, 0.0), axis=-1, keepdims=True)
        code = idxs[k].astype(jnp.int32) * RANK_RADIX + rk.astype(jnp.int32)
        code_o = jnp.where(lane_o == k, code, code_o)
        wts_o = jnp.where(lane_o == k, exps[k] / den, wts_o)
    code_ref[0] = code_o
    wts_ref[0] = wts_o


def _mix_bwd_call(hf, xc, gg, xp, mr, mp, x, wgb, ba, bx, lam, h0b, pm, pc, pw, ps, wr, wp, wo,
                  npost, npre, g1, sh2, sc2, rw, rb):
    bsz, t, d = x.shape
    nt = t // TT
    n_exp = rw.shape[1]
    rev = lambda b, i: (b, nt - 1 - i, 0)
    tile = pl.BlockSpec((1, TT, d), rev)
    small = pl.BlockSpec((1, TT, LANES), rev)
    per_b = pl.BlockSpec((1, 1, d), lambda b, i: (b, 0, 0))
    full = lambda shp: pl.BlockSpec(shp, lambda b, i: (0,) * len(shp))
    return pl.pallas_call(
        _mix_bwd_body,
        out_shape=(jax.ShapeDtypeStruct((bsz, t, d), F32),
                   jax.ShapeDtypeStruct((bsz, t, SUBLANES, LANES), F32),
                   jax.ShapeDtypeStruct((bsz, t, LANES), jnp.int32),
                   jax.ShapeDtypeStruct((bsz, t, LANES), F32),
                   jax.ShapeDtypeStruct((1, n_exp), F32)),
        grid=(bsz, nt),
        in_specs=[tile] * 7 + [full(wgb.shape), full(ba.shape), full(bx.shape), full(lam.shape), per_b,
                               full(pm.shape), full(pc.shape), full(pw.shape), full((1, d)),
                               full((d, d)), full((d, d)), full((d, d)),
                               full((1, d)), full((1, d)), per_b, per_b, per_b,
                               full(rw.shape), full((1, n_exp))],
        out_specs=(tile, pl.BlockSpec((1, TT, SUBLANES, LANES), lambda b, i: (b, nt - 1 - i, 0, 0)),
                   small, small, full((1, n_exp))),
        scratch_shapes=[pltpu.VMEM((TT, d), F32), pltpu.VMEM((1, d), F32), pltpu.VMEM((1, n_exp), F32)],
        compiler_params=pltpu.CompilerParams(
            dimension_semantics=("arbitrary", "arbitrary"), vmem_limit_bytes=VMEM_LIMIT),
        name="mix_bwd",
    )(hf, xc, gg, xp, mr, mp, x, wgb, ba, bx, lam, h0b, pm, pc, pw, ps, wr, wp, wo,
      npost, npre, g1, sh2, sc2, rw, rb)


def _dispatch_body(n_pad_blocks, pos_ref, pad_ref, zero_ref, v_hbm, xs_hbm, vbuf, lsem, ssem):
    i = pl.program_id(0)
    steps = pl.num_programs(0)
    n_pad = pad_ref.shape[0]
    slot = lax.rem(i, DISPATCH_BUFS)

    def load(step):
        s = lax.rem(step, DISPATCH_BUFS)
        return pltpu.make_async_copy(v_hbm.at[pl.ds(step * TD, TD)], vbuf.at[s], lsem.at[s])

    def row_copy(s, src_row, dst_row):
        return pltpu.make_async_copy(vbuf.at[s, src_row], xs_hbm.at[dst_row], ssem.at[s])

    def pad_copy(s, dst_row):
        return pltpu.make_async_copy(zero_ref.at[0], xs_hbm.at[dst_row], ssem.at[s])

    def wait_scatter(step):
        s = lax.rem(step, DISPATCH_BUFS)

        def body(t, carry):
            for _ in range(TOP_K):
                row_copy(s, 0, 0).wait()
            return carry

        lax.fori_loop(0, TD, body, 0, unroll=ISSUE_UNROLL)

        @pl.when(step < n_pad_blocks)
        def _():
            def body_pad(q, carry):
                pad_copy(s, 0).wait()
                return carry

            lax.fori_loop(0, n_pad, body_pad, 0, unroll=ISSUE_UNROLL)

    @pl.when(i == 0)
    def _():
        load(0).start()

    @pl.when(i >= 2)
    def _():
        wait_scatter(i - 2)

    @pl.when(i + 1 < steps)
    def _():
        load(i + 1).start()

    load(i).wait()

    def issue(t, carry):
        for k in range(TOP_K):
            row_copy(slot, t, pos_ref[t * TOP_K + k]).start(priority=k % DMA_PRIORITIES)
        return carry

    lax.fori_loop(0, TD, issue, 0, unroll=ISSUE_UNROLL)

    @pl.when(i < n_pad_blocks)
    def _():
        def issue_pad(q, carry):
            pad_copy(slot, pad_ref[q]).start()
            return carry

        lax.fori_loop(0, n_pad, issue_pad, 0, unroll=ISSUE_UNROLL)

    @pl.when(i == steps - 1)
    def _():
        @pl.when(i >= 1)
        def _():
            wait_scatter(i - 1)

        wait_scatter(i)


def _dispatch_call(pos_flat, pad_slots, v3, n_slots):
    n = v3.shape[0]
    slab = v3.shape[1:]
    steps = n // TD
    n_pad_blocks = pad_slots.shape[0] // PAD_BLOCK
    assert n_pad_blocks <= steps
    return pl.pallas_call(
        functools.partial(_dispatch_body, n_pad_blocks),
        out_shape=jax.ShapeDtypeStruct((n_slots,) + slab, F32),
        grid=(steps,),
        in_specs=[pl.BlockSpec((TD * TOP_K,), lambda i: (i,), memory_space=pltpu.SMEM),
                  pl.BlockSpec((PAD_BLOCK,), lambda i: (jnp.minimum(i, n_pad_blocks - 1),),
                               memory_space=pltpu.SMEM),
                  pl.BlockSpec((1,) + slab, lambda i: (0, 0, 0)),
                  pl.BlockSpec(memory_space=pl.ANY)],
        out_specs=pl.BlockSpec(memory_space=pl.ANY),
        scratch_shapes=[pltpu.VMEM((DISPATCH_BUFS, TD) + slab, F32),
                        pltpu.SemaphoreType.DMA((DISPATCH_BUFS,)), pltpu.SemaphoreType.DMA((DISPATCH_BUFS,))],
        compiler_params=pltpu.CompilerParams(dimension_semantics=("arbitrary",)),
        name="dispatch",
    )(pos_flat, pad_slots, jnp.zeros((1,) + slab, F32), v3)


def _expert_body(te_ref, blk_ref, nvalid_ref, slot_ref, nxt_ref,
                 xs_ref, w1_hbm, b1_ref, w2_hbm, b2_ref, y_ref, w1f_s, w2f_s, w1b_s, w2b_s, sem):
    j = pl.program_id(0)
    n_exp = w1_hbm.shape[0]
    ff = w2_hbm.shape[1]
    e = te_ref[j]
    prev = te_ref[jnp.maximum(j - 1, 0)]

    def weight_copies(expert, slot):
        return (pltpu.make_async_copy(w1_hbm.at[expert], w1f_s.at[slot], sem.at[slot, 0]),
                pltpu.make_async_copy(w2_hbm.at[expert], w2f_s.at[slot], sem.at[slot, 1]))

    @pl.when(j == 0)
    def _():
        for c in weight_copies(e, slot_ref[e]):
            c.start()

    @pl.when((j == 0) | (e != prev))
    def _():
        slot = slot_ref[e]
        for c in weight_copies(e, slot):
            c.wait()
        w1b_s[...] = w1f_s[slot].astype(BF16)
        w2b_s[...] = w2f_s[slot].astype(BF16)

        @pl.when(nxt_ref[e] < n_exp)
        def _():
            for c in weight_copies(nxt_ref[e], 1 - slot):
                c.start()

    @pl.when(j < nvalid_ref[0])
    def _():
        tm = xs_ref.shape[0]
        x = xs_ref[...].reshape(tm, w1_hbm.shape[1])
        z = _dot(x.astype(BF16), w1b_s[...]) + b1_ref[0]
        glu = jnp.minimum(z[:, :ff], SWIGLU_LIMIT)
        lin = jnp.clip(z[:, ff:], -SWIGLU_LIMIT, SWIGLU_LIMIT)
        act = glu * _sigmoid(SWIGLU_ALPHA * glu) * (lin + 1.0)
        y = _dot(act.astype(BF16), w2b_s[...]) + b2_ref[0]
        y_ref[...] = y.reshape(y_ref.shape)

    @pl.when(j >= nvalid_ref[0])
    def _():
        y_ref[...] = jnp.zeros_like(y_ref)


def _expert_call(te, blk, nvalid, slot, nxt, xs, w1, b1, w2, b2):
    n_slots = xs.shape[0]
    slab = xs.shape[1:]
    n_exp, d, ff2 = w1.shape
    ff = w2.shape[1]
    grid_spec = pltpu.PrefetchScalarGridSpec(
        num_scalar_prefetch=5,
        grid=(n_slots // TM,),
        in_specs=[pl.BlockSpec((TM,) + slab, lambda j, te, blk, *_: (blk[j], 0, 0)),
                  pl.BlockSpec(memory_space=pl.ANY),
                  pl.BlockSpec((1, 1, ff2), lambda j, te, *_: (te[j], 0, 0)),
                  pl.BlockSpec(memory_space=pl.ANY),
                  pl.BlockSpec((1, 1, d), lambda j, te, *_: (te[j], 0, 0))],
        out_specs=pl.BlockSpec((TM,) + slab, lambda j, *_: (j, 0, 0)),
        scratch_shapes=[pltpu.VMEM((2, d, ff2), F32), pltpu.VMEM((2, ff, d), F32),
                        pltpu.VMEM((d, ff2), BF16), pltpu.VMEM((ff, d), BF16),
                        pltpu.SemaphoreType.DMA((2, 2))],
    )
    return pl.pallas_call(
        _expert_body,
        out_shape=jax.ShapeDtypeStruct((n_slots,) + slab, F32),
        grid_spec=grid_spec,
        compiler_params=pltpu.CompilerParams(
            dimension_semantics=("arbitrary",), vmem_limit_bytes=VMEM_LIMIT),
        name="experts",
    )(te, blk, nvalid, slot, nxt, xs, w1, b1, w2, b2)


def _combine_body(pos_ref, posn_ref, wts_ref, xmid_ref, g2_ref, npost_ref, y_hbm, o_ref, buf, sem):
    i = pl.program_id(0)
    cur = lax.rem(i, 2)

    def row_copy(p_ref, slot, t, k):
        return pltpu.make_async_copy(y_hbm.at[p_ref[t * TOP_K + k]], buf.at[slot, k, t], sem.at[slot])

    def issue(p_ref, slot):
        def body(t, carry):
            for k in range(TOP_K):
                row_copy(p_ref, slot, t, k).start(priority=k % DMA_PRIORITIES)
            return carry

        lax.fori_loop(0, TD, body, 0, unroll=ISSUE_UNROLL)

    @pl.when(i == 0)
    def _():
        issue(pos_ref, 0)

    @pl.when(i + 1 < pl.num_programs(0))
    def _():
        issue(posn_ref, 1 - cur)

    def drain(t, carry):
        for k in range(TOP_K):
            row_copy(pos_ref, cur, t, k).wait()
        return carry

    lax.fori_loop(0, TD, drain, 0, unroll=ISSUE_UNROLL)

    w = wts_ref[...]
    f = w[:, 0:1] * buf[cur, 0].reshape(xmid_ref.shape)
    for k in range(1, TOP_K):
        f = f + w[:, k:k + 1] * buf[cur, k].reshape(xmid_ref.shape)
    o_ref[...] = xmid_ref[...] + g2_ref[0] * _rmsnorm(f, npost_ref[...])


def _combine_call(pos_flat, wts, xmid2, g2, npost, ys, t_per_batch):
    n, d = xmid2.shape
    steps = n // TD
    pos_spec = lambda f: pl.BlockSpec((TD * TOP_K,), f, memory_space=pltpu.SMEM)
    return pl.pallas_call(
        _combine_body,
        out_shape=jax.ShapeDtypeStruct((n, d), F32),
        grid=(steps,),
        in_specs=[pos_spec(lambda i: (i,)),
                  pos_spec(lambda i: (jnp.minimum(i + 1, steps - 1),)),
                  pl.BlockSpec((TD, LANES), lambda i: (i, 0)),
                  pl.BlockSpec((TD, d), lambda i: (i, 0)),
                  pl.BlockSpec((1, 1, d), lambda i: ((i * TD) // t_per_batch, 0, 0)),
                  pl.BlockSpec((1, d), lambda i: (0, 0)),
                  pl.BlockSpec(memory_space=pl.ANY)],
        out_specs=pl.BlockSpec((TD, d), lambda i: (i, 0)),
        scratch_shapes=[pltpu.VMEM((2, TOP_K, TD) + ys.shape[1:], F32), pltpu.SemaphoreType.DMA((2,))],
        compiler_params=pltpu.CompilerParams(dimension_semantics=("arbitrary",)),
        name="combine",
    )(pos_flat, pos_flat, wts, xmid2, g2, npost, ys)


def _count_le(ends, q):
    return jnp.sum((ends[None, :] <= q[:, None]).astype(jnp.int32), axis=1)


def _gate_weights(wa, wx):
    return jnp.concatenate([wa, wx], axis=-1).astype(BF16)


def kernel(x, c, ctx, c_ctx, w_mod, b_mod, norm_pre_mix, norm_post_mix, norm_pre_ffn, norm_post_ffn, w_in, conv_w, conv_b, lru_wa, lru_ba, lru_wx, lru_bx, lru_lambda, pool_w, pool_scale, w_rnn_proj, w_pool_proj, w_out, router_w, router_b, exp_w1, exp_b1, exp_w2, exp_b2):
    bsz, t, d = x.shape
    assert w_mod.shape[0] == 1, "single-layer block"
    assert t % TT == 0 and TT % POOL_CHUNK == 0 and POOL_CHUNK % GRID_W == 0 and t % TD == 0
    assert d == SUBLANES * LANES, "a token row must be exactly one (8, 128) slab"
    n = bsz * t
    n_exp = router_w.shape[-1]
    assert (n * TOP_K) % TM == 0 and (n_exp * TM) % PAD_BLOCK == 0 and n <= RANK_RADIX
    row = lambda a: a.reshape(1, -1)

    pad = jnp.zeros((SUBLANES - (bsz + 1) % SUBLANES, d), F32)
    cc = jnp.concatenate([c, c_ctx[None, :], pad], axis=0)
    mod = _mod_call(cc, w_mod[0], row(b_mod[0]))
    lat = lambda j: mod[:bsz, j * d:(j + 1) * d].reshape(bsz, 1, d)
    sh1, sc1, g1, sh2, sc2, g2 = (lat(j) for j in range(6))
    csh1, csc1 = mod[bsz:bsz + 1, 0:d], mod[bsz:bsz + 1, d:2 * d]

    w_in_b = w_in[0].astype(BF16)
    wgf = _gate_weights(lru_wa[0, 0], lru_wx[0, 0])
    wgb = _gate_weights(lru_wa[0, 1], lru_wx[0, 1])
    ba, bx, lam = lru_ba[0], lru_bx[0], lru_lambda[0]
    cw, cb = conv_w[0], row(conv_b[0])
    npre_mix = row(norm_pre_mix[0])

    h0f, h0b = _ctx_call(ctx, npre_mix, csh1, csc1, w_in_b, cw, cb, wgf, wgb, ba, bx, lam)

    hf, xc, gg, xp, mr, mp = _mix_fwd_call(x, npre_mix, sh1, sc1, w_in_b, cw, cb, wgf, ba, bx, lam, h0f)

    pm, pc = _pool_consts()
    xmid, v, code_o, wts_o, cnt = _mix_bwd_call(
        hf, xc, gg, xp, mr, mp, x, wgb, ba, bx, lam, h0b,
        pm, pc, pool_w[0].astype(BF16), row(pool_scale[0]),
        w_rnn_proj[0].astype(BF16), w_pool_proj[0].astype(BF16), w_out[0].astype(BF16),
        row(norm_post_mix[0]), row(norm_pre_ffn[0]), g1, sh2, sc2, router_w[0], row(router_b[0]))

    code = code_o.reshape(n, LANES)[:, :TOP_K].T
    idx, rank = code // RANK_RADIX, code % RANK_RADIX
    counts = cnt[0].astype(jnp.int32)
    tiles_e = (counts + TM - 1) // TM
    tile_end = jnp.cumsum(tiles_e)
    offs = (tile_end - tiles_e) * TM
    e_ids = jnp.arange(n_exp, dtype=jnp.int32)
    off_of = jnp.sum(jnp.where(idx[None] == e_ids[:, None, None], offs[:, None, None], 0), axis=0)
    pos_flat = (off_of + rank).T.reshape(n * TOP_K)
    n_tiles = (n * TOP_K) // TM + n_exp
    n_valid = tile_end[-1]
    jj = jnp.arange(n_tiles, dtype=jnp.int32)
    blk = jnp.minimum(jj, n_valid - 1)
    te = jnp.minimum(_count_le(tile_end, blk), n_exp - 1)

    pad_e = tiles_e * TM - counts
    pad_end = jnp.cumsum(pad_e)
    qq = jnp.arange(n_exp * TM, dtype=jnp.int32)
    e_q = jnp.minimum(_count_le(pad_end, qq), n_exp - 1)
    slot_in = (offs + counts)[e_q] + qq - (pad_end - pad_e)[e_q]
    slot_tail = n_valid * TM + qq - pad_end[-1]
    pad_slots = jnp.where(qq < pad_end[-1], slot_in, slot_tail).astype(jnp.int32)

    xs = _dispatch_call(pos_flat, pad_slots, v.reshape((n,) + v.shape[2:]), n_tiles * TM)
    has = tiles_e > 0
    w_slot = ((jnp.cumsum(has.astype(jnp.int32)) - has.astype(jnp.int32)) % 2).astype(jnp.int32)
    later = (e_ids[None, :] > e_ids[:, None]) & has[None, :]
    w_next = jnp.min(jnp.where(later, e_ids[None, :], n_exp), axis=1).astype(jnp.int32)
    ys = _expert_call(te, blk, n_valid.reshape(1), w_slot, w_next, xs, exp_w1[0],
                      exp_b1[0].reshape(n_exp, 1, -1), exp_w2[0], exp_b2[0].reshape(n_exp, 1, -1))
    out = _combine_call(pos_flat, wts_o.reshape(n, LANES), xmid.reshape(n, d), g2, row(norm_post_ffn[0]), ys, t)
    return out.reshape(bsz, t, d)
```

```python
import functools

import numpy as np
import jax
import jax.numpy as jnp
from jax import lax
from jax.experimental import pallas as pl
from jax.experimental.pallas import tpu as pltpu

F32 = jnp.float32
BF16 = jnp.bfloat16

RNN_HEADS = 4
CONV_W = 4
LRU_C = 8.0
POOL_WINDOWS = (2, 4, 8, 16)
GRID_W = 64
TOP_K = 4
SWIGLU_LIMIT = 7.0
SWIGLU_ALPHA = 1.702
EPS = 1e-6
SQRT_FLOOR = 1e-30

SUBLANES = 8
LANES = 128
VMEM_LIMIT = 56 * 1024 * 1024

TT = 256
POOL_CHUNK = 256
TM = 256
TD = 128
PAD_BLOCK = 128
DISPATCH_BUFS = 3
ISSUE_UNROLL = 8
DMA_PRIORITIES = 2
RANK_RADIX = 1 << 16


def _sigmoid(x):
    return 1.0 / (1.0 + jnp.exp(-x))


def _softplus(z):
    return jnp.maximum(z, 0.0) + jnp.log1p(jnp.exp(-jnp.abs(z)))


def _sqrt_nonneg(x):
    return x * lax.rsqrt(jnp.maximum(x, SQRT_FLOOR))


def _rmsnorm(x, g):
    ms = jnp.mean(x * x, axis=-1, keepdims=True)
    return (x * lax.rsqrt(ms + EPS)) * g


def _modulate(u, shift, scale):
    return u * (1.0 + scale) + shift


def _dot(a, b):
    return jnp.dot(a, b, preferred_element_type=F32)


def _split(x):
    hi = x.astype(BF16)
    return hi, (x - hi.astype(F32)).astype(BF16)


def _dot3(a, b):
    ah, al = _split(a)
    bh, bl = _split(b)
    return _dot(ah, bh) + (_dot(al, bh) + _dot(ah, bl))


def _load_cast(w_hbm, dst_s, tmp_s, sem):
    c = tmp_s.shape[1]
    for j in range(w_hbm.shape[1] // c):
        cp = pltpu.make_async_copy(w_hbm.at[:, pl.ds(j * c, c)], tmp_s, sem)
        cp.start()
        cp.wait()
        dst_s[:, j * c:(j + 1) * c] = tmp_s[...].astype(BF16)


def _conv(prev8, xr, next8, cw, cb):
    t = xr.shape[0]
    ext = jnp.concatenate([prev8, xr, next8], axis=0)
    acc = cb + cw[0:1, :] * ext[SUBLANES - 2:SUBLANES - 2 + t, :]
    for k in range(1, CONV_W):
        off = SUBLANES - 2 + k
        acc = acc + cw[k:k + 1, :] * ext[off:off + t, :]
    return acc


def _lru_coeffs(xc, wg_ref, ba, bx, sp):
    hd = xc.shape[1] // RNN_HEADS
    xcb = xc.astype(BF16)
    a_parts, b_parts = [], []
    for h in range(RNN_HEADS):
        cols = slice(h * hd, (h + 1) * hd)
        z = _dot(xcb[:, cols], wg_ref[h])
        r = _sigmoid(z[:, :hd] + ba[:, cols])
        i = _sigmoid(z[:, hd:] + bx[:, cols])
        a = jnp.exp((-LRU_C) * r * sp[:, cols])
        b = _sqrt_nonneg(1.0 - a * a) * (i * xc[:, cols])
        a_parts.append(a)
        b_parts.append(b)
    return jnp.concatenate(a_parts, axis=1), jnp.concatenate(b_parts, axis=1)


def _scan_tile(a, b, h_in, reverse, store):
    t, c = a.shape
    g = t // SUBLANES
    a3 = a.reshape(g, SUBLANES, c)
    b3 = b.reshape(g, SUBLANES, c)
    row = lax.broadcasted_iota(jnp.int32, (g, SUBLANES, c), 1)
    for s in (1, 2, 4):
        if reverse:
            shift, m = SUBLANES - s, row < SUBLANES - s
        else:
            shift, m = s, row >= s
        ra = pltpu.roll(a3, shift, axis=1)
        rb = pltpu.roll(b3, shift, axis=1)
        b3 = a3 * jnp.where(m, rb, 0.0) + b3
        a3 = a3 * jnp.where(m, ra, 1.0)
    h = h_in
    order = range(g - 1, -1, -1) if reverse else range(g)
    for gi in order:
        hg = a3[gi] * h + b3[gi]
        store(gi, hg)
        h = hg[0:1, :] if reverse else hg[SUBLANES - 1:SUBLANES, :]
    return h


def _mod_body(c_ref, w_ref, b_ref, o_ref):
    c = c_ref[...]
    s = c * _sigmoid(c)
    o_ref[...] = _dot3(s, w_ref[...]) + b_ref[...]


def _mod_call(cc, w_mod, b_mod):
    d = cc.shape[1]
    n = w_mod.shape[1] // d
    return pl.pallas_call(
        _mod_body,
        out_shape=jax.ShapeDtypeStruct((cc.shape[0], n * d), F32),
        grid=(n,),
        in_specs=[pl.BlockSpec(cc.shape, lambda j: (0, 0)),
                  pl.BlockSpec((d, d), lambda j: (0, j)),
                  pl.BlockSpec((1, d), lambda j: (0, j))],
        out_specs=pl.BlockSpec((cc.shape[0], d), lambda j: (0, j)),
        name="mod",
    )(cc, w_mod, b_mod)


def _ctx_body(ctx_ref, g_ref, sh_ref, sc_ref, w_ref, cw_ref, cb_ref, wgf_ref, wgb_ref,
              ba_ref, bx_ref, lam_ref, hf_ref, hb_ref):
    d = ctx_ref.shape[2]
    u = _modulate(_rmsnorm(ctx_ref[0], g_ref[...]), sh_ref[...], sc_ref[...])
    xr = _dot(u.astype(BF16), w_ref[...].astype(BF16))
    z8 = jnp.zeros((SUBLANES, d), F32)
    xc = _conv(z8, xr, z8, cw_ref[...], cb_ref[...])
    h0 = jnp.zeros((1, d), F32)
    for di, (wg_ref, out_ref) in enumerate(((wgf_ref, hf_ref), (wgb_ref, hb_ref))):
        sp = _softplus(-lam_ref[di:di + 1, :])
        a, b = _lru_coeffs(xc, wg_ref, ba_ref[di:di + 1, :], bx_ref[di:di + 1, :], sp)
        out_ref[0] = _scan_tile(a, b, h0, di == 1, lambda gi, hg: None)


def _ctx_call(ctx, g, sh, sc, w_in, cw, cb, wgf, wgb, ba, bx, lam):
    bsz, tc, d = ctx.shape
    full = lambda shp: pl.BlockSpec(shp, lambda b: (0,) * len(shp))
    return pl.pallas_call(
        _ctx_body,
        out_shape=(jax.ShapeDtypeStruct((bsz, 1, d), F32),) * 2,
        grid=(bsz,),
        in_specs=[pl.BlockSpec((1, tc, d), lambda b: (b, 0, 0)),
                  full((1, d)), full((1, d)), full((1, d)),
                  pl.BlockSpec((d, d), lambda b: (0, 0)),
                  full(cw.shape), full((1, d)), full(wgf.shape), full(wgb.shape),
                  full(ba.shape), full(bx.shape), full(lam.shape)],
        out_specs=(pl.BlockSpec((1, 1, d), lambda b: (b, 0, 0)),) * 2,
        compiler_params=pltpu.CompilerParams(vmem_limit_bytes=VMEM_LIMIT),
        name="ctx",
    )(ctx, g, sh, sc, w_in, cw, cb, wgf, wgb, ba, bx, lam)


def _mix_fwd_body(x_ref, xn_ref, g_ref, sh_ref, sc_ref, win_ref, cw_ref, cb_ref, wgf_ref,
                  ba_ref, bx_ref, lam_ref, h0_ref,
                  hf_ref, xc_ref, gg_ref, xp_ref, mr_ref, mp_ref, tail_s, h_s, win_s, wtmp_s, wsem):
    i = pl.program_id(1)
    last = pl.num_programs(1) - 1
    tt, d = x_ref.shape[1], x_ref.shape[2]

    @pl.when((i == 0) & (pl.program_id(0) == 0))
    def _():
        _load_cast(win_ref, win_s, wtmp_s, wsem)

    @pl.when(i == 0)
    def _():
        tail_s[...] = jnp.zeros_like(tail_s)
        h_s[...] = h0_ref[0]

    g, sh, sc = g_ref[...], sh_ref[0], sc_ref[0]
    ub = _modulate(_rmsnorm(x_ref[0], g), sh, sc).astype(BF16)
    unb = _modulate(_rmsnorm(xn_ref[0], g), sh, sc).astype(BF16)
    w_rnn = win_s[:, 0:d]
    xr = _dot(ub, w_rnn)
    xrn = jnp.where(i == last, 0.0, _dot(unb, w_rnn))
    xc = _conv(tail_s[...], xr, xrn, cw_ref[...], cb_ref[...])
    tail_s[...] = xr[tt - SUBLANES:tt, :]
    xc_ref[0] = xc.astype(BF16)
    for j, ref in enumerate((gg_ref, xp_ref, mr_ref, mp_ref)):
        ref[0] = _dot(ub, win_s[:, (j + 1) * d:(j + 2) * d]).astype(BF16)

    sp = _softplus(-lam_ref[0:1, :])
    a, b = _lru_coeffs(xc, wgf_ref, ba_ref[0:1, :], bx_ref[0:1, :], sp)

    def store(gi, hg):
        hf_ref[0, gi * SUBLANES:(gi + 1) * SUBLANES, :] = hg

    h_s[...] = _scan_tile(a, b, h_s[...], False, store)


def _mix_fwd_call(x, g, sh, sc, w_in, cw, cb, wgf, ba, bx, lam, h0f):
    bsz, t, d = x.shape
    nt = t // TT
    nblk8 = t // SUBLANES
    per_b = pl.BlockSpec((1, 1, d), lambda b, i: (b, 0, 0))
    full = lambda shp: pl.BlockSpec(shp, lambda b, i: (0,) * len(shp))
    tile = pl.BlockSpec((1, TT, d), lambda b, i: (b, i, 0))
    return pl.pallas_call(
        _mix_fwd_body,
        out_shape=(jax.ShapeDtypeStruct((bsz, t, d), F32),) + (jax.ShapeDtypeStruct((bsz, t, d), BF16),) * 5,
        grid=(bsz, nt),
        in_specs=[tile,
                  pl.BlockSpec((1, SUBLANES, d),
                               lambda b, i: (b, jnp.minimum((i + 1) * (TT // SUBLANES), nblk8 - 1), 0)),
                  full((1, d)), per_b, per_b,
                  pl.BlockSpec(memory_space=pl.ANY), full(cw.shape), full((1, d)), full(wgf.shape),
                  full(ba.shape), full(bx.shape), full(lam.shape), per_b],
        out_specs=(tile,) * 6,
        scratch_shapes=[pltpu.VMEM((SUBLANES, d), F32), pltpu.VMEM((1, d), F32),
                        pltpu.VMEM(w_in.shape, BF16), pltpu.VMEM((d, d), F32), pltpu.SemaphoreType.DMA],
        compiler_params=pltpu.CompilerParams(
            dimension_semantics=("arbitrary", "arbitrary"), vmem_limit_bytes=VMEM_LIMIT),
        name="mix_fwd",
    )(x, x, g, sh, sc, w_in, cw, cb, wgf, ba, bx, lam, h0f)


def _pool_consts():
    p = np.arange(POOL_CHUNK)
    pos, line = p % GRID_W, p // GRID_W
    mats, cnts = [], []
    for w in POOL_WINDOWS:
        lo = np.clip(pos - w // 2, 0, GRID_W)
        hi = np.clip(pos + w - w // 2, 0, GRID_W)
        m = (line[:, None] == line[None, :]) & (pos[None, :] >= lo[:, None]) & (pos[None, :] < hi[:, None])
        mats.append(m.astype(np.float32))
        cnts.append((hi - lo).astype(np.float32)[:, None])
    return jnp.asarray(np.stack(mats), BF16), jnp.asarray(np.stack(cnts), F32)


def _mix_bwd_body(hf_ref, xc_ref, gg_ref, xp_ref, mr_ref, mp_ref, x_ref,
                  wgb_ref, ba_ref, bx_ref, lam_ref, h0_ref,
                  pm_ref, pc_ref, pw_ref, ps_ref, wr_ref, wp_ref, wo_ref,
                  npost_ref, npre_ref, g1_ref, sh2_ref, sc2_ref, rw_ref, rb_ref,
                  xmid_ref, v_ref, code_ref, wts_ref, cnt_ref,
                  hb_s, h_s, cnt_s, w3_s, wtmp_s, wsem):
    b_id, i = pl.program_id(0), pl.program_id(1)
    tt, d = x_ref.shape[1], x_ref.shape[2]
    n_exp = rw_ref.shape[1]

    @pl.when((i == 0) & (b_id == 0))
    def _():
        for k, w_hbm in enumerate((wr_ref, wp_ref, wo_ref)):
            _load_cast(w_hbm, w3_s.at[k], wtmp_s, wsem)

    @pl.when(i == 0)
    def _():
        h_s[...] = h0_ref[0]

    @pl.when((i == 0) & (b_id == 0))
    def _():
        cnt_s[...] = jnp.zeros_like(cnt_s)

    xc = xc_ref[0].astype(F32)
    sp = _softplus(-lam_ref[1:2, :])
    a, b = _lru_coeffs(xc, wgb_ref, ba_ref[1:2, :], bx_ref[1:2, :], sp)

    def store(gi, hg):
        hb_s[gi * SUBLANES:(gi + 1) * SUBLANES, :] = hg

    h_s[...] = _scan_tile(a, b, h_s[...], True, store)

    gg = gg_ref[0].astype(F32)
    gelu = gg * _sigmoid(gg * (1.5957691216057308 + 0.07135481627260025 * (gg * gg)))
    y_rnn = (hf_ref[0] + hb_s[...]) * gelu

    xpb = xp_ref[0]
    grp = d // len(POOL_WINDOWS)
    y_parts = []
    for gi in range(len(POOL_WINDOWS)):
        cols = slice(gi * grp, (gi + 1) * grp)
        rows = []
        for c0 in range(0, tt, POOL_CHUNK):
            xg = xpb[c0:c0 + POOL_CHUNK, cols]
            mean = _dot(pm_ref[gi], xg) / pc_ref[gi]
            rows.append((mean - xg.astype(F32)).astype(BF16))
        dg = rows[0] if len(rows) == 1 else jnp.concatenate(rows, axis=0)
        y_parts.append(_dot(dg, pw_ref[gi]) * ps_ref[:, cols])
    y_pool = jnp.concatenate(y_parts, axis=1)

    merged = (_sigmoid(mr_ref[0].astype(F32)) * _dot(y_rnn.astype(BF16), w3_s[0])
              + _sigmoid(mp_ref[0].astype(F32)) * _dot(y_pool.astype(BF16), w3_s[1]))
    m_lat = _dot(merged.astype(BF16), w3_s[2])
    x_mid = x_ref[0] + g1_ref[0] * _rmsnorm(m_lat, npost_ref[...])
    xmid_ref[0] = x_mid
    v = _modulate(_rmsnorm(x_mid, npre_ref[...]), sh2_ref[0], sc2_ref[0])
    v_ref[0] = v.reshape(tt, SUBLANES, LANES)

    logits = _dot3(v, rw_ref[...]) + rb_ref[...]
    lane = lax.broadcasted_iota(jnp.int32, (tt, n_exp), 1).astype(F32)
    work = logits
    vals, idxs, sels = [], [], []
    for _ in range(TOP_K):
        m = jnp.max(work, axis=-1, keepdims=True)
        idx = jnp.min(jnp.where(work == m, lane, float(n_exp)), axis=-1, keepdims=True)
        sel = lane == idx
        vals.append(m)
        idxs.append(idx)
        sels.append(sel)
        work = jnp.where(sel, -jnp.inf, work)
    exps = [jnp.exp(vk - vals[0]) for vk in vals]
    den = exps[0] + exps[1] + exps[2] + exps[3]
    anyf = jnp.zeros((tt, n_exp), F32)
    for sel in sels:
        anyf = anyf + sel.astype(F32)
    r_i = lax.broadcasted_iota(jnp.int32, (tt, tt), 0)
    c_i = lax.broadcasted_iota(jnp.int32, (tt, tt), 1)
    lower = (c_i < r_i).astype(BF16)
    before = _dot(lower, anyf.astype(BF16)) + cnt_s[...]
    cnt_new = cnt_s[...] + jnp.sum(anyf, axis=0, keepdims=True)
    cnt_s[...] = cnt_new
    cnt_ref[...] = cnt_new

    lane_o = lax.broadcasted_iota(jnp.int32, (tt, LANES), 1)
    code_o = jnp.zeros((tt, LANES), jnp.int32)
    wts_o = jnp.zeros((tt, LANES), F32)
    for k in range(TOP_K):
        rk = jnp.sum(jnp.where(sels[k], before, 0.0), axis=-1, keepdims=True)
        code = idxs[k].astype(jnp.int32) * RANK_RADIX + rk.astype(jnp.int32)
        code_o = jnp.where(lane_o == k, code, code_o)
        wts_o = jnp.where(lane_o == k, exps[k] / den, wts_o)
    code_ref[0] = code_o
    wts_ref[0] = wts_o


def _mix_bwd_call(hf, xc, gg, xp, mr, mp, x, wgb, ba, bx, lam, h0b, pm, pc, pw, ps, wr, wp, wo,
                  npost, npre, g1, sh2, sc2, rw, rb):
    bsz, t, d = x.shape
    nt = t // TT
    n_exp = rw.shape[1]
    rev = lambda b, i: (b, nt - 1 - i, 0)
    tile = pl.BlockSpec((1, TT, d), rev)
    small = pl.BlockSpec((1, TT, LANES), rev)
    per_b = pl.BlockSpec((1, 1, d), lambda b, i: (b, 0, 0))
    full = lambda shp: pl.BlockSpec(shp, lambda b, i: (0,) * len(shp))
    return pl.pallas_call(
        _mix_bwd_body,
        out_shape=(jax.ShapeDtypeStruct((bsz, t, d), F32),
                   jax.ShapeDtypeStruct((bsz, t, SUBLANES, LANES), F32),
                   jax.ShapeDtypeStruct((bsz, t, LANES), jnp.int32),
                   jax.ShapeDtypeStruct((bsz, t, LANES), F32),
                   jax.ShapeDtypeStruct((1, n_exp), F32)),
        grid=(bsz, nt),
        in_specs=[tile] * 7 + [full(wgb.shape), full(ba.shape), full(bx.shape), full(lam.shape), per_b,
                               full(pm.shape), full(pc.shape), full(pw.shape), full((1, d)),
                               pl.BlockSpec(memory_space=pl.ANY), pl.BlockSpec(memory_space=pl.ANY),
                               pl.BlockSpec(memory_space=pl.ANY),
                               full((1, d)), full((1, d)), per_b, per_b, per_b,
                               full(rw.shape), full((1, n_exp))],
        out_specs=(tile, pl.BlockSpec((1, TT, SUBLANES, LANES), lambda b, i: (b, nt - 1 - i, 0, 0)),
                   small, small, full((1, n_exp))),
        scratch_shapes=[pltpu.VMEM((TT, d), F32), pltpu.VMEM((1, d), F32), pltpu.VMEM((1, n_exp), F32),
                        pltpu.VMEM((3, d, d), BF16), pltpu.VMEM((d, d), F32), pltpu.SemaphoreType.DMA],
        compiler_params=pltpu.CompilerParams(
            dimension_semantics=("arbitrary", "arbitrary"), vmem_limit_bytes=VMEM_LIMIT),
        name="mix_bwd",
    )(hf, xc, gg, xp, mr, mp, x, wgb, ba, bx, lam, h0b, pm, pc, pw, ps, wr, wp, wo,
      npost, npre, g1, sh2, sc2, rw, rb)


def _dispatch_body(n_pad_blocks, pos_ref, pad_ref, zero_ref, v_hbm, xs_hbm, vbuf, lsem, ssem):
    i = pl.program_id(0)
    steps = pl.num_programs(0)
    n_pad = pad_ref.shape[0]
    slot = lax.rem(i, DISPATCH_BUFS)

    def load(step):
        s = lax.rem(step, DISPATCH_BUFS)
        return pltpu.make_async_copy(v_hbm.at[pl.ds(step * TD, TD)], vbuf.at[s], lsem.at[s])

    def row_copy(s, src_row, dst_row):
        return pltpu.make_async_copy(vbuf.at[s, src_row], xs_hbm.at[dst_row], ssem.at[s])

    def pad_copy(s, dst_row):
        return pltpu.make_async_copy(zero_ref.at[0], xs_hbm.at[dst_row], ssem.at[s])

    def wait_scatter(step):
        s = lax.rem(step, DISPATCH_BUFS)

        def body(t, carry):
            for _ in range(TOP_K):
                row_copy(s, 0, 0).wait()
            return carry

        lax.fori_loop(0, TD, body, 0, unroll=ISSUE_UNROLL)

        @pl.when(step < n_pad_blocks)
        def _():
            def body_pad(q, carry):
                pad_copy(s, 0).wait()
                return carry

            lax.fori_loop(0, n_pad, body_pad, 0, unroll=ISSUE_UNROLL)

    @pl.when(i == 0)
    def _():
        load(0).start()

    @pl.when(i >= 2)
    def _():
        wait_scatter(i - 2)

    @pl.when(i + 1 < steps)
    def _():
        load(i + 1).start()

    load(i).wait()

    def issue(t, carry):
        for k in range(TOP_K):
            row_copy(slot, t, pos_ref[t * TOP_K + k]).start(priority=k % DMA_PRIORITIES)
        return carry

    lax.fori_loop(0, TD, issue, 0, unroll=ISSUE_UNROLL)

    @pl.when(i < n_pad_blocks)
    def _():
        def issue_pad(q, carry):
            pad_copy(slot, pad_ref[q]).start()
            return carry

        lax.fori_loop(0, n_pad, issue_pad, 0, unroll=ISSUE_UNROLL)

    @pl.when(i == steps - 1)
    def _():
        @pl.when(i >= 1)
        def _():
            wait_scatter(i - 1)

        wait_scatter(i)


def _dispatch_call(pos_flat, pad_slots, v3, n_slots):
    n = v3.shape[0]
    slab = v3.shape[1:]
    steps = n // TD
    n_pad_blocks = pad_slots.shape[0] // PAD_BLOCK
    assert n_pad_blocks <= steps
    return pl.pallas_call(
        functools.partial(_dispatch_body, n_pad_blocks),
        out_shape=jax.ShapeDtypeStruct((n_slots,) + slab, F32),
        grid=(steps,),
        in_specs=[pl.BlockSpec((TD * TOP_K,), lambda i: (i,), memory_space=pltpu.SMEM),
                  pl.BlockSpec((PAD_BLOCK,), lambda i: (jnp.minimum(i, n_pad_blocks - 1),),
                               memory_space=pltpu.SMEM),
                  pl.BlockSpec((1,) + slab, lambda i: (0, 0, 0)),
                  pl.BlockSpec(memory_space=pl.ANY)],
        out_specs=pl.BlockSpec(memory_space=pl.ANY),
        scratch_shapes=[pltpu.VMEM((DISPATCH_BUFS, TD) + slab, F32),
                        pltpu.SemaphoreType.DMA((DISPATCH_BUFS,)), pltpu.SemaphoreType.DMA((DISPATCH_BUFS,))],
        compiler_params=pltpu.CompilerParams(dimension_semantics=("arbitrary",)),
        name="dispatch",
    )(pos_flat, pad_slots, jnp.zeros((1,) + slab, F32), v3)


def _expert_body(te_ref, blk_ref, nvalid_ref, slot_ref, nxt_ref,
                 xs_ref, w1_hbm, b1_ref, w2_hbm, b2_ref, y_ref, w1f_s, w2f_s, w1b_s, w2b_s, sem):
    j = pl.program_id(0)
    n_exp = w1_hbm.shape[0]
    ff = w2_hbm.shape[1]
    e = te_ref[j]
    prev = te_ref[jnp.maximum(j - 1, 0)]

    def weight_copies(expert, slot):
        return (pltpu.make_async_copy(w1_hbm.at[expert], w1f_s.at[slot], sem.at[slot, 0]),
                pltpu.make_async_copy(w2_hbm.at[expert], w2f_s.at[slot], sem.at[slot, 1]))

    @pl.when(j == 0)
    def _():
        for c in weight_copies(e, slot_ref[e]):
            c.start()

    @pl.when((j == 0) | (e != prev))
    def _():
        slot = slot_ref[e]
        for c in weight_copies(e, slot):
            c.wait()
        w1b_s[...] = w1f_s[slot].astype(BF16)
        w2b_s[...] = w2f_s[slot].astype(BF16)

        @pl.when(nxt_ref[e] < n_exp)
        def _():
            for c in weight_copies(nxt_ref[e], 1 - slot):
                c.start()

    @pl.when(j < nvalid_ref[0])
    def _():
        tm = xs_ref.shape[0]
        x = xs_ref[...].reshape(tm, w1_hbm.shape[1])
        z = _dot(x.astype(BF16), w1b_s[...]) + b1_ref[0]
        glu = jnp.minimum(z[:, :ff], SWIGLU_LIMIT)
        lin = jnp.clip(z[:, ff:], -SWIGLU_LIMIT, SWIGLU_LIMIT)
        act = glu * _sigmoid(SWIGLU_ALPHA * glu) * (lin + 1.0)
        y = _dot(act.astype(BF16), w2b_s[...]) + b2_ref[0]
        y_ref[...] = y.reshape(y_ref.shape)

    @pl.when(j >= nvalid_ref[0])
    def _():
        y_ref[...] = jnp.zeros_like(y_ref)


def _expert_call(te, blk, nvalid, slot, nxt, xs, w1, b1, w2, b2):
    n_slots = xs.shape[0]
    slab = xs.shape[1:]
    n_exp, d, ff2 = w1.shape
    ff = w2.shape[1]
    grid_spec = pltpu.PrefetchScalarGridSpec(
        num_scalar_prefetch=5,
        grid=(n_slots // TM,),
        in_specs=[pl.BlockSpec((TM,) + slab, lambda j, te, blk, *_: (blk[j], 0, 0)),
                  pl.BlockSpec(memory_space=pl.ANY),
                  pl.BlockSpec((1, 1, ff2), lambda j, te, *_: (te[j], 0, 0)),
                  pl.BlockSpec(memory_space=pl.ANY),
                  pl.BlockSpec((1, 1, d), lambda j, te, *_: (te[j], 0, 0))],
        out_specs=pl.BlockSpec((TM,) + slab, lambda j, *_: (j, 0, 0)),
        scratch_shapes=[pltpu.VMEM((2, d, ff2), F32), pltpu.VMEM((2, ff, d), F32),
                        pltpu.VMEM((d, ff2), BF16), pltpu.VMEM((ff, d), BF16),
                        pltpu.SemaphoreType.DMA((2, 2))],
    )
    return pl.pallas_call(
        _expert_body,
        out_shape=jax.ShapeDtypeStruct((n_slots,) + slab, F32),
        grid_spec=grid_spec,
        compiler_params=pltpu.CompilerParams(
            dimension_semantics=("arbitrary",), vmem_limit_bytes=VMEM_LIMIT),
        name="experts",
    )(te, blk, nvalid, slot, nxt, xs, w1, b1, w2, b2)


def _combine_body(pos_ref, posn_ref, wts_ref, xmid_ref, g2_ref, npost_ref, y_hbm, o_ref, buf, sem):
    i = pl.program_id(0)
    cur = lax.rem(i, 2)

    def row_copy(p_ref, slot, t, k):
        return pltpu.make_async_copy(y_hbm.at[p_ref[t * TOP_K + k]], buf.at[slot, k, t], sem.at[slot])

    def issue(p_ref, slot):
        def body(t, carry):
            for k in range(TOP_K):
                row_copy(p_ref, slot, t, k).start(priority=k % DMA_PRIORITIES)
            return carry

        lax.fori_loop(0, TD, body, 0, unroll=ISSUE_UNROLL)

    @pl.when(i == 0)
    def _():
        issue(pos_ref, 0)

    @pl.when(i + 1 < pl.num_programs(0))
    def _():
        issue(posn_ref, 1 - cur)

    def drain(t, carry):
        for k in range(TOP_K):
            row_copy(pos_ref, cur, t, k).wait()
        return carry

    lax.fori_loop(0, TD, drain, 0, unroll=ISSUE_UNROLL)

    w = wts_ref[...]
    f = w[:, 0:1] * buf[cur, 0].reshape(xmid_ref.shape)
    for k in range(1, TOP_K):
        f = f + w[:, k:k + 1] * buf[cur, k].reshape(xmid_ref.shape)
    o_ref[...] = xmid_ref[...] + g2_ref[0] * _rmsnorm(f, npost_ref[...])


def _combine_call(pos_flat, wts, xmid2, g2, npost, ys, t_per_batch):
    n, d = xmid2.shape
    steps = n // TD
    pos_spec = lambda f: pl.BlockSpec((TD * TOP_K,), f, memory_space=pltpu.SMEM)
    return pl.pallas_call(
        _combine_body,
        out_shape=jax.ShapeDtypeStruct((n, d), F32),
        grid=(steps,),
        in_specs=[pos_spec(lambda i: (i,)),
                  pos_spec(lambda i: (jnp.minimum(i + 1, steps - 1),)),
                  pl.BlockSpec((TD, LANES), lambda i: (i, 0)),
                  pl.BlockSpec((TD, d), lambda i: (i, 0)),
                  pl.BlockSpec((1, 1, d), lambda i: ((i * TD) // t_per_batch, 0, 0)),
                  pl.BlockSpec((1, d), lambda i: (0, 0)),
                  pl.BlockSpec(memory_space=pl.ANY)],
        out_specs=pl.BlockSpec((TD, d), lambda i: (i, 0)),
        scratch_shapes=[pltpu.VMEM((2, TOP_K, TD) + ys.shape[1:], F32), pltpu.SemaphoreType.DMA((2,))],
        compiler_params=pltpu.CompilerParams(dimension_semantics=("arbitrary",)),
        name="combine",
    )(pos_flat, pos_flat, wts, xmid2, g2, npost, ys)


def _count_le(ends, q):
    return jnp.sum((ends[None, :] <= q[:, None]).astype(jnp.int32), axis=1)


def _gate_weights(wa, wx):
    return jnp.concatenate([wa, wx], axis=-1).astype(BF16)


def kernel(x, c, ctx, c_ctx, w_mod, b_mod, norm_pre_mix, norm_post_mix, norm_pre_ffn, norm_post_ffn, w_in, conv_w, conv_b, lru_wa, lru_ba, lru_wx, lru_bx, lru_lambda, pool_w, pool_scale, w_rnn_proj, w_pool_proj, w_out, router_w, router_b, exp_w1, exp_b1, exp_w2, exp_b2):
    bsz, t, d = x.shape
    assert w_mod.shape[0] == 1, "single-layer block"
    assert t % TT == 0 and TT % POOL_CHUNK == 0 and POOL_CHUNK % GRID_W == 0 and t % TD == 0
    assert d == SUBLANES * LANES, "a token row must be exactly one (8, 128) slab"
    n = bsz * t
    n_exp = router_w.shape[-1]
    assert (n * TOP_K) % TM == 0 and (n_exp * TM) % PAD_BLOCK == 0 and n <= RANK_RADIX
    row = lambda a: a.reshape(1, -1)

    pad = jnp.zeros((SUBLANES - (bsz + 1) % SUBLANES, d), F32)
    cc = jnp.concatenate([c, c_ctx[None, :], pad], axis=0)
    mod = _mod_call(cc, w_mod[0], row(b_mod[0]))
    lat = lambda j: mod[:bsz, j * d:(j + 1) * d].reshape(bsz, 1, d)
    sh1, sc1, g1, sh2, sc2, g2 = (lat(j) for j in range(6))
    csh1, csc1 = mod[bsz:bsz + 1, 0:d], mod[bsz:bsz + 1, d:2 * d]

    wgf = _gate_weights(lru_wa[0, 0], lru_wx[0, 0])
    wgb = _gate_weights(lru_wa[0, 1], lru_wx[0, 1])
    ba, bx, lam = lru_ba[0], lru_bx[0], lru_lambda[0]
    cw, cb = conv_w[0], row(conv_b[0])
    npre_mix = row(norm_pre_mix[0])

    h0f, h0b = _ctx_call(ctx, npre_mix, csh1, csc1, w_in[0], cw, cb, wgf, wgb, ba, bx, lam)

    hf, xc, gg, xp, mr, mp = _mix_fwd_call(x, npre_mix, sh1, sc1, w_in[0], cw, cb, wgf, ba, bx, lam, h0f)

    pm, pc = _pool_consts()
    xmid, v, code_o, wts_o, cnt = _mix_bwd_call(
        hf, xc, gg, xp, mr, mp, x, wgb, ba, bx, lam, h0b,
        pm, pc, pool_w[0].astype(BF16), row(pool_scale[0]),
        w_rnn_proj[0], w_pool_proj[0], w_out[0],
        row(norm_post_mix[0]), row(norm_pre_ffn[0]), g1, sh2, sc2, router_w[0], row(router_b[0]))

    code = code_o.reshape(n, LANES)[:, :TOP_K].T
    idx, rank = code // RANK_RADIX, code % RANK_RADIX
    counts = cnt[0].astype(jnp.int32)
    tiles_e = (counts + TM - 1) // TM
    tile_end = jnp.cumsum(tiles_e)
    offs = (tile_end - tiles_e) * TM
    e_ids = jnp.arange(n_exp, dtype=jnp.int32)
    off_of = jnp.sum(jnp.where(idx[None] == e_ids[:, None, None], offs[:, None, None], 0), axis=0)
    pos_flat = (off_of + rank).T.reshape(n * TOP_K)
    n_tiles = (n * TOP_K) // TM + n_exp
    n_valid = tile_end[-1]
    jj = jnp.arange(n_tiles, dtype=jnp.int32)
    blk = jnp.minimum(jj, n_valid - 1)
    te = jnp.minimum(_count_le(tile_end, blk), n_exp - 1)

    pad_e = tiles_e * TM - counts
    pad_end = jnp.cumsum(pad_e)
    qq = jnp.arange(n_exp * TM, dtype=jnp.int32)
    e_q = jnp.minimum(_count_le(pad_end, qq), n_exp - 1)
    slot_in = (offs + counts)[e_q] + qq - (pad_end - pad_e)[e_q]
    slot_tail = n_valid * TM + qq - pad_end[-1]
    pad_slots = jnp.where(qq < pad_end[-1], slot_in, slot_tail).astype(jnp.int32)

    xs = _dispatch_call(pos_flat, pad_slots, v.reshape((n,) + v.shape[2:]), n_tiles * TM)
    has = tiles_e > 0
    w_slot = ((jnp.cumsum(has.astype(jnp.int32)) - has.astype(jnp.int32)) % 2).astype(jnp.int32)
    later = (e_ids[None, :] > e_ids[:, None]) & has[None, :]
    w_next = jnp.min(jnp.where(later, e_ids[None, :], n_exp), axis=1).astype(jnp.int32)
    ys = _expert_call(te, blk, n_valid.reshape(1), w_slot, w_next, xs, exp_w1[0],
                      exp_b1[0].reshape(n_exp, 1, -1), exp_w2[0], exp_b2[0].reshape(n_exp, 1, -1))
    out = _combine_call(pos_flat, wts_o.reshape(n, LANES), xmid.reshape(n, d), g2, row(norm_post_ffn[0]), ys, t)
    return out.reshape(bsz, t, d)
```

```python
import functools

import numpy as np
import jax
import jax.numpy as jnp
from jax import lax
from jax.experimental import pallas as pl
from jax.experimental.pallas import tpu as pltpu

F32 = jnp.float32
BF16 = jnp.bfloat16

RNN_HEADS = 4
CONV_W = 4
LRU_C = 8.0
POOL_WINDOWS = (2, 4, 8, 16)
GRID_W = 64
TOP_K = 4
SWIGLU_LIMIT = 7.0
SWIGLU_ALPHA = 1.702
EPS = 1e-6
SQRT_FLOOR = 1e-30

SUBLANES = 8
LANES = 128
VMEM_LIMIT = 56 * 1024 * 1024

TT = 256
POOL_CHUNK = 256
TM = 256
TD = 128
PAD_BLOCK = 128
DISPATCH_BUFS = 3
ISSUE_UNROLL = 8
DMA_PRIORITIES = 2
RANK_RADIX = 1 << 16


def _sigmoid(x):
    return 1.0 / (1.0 + jnp.exp(-x))


def _softplus(z):
    return jnp.maximum(z, 0.0) + jnp.log1p(jnp.exp(-jnp.abs(z)))


def _sqrt_nonneg(x):
    return x * lax.rsqrt(jnp.maximum(x, SQRT_FLOOR))


def _rmsnorm(x, g):
    ms = jnp.mean(x * x, axis=-1, keepdims=True)
    return (x * lax.rsqrt(ms + EPS)) * g


def _modulate(u, shift, scale):
    return u * (1.0 + scale) + shift


def _dot(a, b):
    return jnp.dot(a, b, preferred_element_type=F32)


def _split(x):
    hi = x.astype(BF16)
    return hi, (x - hi.astype(F32)).astype(BF16)


def _dot3(a, b):
    ah, al = _split(a)
    bh, bl = _split(b)
    return _dot(ah, bh) + (_dot(al, bh) + _dot(ah, bl))


def _load_cast(w_hbm, dst_s, tmp_s, sem):
    c = tmp_s.shape[1]
    for j in range(w_hbm.shape[1] // c):
        cp = pltpu.make_async_copy(w_hbm.at[:, pl.ds(j * c, c)], tmp_s, sem)
        cp.start()
        cp.wait()
        dst_s[:, j * c:(j + 1) * c] = tmp_s[...].astype(BF16)


def _conv(prev8, xr, next8, cw, cb):
    t = xr.shape[0]
    ext = jnp.concatenate([prev8, xr, next8], axis=0)
    acc = cb + cw[0:1, :] * ext[SUBLANES - 2:SUBLANES - 2 + t, :]
    for k in range(1, CONV_W):
        off = SUBLANES - 2 + k
        acc = acc + cw[k:k + 1, :] * ext[off:off + t, :]
    return acc


def _lru_coeffs(xc, wg_ref, ba, bx, sp):
    hd = xc.shape[1] // RNN_HEADS
    xcb = xc.astype(BF16)
    a_parts, b_parts = [], []
    for h in range(RNN_HEADS):
        cols = slice(h * hd, (h + 1) * hd)
        z = _dot(xcb[:, cols], wg_ref[h])
        r = _sigmoid(z[:, :hd] + ba[:, cols])
        i = _sigmoid(z[:, hd:] + bx[:, cols])
        a = jnp.exp((-LRU_C) * r * sp[:, cols])
        b = _sqrt_nonneg(1.0 - a * a) * (i * xc[:, cols])
        a_parts.append(a)
        b_parts.append(b)
    return jnp.concatenate(a_parts, axis=1), jnp.concatenate(b_parts, axis=1)


def _scan_tile(a, b, h_in, reverse, store):
    t, c = a.shape
    g = t // SUBLANES
    a3 = a.reshape(g, SUBLANES, c)
    b3 = b.reshape(g, SUBLANES, c)
    row = lax.broadcasted_iota(jnp.int32, (g, SUBLANES, c), 1)
    for s in (1, 2, 4):
        if reverse:
            shift, m = SUBLANES - s, row < SUBLANES - s
        else:
            shift, m = s, row >= s
        ra = pltpu.roll(a3, shift, axis=1)
        rb = pltpu.roll(b3, shift, axis=1)
        b3 = a3 * jnp.where(m, rb, 0.0) + b3
        a3 = a3 * jnp.where(m, ra, 1.0)
    h = h_in
    order = range(g - 1, -1, -1) if reverse else range(g)
    for gi in order:
        hg = a3[gi] * h + b3[gi]
        store(gi, hg)
        h = hg[0:1, :] if reverse else hg[SUBLANES - 1:SUBLANES, :]
    return h


def _mod_body(c_ref, w_ref, b_ref, o_ref):
    c = c_ref[...]
    s = c * _sigmoid(c)
    o_ref[...] = _dot3(s, w_ref[...]) + b_ref[...]


def _mod_call(cc, w_mod, b_mod):
    d = cc.shape[1]
    n = w_mod.shape[1] // d
    return pl.pallas_call(
        _mod_body,
        out_shape=jax.ShapeDtypeStruct((cc.shape[0], n * d), F32),
        grid=(n,),
        in_specs=[pl.BlockSpec(cc.shape, lambda j: (0, 0)),
                  pl.BlockSpec((d, d), lambda j: (0, j)),
                  pl.BlockSpec((1, d), lambda j: (0, j))],
        out_specs=pl.BlockSpec((cc.shape[0], d), lambda j: (0, j)),
        name="mod",
    )(cc, w_mod, b_mod)


def _ctx_body(ctx_ref, g_ref, sh_ref, sc_ref, w_ref, cw_ref, cb_ref, wgf_ref, wgb_ref,
              ba_ref, bx_ref, lam_ref, hf_ref, hb_ref):
    d = ctx_ref.shape[2]
    u = _modulate(_rmsnorm(ctx_ref[0], g_ref[...]), sh_ref[...], sc_ref[...])
    xr = _dot(u.astype(BF16), w_ref[...].astype(BF16))
    z8 = jnp.zeros((SUBLANES, d), F32)
    xc = _conv(z8, xr, z8, cw_ref[...], cb_ref[...])
    h0 = jnp.zeros((1, d), F32)
    for di, (wg_ref, out_ref) in enumerate(((wgf_ref, hf_ref), (wgb_ref, hb_ref))):
        sp = _softplus(-lam_ref[di:di + 1, :])
        a, b = _lru_coeffs(xc, wg_ref, ba_ref[di:di + 1, :], bx_ref[di:di + 1, :], sp)
        out_ref[0] = _scan_tile(a, b, h0, di == 1, lambda gi, hg: None)


def _ctx_call(ctx, g, sh, sc, w_in, cw, cb, wgf, wgb, ba, bx, lam):
    bsz, tc, d = ctx.shape
    full = lambda shp: pl.BlockSpec(shp, lambda b: (0,) * len(shp))
    return pl.pallas_call(
        _ctx_body,
        out_shape=(jax.ShapeDtypeStruct((bsz, 1, d), F32),) * 2,
        grid=(bsz,),
        in_specs=[pl.BlockSpec((1, tc, d), lambda b: (b, 0, 0)),
                  full((1, d)), full((1, d)), full((1, d)),
                  pl.BlockSpec((d, d), lambda b: (0, 0)),
                  full(cw.shape), full((1, d)), full(wgf.shape), full(wgb.shape),
                  full(ba.shape), full(bx.shape), full(lam.shape)],
        out_specs=(pl.BlockSpec((1, 1, d), lambda b: (b, 0, 0)),) * 2,
        compiler_params=pltpu.CompilerParams(vmem_limit_bytes=VMEM_LIMIT),
        name="ctx",
    )(ctx, g, sh, sc, w_in, cw, cb, wgf, wgb, ba, bx, lam)


def _mix_fwd_body(x_ref, xn_ref, g_ref, sh_ref, sc_ref, win_ref, cw_ref, cb_ref, wgf_ref,
                  ba_ref, bx_ref, lam_ref, h0_ref,
                  hf_ref, xc_ref, gg_ref, xp_ref, mr_ref, mp_ref, tail_s, h_s, win_s, wtmp_s, wsem):
    i = pl.program_id(1)
    last = pl.num_programs(1) - 1
    tt, d = x_ref.shape[1], x_ref.shape[2]

    @pl.when((i == 0) & (pl.program_id(0) == 0))
    def _():
        _load_cast(win_ref, win_s, wtmp_s, wsem)

    @pl.when(i == 0)
    def _():
        tail_s[...] = jnp.zeros_like(tail_s)
        h_s[...] = h0_ref[0]

    g, sh, sc = g_ref[...], sh_ref[0], sc_ref[0]
    ub = _modulate(_rmsnorm(x_ref[0], g), sh, sc).astype(BF16)
    unb = _modulate(_rmsnorm(xn_ref[0], g), sh, sc).astype(BF16)
    w_rnn = win_s[:, 0:d]
    xr = _dot(ub, w_rnn)
    xrn = jnp.where(i == last, 0.0, _dot(unb, w_rnn))
    xc = _conv(tail_s[...], xr, xrn, cw_ref[...], cb_ref[...])
    tail_s[...] = xr[tt - SUBLANES:tt, :]
    xc_ref[0] = xc.astype(BF16)
    for j, ref in enumerate((gg_ref, xp_ref, mr_ref, mp_ref)):
        ref[0] = _dot(ub, win_s[:, (j + 1) * d:(j + 2) * d]).astype(BF16)

    sp = _softplus(-lam_ref[0:1, :])
    a, b = _lru_coeffs(xc, wgf_ref, ba_ref[0:1, :], bx_ref[0:1, :], sp)

    def store(gi, hg):
        hf_ref[0, gi * SUBLANES:(gi + 1) * SUBLANES, :] = hg

    h_s[...] = _scan_tile(a, b, h_s[...], False, store)


def _mix_fwd_call(x, g, sh, sc, w_in, cw, cb, wgf, ba, bx, lam, h0f):
    bsz, t, d = x.shape
    nt = t // TT
    nblk8 = t // SUBLANES
    per_b = pl.BlockSpec((1, 1, d), lambda b, i: (b, 0, 0))
    full = lambda shp: pl.BlockSpec(shp, lambda b, i: (0,) * len(shp))
    tile = pl.BlockSpec((1, TT, d), lambda b, i: (b, i, 0))
    return pl.pallas_call(
        _mix_fwd_body,
        out_shape=(jax.ShapeDtypeStruct((bsz, t, d), F32),) + (jax.ShapeDtypeStruct((bsz, t, d), BF16),) * 5,
        grid=(bsz, nt),
        in_specs=[tile,
                  pl.BlockSpec((1, SUBLANES, d),
                               lambda b, i: (b, jnp.minimum((i + 1) * (TT // SUBLANES), nblk8 - 1), 0)),
                  full((1, d)), per_b, per_b,
                  pl.BlockSpec(memory_space=pl.ANY), full(cw.shape), full((1, d)), full(wgf.shape),
                  full(ba.shape), full(bx.shape), full(lam.shape), per_b],
        out_specs=(tile,) * 6,
        scratch_shapes=[pltpu.VMEM((SUBLANES, d), F32), pltpu.VMEM((1, d), F32),
                        pltpu.VMEM(w_in.shape, BF16), pltpu.VMEM((d, d), F32), pltpu.SemaphoreType.DMA],
        compiler_params=pltpu.CompilerParams(
            dimension_semantics=("arbitrary", "arbitrary"), vmem_limit_bytes=VMEM_LIMIT),
        name="mix_fwd",
    )(x, x, g, sh, sc, w_in, cw, cb, wgf, ba, bx, lam, h0f)


def _pool_consts():
    p = np.arange(POOL_CHUNK)
    pos, line = p % GRID_W, p // GRID_W
    mats, cnts = [], []
    for w in POOL_WINDOWS:
        lo = np.clip(pos - w // 2, 0, GRID_W)
        hi = np.clip(pos + w - w // 2, 0, GRID_W)
        m = (line[:, None] == line[None, :]) & (pos[None, :] >= lo[:, None]) & (pos[None, :] < hi[:, None])
        mats.append(m.astype(np.float32))
        cnts.append((hi - lo).astype(np.float32)[:, None])
    return jnp.asarray(np.stack(mats), BF16), jnp.asarray(np.stack(cnts), F32)


def _mix_bwd_body(hf_ref, xc_ref, gg_ref, xp_ref, mr_ref, mp_ref, x_ref,
                  wgb_ref, ba_ref, bx_ref, lam_ref, h0_ref,
                  pm_ref, pc_ref, pw_ref, ps_ref, wr_ref, wp_ref, wo_ref,
                  npost_ref, npre_ref, g1_ref, sh2_ref, sc2_ref, rw_ref, rb_ref,
                  xmid_ref, v_ref, code_ref, wts_ref, cnt_ref,
                  hb_s, h_s, cnt_s, w3_s, wtmp_s, wsem):
    b_id, i = pl.program_id(0), pl.program_id(1)
    tt, d = x_ref.shape[1], x_ref.shape[2]
    n_exp = rw_ref.shape[1]

    @pl.when((i == 0) & (b_id == 0))
    def _():
        for k, w_hbm in enumerate((wr_ref, wp_ref, wo_ref)):
            _load_cast(w_hbm, w3_s.at[k], wtmp_s, wsem)

    @pl.when(i == 0)
    def _():
        h_s[...] = h0_ref[0]

    @pl.when((i == 0) & (b_id == 0))
    def _():
        cnt_s[...] = jnp.zeros_like(cnt_s)

    xc = xc_ref[0].astype(F32)
    sp = _softplus(-lam_ref[1:2, :])
    a, b = _lru_coeffs(xc, wgb_ref, ba_ref[1:2, :], bx_ref[1:2, :], sp)

    def store(gi, hg):
        hb_s[gi * SUBLANES:(gi + 1) * SUBLANES, :] = hg

    h_s[...] = _scan_tile(a, b, h_s[...], True, store)

    gg = gg_ref[0].astype(F32)
    gelu = gg * _sigmoid(gg * (1.5957691216057308 + 0.07135481627260025 * (gg * gg)))
    y_rnn = (hf_ref[0] + hb_s[...]) * gelu

    xpb = xp_ref[0]
    grp = d // len(POOL_WINDOWS)
    y_parts = []
    for gi in range(len(POOL_WINDOWS)):
        cols = slice(gi * grp, (gi + 1) * grp)
        rows = []
        for c0 in range(0, tt, POOL_CHUNK):
            xg = xpb[c0:c0 + POOL_CHUNK, cols]
            mean = _dot(pm_ref[gi], xg) / pc_ref[gi]
            rows.append((mean - xg.astype(F32)).astype(BF16))
        dg = rows[0] if len(rows) == 1 else jnp.concatenate(rows, axis=0)
        y_parts.append(_dot(dg, pw_ref[gi]) * ps_ref[:, cols])
    y_pool = jnp.concatenate(y_parts, axis=1)

    merged = (_sigmoid(mr_ref[0].astype(F32)) * _dot(y_rnn.astype(BF16), w3_s[0])
              + _sigmoid(mp_ref[0].astype(F32)) * _dot(y_pool.astype(BF16), w3_s[1]))
    m_lat = _dot(merged.astype(BF16), w3_s[2])
    x_mid = x_ref[0] + g1_ref[0] * _rmsnorm(m_lat, npost_ref[...])
    xmid_ref[0] = x_mid
    v = _modulate(_rmsnorm(x_mid, npre_ref[...]), sh2_ref[0], sc2_ref[0])
    v_ref[0] = v.reshape(tt, SUBLANES, LANES)

    logits = _dot3(v, rw_ref[...]) + rb_ref[...]
    lane = lax.broadcasted_iota(jnp.int32, (tt, n_exp), 1).astype(F32)
    work = logits
    vals, idxs, sels = [], [], []
    for _ in range(TOP_K):
        m = jnp.max(work, axis=-1, keepdims=True)
        idx = jnp.min(jnp.where(work == m, lane, float(n_exp)), axis=-1, keepdims=True)
        sel = lane == idx
        vals.append(m)
        idxs.append(idx)
        sels.append(sel)
        work = jnp.where(sel, -jnp.inf, work)
    exps = [jnp.exp(vk - vals[0]) for vk in vals]
    den = exps[0] + exps[1] + exps[2] + exps[3]
    anyf = jnp.zeros((tt, n_exp), F32)
    for sel in sels:
        anyf = anyf + sel.astype(F32)
    r_i = lax.broadcasted_iota(jnp.int32, (tt, tt), 0)
    c_i = lax.broadcasted_iota(jnp.int32, (tt, tt), 1)
    lower = (c_i < r_i).astype(BF16)
    before = _dot(lower, anyf.astype(BF16)) + cnt_s[...]
    cnt_new = cnt_s[...] + jnp.sum(anyf, axis=0, keepdims=True)
    cnt_s[...] = cnt_new
    cnt_ref[...] = cnt_new

    lane_o = lax.broadcasted_iota(jnp.int32, (tt, LANES), 1)
    code_o = jnp.zeros((tt, LANES), jnp.int32)
    wts_o = jnp.zeros((tt, LANES), F32)
    for k in range(TOP_K):
        rk = jnp.sum(jnp.where(sels[k], before, 0.0), axis=-1, keepdims=True)
        code = idxs[k].astype(jnp.int32) * RANK_RADIX + rk.astype(jnp.int32)
        code_o = jnp.where(lane_o == k, code, code_o)
        wts_o = jnp.where(lane_o == k, exps[k] / den, wts_o)
    code_ref[0] = code_o
    wts_ref[0] = wts_o


def _mix_bwd_call(hf, xc, gg, xp, mr, mp, x, wgb, ba, bx, lam, h0b, pm, pc, pw, ps, wr, wp, wo,
                  npost, npre, g1, sh2, sc2, rw, rb):
    bsz, t, d = x.shape
    nt = t // TT
    n_exp = rw.shape[1]
    rev = lambda b, i: (b, nt - 1 - i, 0)
    tile = pl.BlockSpec((1, TT, d), rev)
    small = pl.BlockSpec((1, TT, LANES), rev)
    per_b = pl.BlockSpec((1, 1, d), lambda b, i: (b, 0, 0))
    full = lambda shp: pl.BlockSpec(shp, lambda b, i: (0,) * len(shp))
    return pl.pallas_call(
        _mix_bwd_body,
        out_shape=(jax.ShapeDtypeStruct((bsz, t, d), F32),
                   jax.ShapeDtypeStruct((bsz, t, SUBLANES, LANES), F32),
                   jax.ShapeDtypeStruct((bsz, t, LANES), jnp.int32),
                   jax.ShapeDtypeStruct((bsz, t, LANES), F32),
                   jax.ShapeDtypeStruct((1, n_exp), F32)),
        grid=(bsz, nt),
        in_specs=[tile] * 7 + [full(wgb.shape), full(ba.shape), full(bx.shape), full(lam.shape), per_b,
                               full(pm.shape), full(pc.shape), full(pw.shape), full((1, d)),
                               pl.BlockSpec(memory_space=pl.ANY), pl.BlockSpec(memory_space=pl.ANY),
                               pl.BlockSpec(memory_space=pl.ANY),
                               full((1, d)), full((1, d)), per_b, per_b, per_b,
                               full(rw.shape), full((1, n_exp))],
        out_specs=(tile, pl.BlockSpec((1, TT, SUBLANES, LANES), lambda b, i: (b, nt - 1 - i, 0, 0)),
                   small, small, full((1, n_exp))),
        scratch_shapes=[pltpu.VMEM((TT, d), F32), pltpu.VMEM((1, d), F32), pltpu.VMEM((1, n_exp), F32),
                        pltpu.VMEM((3, d, d), BF16), pltpu.VMEM((d, d), F32), pltpu.SemaphoreType.DMA],
        compiler_params=pltpu.CompilerParams(
            dimension_semantics=("arbitrary", "arbitrary"), vmem_limit_bytes=VMEM_LIMIT),
        name="mix_bwd",
    )(hf, xc, gg, xp, mr, mp, x, wgb, ba, bx, lam, h0b, pm, pc, pw, ps, wr, wp, wo,
      npost, npre, g1, sh2, sc2, rw, rb)


def _dispatch_body(n_pad_blocks, *refs):
    pos_refs = refs[:TOP_K]
    pad_ref, zero_ref, v_hbm, xs_hbm, vbuf, lsem, ssem = refs[TOP_K:]
    i = pl.program_id(0)
    steps = pl.num_programs(0)
    n_pad = pad_ref.shape[0]
    slot = lax.rem(i, DISPATCH_BUFS)

    def load(step):
        s = lax.rem(step, DISPATCH_BUFS)
        return pltpu.make_async_copy(v_hbm.at[pl.ds(step * TD, TD)], vbuf.at[s], lsem.at[s])

    def row_copy(s, src_row, dst_row):
        return pltpu.make_async_copy(vbuf.at[s, src_row], xs_hbm.at[dst_row], ssem.at[s])

    def pad_copy(s, dst_row):
        return pltpu.make_async_copy(zero_ref.at[0], xs_hbm.at[dst_row], ssem.at[s])

    def wait_scatter(step):
        s = lax.rem(step, DISPATCH_BUFS)

        def body(t, carry):
            for _ in range(TOP_K):
                row_copy(s, 0, 0).wait()
            return carry

        lax.fori_loop(0, TD, body, 0, unroll=ISSUE_UNROLL)

        @pl.when(step < n_pad_blocks)
        def _():
            def body_pad(q, carry):
                pad_copy(s, 0).wait()
                return carry

            lax.fori_loop(0, n_pad, body_pad, 0, unroll=ISSUE_UNROLL)

    @pl.when(i == 0)
    def _():
        load(0).start()

    @pl.when(i >= 2)
    def _():
        wait_scatter(i - 2)

    @pl.when(i + 1 < steps)
    def _():
        load(i + 1).start()

    load(i).wait()

    def issue(t, carry):
        for k in range(TOP_K):
            row_copy(slot, t, pos_refs[k][t]).start(priority=k % DMA_PRIORITIES)
        return carry

    lax.fori_loop(0, TD, issue, 0, unroll=ISSUE_UNROLL)

    @pl.when(i < n_pad_blocks)
    def _():
        def issue_pad(q, carry):
            pad_copy(slot, pad_ref[q]).start()
            return carry

        lax.fori_loop(0, n_pad, issue_pad, 0, unroll=ISSUE_UNROLL)

    @pl.when(i == steps - 1)
    def _():
        @pl.when(i >= 1)
        def _():
            wait_scatter(i - 1)

        wait_scatter(i)


def _pos_specs(steps, step_of):
    return [pl.BlockSpec((TD,), lambda i, k=k: (k * steps + step_of(i),), memory_space=pltpu.SMEM)
            for k in range(TOP_K)]


def _dispatch_call(pos_kmajor, pad_slots, v3, n_slots):
    n = v3.shape[0]
    slab = v3.shape[1:]
    steps = n // TD
    n_pad_blocks = pad_slots.shape[0] // PAD_BLOCK
    assert n_pad_blocks <= steps
    return pl.pallas_call(
        functools.partial(_dispatch_body, n_pad_blocks),
        out_shape=jax.ShapeDtypeStruct((n_slots,) + slab, F32),
        grid=(steps,),
        in_specs=_pos_specs(steps, lambda i: i) + [
            pl.BlockSpec((PAD_BLOCK,), lambda i: (jnp.minimum(i, n_pad_blocks - 1),), memory_space=pltpu.SMEM),
            pl.BlockSpec((1,) + slab, lambda i: (0, 0, 0)),
            pl.BlockSpec(memory_space=pl.ANY)],
        out_specs=pl.BlockSpec(memory_space=pl.ANY),
        scratch_shapes=[pltpu.VMEM((DISPATCH_BUFS, TD) + slab, F32),
                        pltpu.SemaphoreType.DMA((DISPATCH_BUFS,)), pltpu.SemaphoreType.DMA((DISPATCH_BUFS,))],
        compiler_params=pltpu.CompilerParams(dimension_semantics=("arbitrary",)),
        name="dispatch",
    )(*([pos_kmajor] * TOP_K), pad_slots, jnp.zeros((1,) + slab, F32), v3)


def _expert_body(te_ref, blk_ref, nvalid_ref, slot_ref, nxt_ref,
                 xs_ref, w1_hbm, b1_ref, w2_hbm, b2_ref, y_ref, w1f_s, w2f_s, w1b_s, w2b_s, sem):
    j = pl.program_id(0)
    n_exp = w1_hbm.shape[0]
    ff = w2_hbm.shape[1]
    e = te_ref[j]
    prev = te_ref[jnp.maximum(j - 1, 0)]

    def weight_copies(expert, slot):
        return (pltpu.make_async_copy(w1_hbm.at[expert], w1f_s.at[slot], sem.at[slot, 0]),
                pltpu.make_async_copy(w2_hbm.at[expert], w2f_s.at[slot], sem.at[slot, 1]))

    @pl.when(j == 0)
    def _():
        for c in weight_copies(e, slot_ref[e]):
            c.start()

    @pl.when((j == 0) | (e != prev))
    def _():
        slot = slot_ref[e]
        for c in weight_copies(e, slot):
            c.wait()
        w1b_s[...] = w1f_s[slot].astype(BF16)
        w2b_s[...] = w2f_s[slot].astype(BF16)

        @pl.when(nxt_ref[e] < n_exp)
        def _():
            for c in weight_copies(nxt_ref[e], 1 - slot):
                c.start()

    @pl.when(j < nvalid_ref[0])
    def _():
        tm = xs_ref.shape[0]
        x = xs_ref[...].reshape(tm, w1_hbm.shape[1])
        z = _dot(x.astype(BF16), w1b_s[...]) + b1_ref[0]
        glu = jnp.minimum(z[:, :ff], SWIGLU_LIMIT)
        lin = jnp.clip(z[:, ff:], -SWIGLU_LIMIT, SWIGLU_LIMIT)
        act = glu * _sigmoid(SWIGLU_ALPHA * glu) * (lin + 1.0)
        y = _dot(act.astype(BF16), w2b_s[...]) + b2_ref[0]
        y_ref[...] = y.reshape(y_ref.shape)

    @pl.when(j >= nvalid_ref[0])
    def _():
        y_ref[...] = jnp.zeros_like(y_ref)


def _expert_call(te, blk, nvalid, slot, nxt, xs, w1, b1, w2, b2):
    n_slots = xs.shape[0]
    slab = xs.shape[1:]
    n_exp, d, ff2 = w1.shape
    ff = w2.shape[1]
    grid_spec = pltpu.PrefetchScalarGridSpec(
        num_scalar_prefetch=5,
        grid=(n_slots // TM,),
        in_specs=[pl.BlockSpec((TM,) + slab, lambda j, te, blk, *_: (blk[j], 0, 0)),
                  pl.BlockSpec(memory_space=pl.ANY),
                  pl.BlockSpec((1, 1, ff2), lambda j, te, *_: (te[j], 0, 0)),
                  pl.BlockSpec(memory_space=pl.ANY),
                  pl.BlockSpec((1, 1, d), lambda j, te, *_: (te[j], 0, 0))],
        out_specs=pl.BlockSpec((TM,) + slab, lambda j, *_: (j, 0, 0)),
        scratch_shapes=[pltpu.VMEM((2, d, ff2), F32), pltpu.VMEM((2, ff, d), F32),
                        pltpu.VMEM((d, ff2), BF16), pltpu.VMEM((ff, d), BF16),
                        pltpu.SemaphoreType.DMA((2, 2))],
    )
    return pl.pallas_call(
        _expert_body,
        out_shape=jax.ShapeDtypeStruct((n_slots,) + slab, F32),
        grid_spec=grid_spec,
        compiler_params=pltpu.CompilerParams(
            dimension_semantics=("arbitrary",), vmem_limit_bytes=VMEM_LIMIT),
        name="experts",
    )(te, blk, nvalid, slot, nxt, xs, w1, b1, w2, b2)


def _combine_body(*refs):
    pos_refs, posn_refs = refs[:TOP_K], refs[TOP_K:2 * TOP_K]
    wts_ref, xmid_ref, g2_ref, npost_ref, y_hbm, o_ref, buf, sem = refs[2 * TOP_K:]
    i = pl.program_id(0)
    cur = lax.rem(i, 2)

    def row_copy(p_refs, slot, t, k):
        return pltpu.make_async_copy(y_hbm.at[p_refs[k][t]], buf.at[slot, k, t], sem.at[slot])

    def issue(p_refs, slot):
        def body(t, carry):
            for k in range(TOP_K):
                row_copy(p_refs, slot, t, k).start(priority=k % DMA_PRIORITIES)
            return carry

        lax.fori_loop(0, TD, body, 0, unroll=ISSUE_UNROLL)

    @pl.when(i == 0)
    def _():
        issue(pos_refs, 0)

    @pl.when(i + 1 < pl.num_programs(0))
    def _():
        issue(posn_refs, 1 - cur)

    def drain(t, carry):
        for k in range(TOP_K):
            row_copy(pos_refs, cur, t, k).wait()
        return carry

    lax.fori_loop(0, TD, drain, 0, unroll=ISSUE_UNROLL)

    w = wts_ref[...]
    f = w[:, 0:1] * buf[cur, 0].reshape(xmid_ref.shape)
    for k in range(1, TOP_K):
        f = f + w[:, k:k + 1] * buf[cur, k].reshape(xmid_ref.shape)
    o_ref[...] = xmid_ref[...] + g2_ref[0] * _rmsnorm(f, npost_ref[...])


def _combine_call(pos_kmajor, wts, xmid2, g2, npost, ys, t_per_batch):
    n, d = xmid2.shape
    steps = n // TD
    return pl.pallas_call(
        _combine_body,
        out_shape=jax.ShapeDtypeStruct((n, d), F32),
        grid=(steps,),
        in_specs=_pos_specs(steps, lambda i: i) + _pos_specs(steps, lambda i: jnp.minimum(i + 1, steps - 1)) + [
                  pl.BlockSpec((TD, LANES), lambda i: (i, 0)),
                  pl.BlockSpec((TD, d), lambda i: (i, 0)),
                  pl.BlockSpec((1, 1, d), lambda i: ((i * TD) // t_per_batch, 0, 0)),
                  pl.BlockSpec((1, d), lambda i: (0, 0)),
                  pl.BlockSpec(memory_space=pl.ANY)],
        out_specs=pl.BlockSpec((TD, d), lambda i: (i, 0)),
        scratch_shapes=[pltpu.VMEM((2, TOP_K, TD) + ys.shape[1:], F32), pltpu.SemaphoreType.DMA((2,))],
        compiler_params=pltpu.CompilerParams(dimension_semantics=("arbitrary",)),
        name="combine",
    )(*([pos_kmajor] * (2 * TOP_K)), wts, xmid2, g2, npost, ys)


def _count_le(ends, q):
    return jnp.sum((ends[None, :] <= q[:, None]).astype(jnp.int32), axis=1)


def _gate_weights(wa, wx):
    return jnp.concatenate([wa, wx], axis=-1).astype(BF16)


def kernel(x, c, ctx, c_ctx, w_mod, b_mod, norm_pre_mix, norm_post_mix, norm_pre_ffn, norm_post_ffn, w_in, conv_w, conv_b, lru_wa, lru_ba, lru_wx, lru_bx, lru_lambda, pool_w, pool_scale, w_rnn_proj, w_pool_proj, w_out, router_w, router_b, exp_w1, exp_b1, exp_w2, exp_b2):
    bsz, t, d = x.shape
    assert w_mod.shape[0] == 1, "single-layer block"
    assert t % TT == 0 and TT % POOL_CHUNK == 0 and POOL_CHUNK % GRID_W == 0 and t % TD == 0
    assert d == SUBLANES * LANES, "a token row must be exactly one (8, 128) slab"
    n = bsz * t
    n_exp = router_w.shape[-1]
    assert (n * TOP_K) % TM == 0 and (n_exp * TM) % PAD_BLOCK == 0 and n <= RANK_RADIX
    row = lambda a: a.reshape(1, -1)

    pad = jnp.zeros((SUBLANES - (bsz + 1) % SUBLANES, d), F32)
    cc = jnp.concatenate([c, c_ctx[None, :], pad], axis=0)
    mod = _mod_call(cc, w_mod[0], row(b_mod[0]))
    lat = lambda j: mod[:bsz, j * d:(j + 1) * d].reshape(bsz, 1, d)
    sh1, sc1, g1, sh2, sc2, g2 = (lat(j) for j in range(6))
    csh1, csc1 = mod[bsz:bsz + 1, 0:d], mod[bsz:bsz + 1, d:2 * d]

    wgf = _gate_weights(lru_wa[0, 0], lru_wx[0, 0])
    wgb = _gate_weights(lru_wa[0, 1], lru_wx[0, 1])
    ba, bx, lam = lru_ba[0], lru_bx[0], lru_lambda[0]
    cw, cb = conv_w[0], row(conv_b[0])
    npre_mix = row(norm_pre_mix[0])

    h0f, h0b = _ctx_call(ctx, npre_mix, csh1, csc1, w_in[0], cw, cb, wgf, wgb, ba, bx, lam)

    hf, xc, gg, xp, mr, mp = _mix_fwd_call(x, npre_mix, sh1, sc1, w_in[0], cw, cb, wgf, ba, bx, lam, h0f)

    pm, pc = _pool_consts()
    xmid, v, code_o, wts_o, cnt = _mix_bwd_call(
        hf, xc, gg, xp, mr, mp, x, wgb, ba, bx, lam, h0b,
        pm, pc, pool_w[0].astype(BF16), row(pool_scale[0]),
        w_rnn_proj[0], w_pool_proj[0], w_out[0],
        row(norm_post_mix[0]), row(norm_pre_ffn[0]), g1, sh2, sc2, router_w[0], row(router_b[0]))

    code = code_o.reshape(n, LANES)[:, :TOP_K].T
    idx, rank = code // RANK_RADIX, code % RANK_RADIX
    counts = cnt[0].astype(jnp.int32)
    tiles_e = (counts + TM - 1) // TM
    tile_end = jnp.cumsum(tiles_e)
    offs = (tile_end - tiles_e) * TM
    e_ids = jnp.arange(n_exp, dtype=jnp.int32)
    off_of = jnp.sum(jnp.where(idx[None] == e_ids[:, None, None], offs[:, None, None], 0), axis=0)
    pos_kmajor = (off_of + rank).reshape(TOP_K * n)
    n_tiles = (n * TOP_K) // TM + n_exp
    n_valid = tile_end[-1]
    jj = jnp.arange(n_tiles, dtype=jnp.int32)
    blk = jnp.minimum(jj, n_valid - 1)
    te = jnp.minimum(_count_le(tile_end, blk), n_exp - 1)

    pad_e = tiles_e * TM - counts
    pad_end = jnp.cumsum(pad_e)
    qq = jnp.arange(n_exp * TM, dtype=jnp.int32)
    pad_start = pad_end - pad_e
    inside = (qq[None, :] >= pad_start[:, None]) & (qq[None, :] < pad_end[:, None])
    slot_in = jnp.sum(jnp.where(inside, (offs + counts - pad_start)[:, None] + qq[None, :], 0), axis=0)
    slot_tail = n_valid * TM + qq - pad_end[-1]
    pad_slots = jnp.where(qq < pad_end[-1], slot_in, slot_tail).astype(jnp.int32)

    xs = _dispatch_call(pos_kmajor, pad_slots, v.reshape((n,) + v.shape[2:]), n_tiles * TM)
    has = tiles_e > 0
    w_slot = ((jnp.cumsum(has.astype(jnp.int32)) - has.astype(jnp.int32)) % 2).astype(jnp.int32)
    later = (e_ids[None, :] > e_ids[:, None]) & has[None, :]
    w_next = jnp.min(jnp.where(later, e_ids[None, :], n_exp), axis=1).astype(jnp.int32)
    ys = _expert_call(te, blk, n_valid.reshape(1), w_slot, w_next, xs, exp_w1[0],
                      exp_b1[0].reshape(n_exp, 1, -1), exp_w2[0], exp_b2[0].reshape(n_exp, 1, -1))
    out = _combine_call(pos_kmajor, wts_o.reshape(n, LANES), xmid.reshape(n, d), g2, row(norm_post_ffn[0]), ys, t)
    return out.reshape(bsz, t, d)
```

```python
import functools

import numpy as np
import jax
import jax.numpy as jnp
from jax import lax
from jax.experimental import pallas as pl
from jax.experimental.pallas import tpu as pltpu

F32 = jnp.float32
BF16 = jnp.bfloat16

RNN_HEADS = 4
CONV_W = 4
LRU_C = 8.0
POOL_WINDOWS = (2, 4, 8, 16)
GRID_W = 64
TOP_K = 4
SWIGLU_LIMIT = 7.0
SWIGLU_ALPHA = 1.702
EPS = 1e-6
SQRT_FLOOR = 1e-30

SUBLANES = 8
LANES = 128
VMEM_LIMIT = 56 * 1024 * 1024

TT = 256
POOL_CHUNK = 256
TM = 256
TD = 128
INVERT_BLOCK = 2048
ISSUE_UNROLL = 8
DMA_PRIORITIES = 2
RANK_RADIX = 1 << 16


def _sigmoid(x):
    return 1.0 / (1.0 + jnp.exp(-x))


def _softplus(z):
    return jnp.maximum(z, 0.0) + jnp.log1p(jnp.exp(-jnp.abs(z)))


def _sqrt_nonneg(x):
    return x * lax.rsqrt(jnp.maximum(x, SQRT_FLOOR))


def _rmsnorm(x, g):
    ms = jnp.mean(x * x, axis=-1, keepdims=True)
    return (x * lax.rsqrt(ms + EPS)) * g


def _modulate(u, shift, scale):
    return u * (1.0 + scale) + shift


def _dot(a, b):
    return jnp.dot(a, b, preferred_element_type=F32)


def _split(x):
    hi = x.astype(BF16)
    return hi, (x - hi.astype(F32)).astype(BF16)


def _dot3(a, b):
    ah, al = _split(a)
    bh, bl = _split(b)
    return _dot(ah, bh) + (_dot(al, bh) + _dot(ah, bl))


def _load_cast(w_hbm, dst_s, tmp_s, sem):
    c = tmp_s.shape[1]
    for j in range(w_hbm.shape[1] // c):
        cp = pltpu.make_async_copy(w_hbm.at[:, pl.ds(j * c, c)], tmp_s, sem)
        cp.start()
        cp.wait()
        dst_s[:, j * c:(j + 1) * c] = tmp_s[...].astype(BF16)


def _conv(prev8, xr, next8, cw, cb):
    t = xr.shape[0]
    ext = jnp.concatenate([prev8, xr, next8], axis=0)
    acc = cb + cw[0:1, :] * ext[SUBLANES - 2:SUBLANES - 2 + t, :]
    for k in range(1, CONV_W):
        off = SUBLANES - 2 + k
        acc = acc + cw[k:k + 1, :] * ext[off:off + t, :]
    return acc


def _lru_coeffs(xc, wg_ref, ba, bx, sp):
    hd = xc.shape[1] // RNN_HEADS
    xcb = xc.astype(BF16)
    a_parts, b_parts = [], []
    for h in range(RNN_HEADS):
        cols = slice(h * hd, (h + 1) * hd)
        z = _dot(xcb[:, cols], wg_ref[h])
        r = _sigmoid(z[:, :hd] + ba[:, cols])
        i = _sigmoid(z[:, hd:] + bx[:, cols])
        a = jnp.exp((-LRU_C) * r * sp[:, cols])
        b = _sqrt_nonneg(1.0 - a * a) * (i * xc[:, cols])
        a_parts.append(a)
        b_parts.append(b)
    return jnp.concatenate(a_parts, axis=1), jnp.concatenate(b_parts, axis=1)


def _scan_tile(a, b, h_in, reverse, store):
    t, c = a.shape
    g = t // SUBLANES
    a3 = a.reshape(g, SUBLANES, c)
    b3 = b.reshape(g, SUBLANES, c)
    row = lax.broadcasted_iota(jnp.int32, (g, SUBLANES, c), 1)
    for s in (1, 2, 4):
        if reverse:
            shift, m = SUBLANES - s, row < SUBLANES - s
        else:
            shift, m = s, row >= s
        ra = pltpu.roll(a3, shift, axis=1)
        rb = pltpu.roll(b3, shift, axis=1)
        b3 = a3 * jnp.where(m, rb, 0.0) + b3
        a3 = a3 * jnp.where(m, ra, 1.0)
    h = h_in
    order = range(g - 1, -1, -1) if reverse else range(g)
    for gi in order:
        hg = a3[gi] * h + b3[gi]
        store(gi, hg)
        h = hg[0:1, :] if reverse else hg[SUBLANES - 1:SUBLANES, :]
    return h


def _mod_body(c_ref, w_ref, b_ref, o_ref):
    c = c_ref[...]
    s = c * _sigmoid(c)
    o_ref[...] = _dot3(s, w_ref[...]) + b_ref[...]


def _mod_call(cc, w_mod, b_mod):
    d = cc.shape[1]
    n = w_mod.shape[1] // d
    return pl.pallas_call(
        _mod_body,
        out_shape=jax.ShapeDtypeStruct((cc.shape[0], n * d), F32),
        grid=(n,),
        in_specs=[pl.BlockSpec(cc.shape, lambda j: (0, 0)),
                  pl.BlockSpec((d, d), lambda j: (0, j)),
                  pl.BlockSpec((1, d), lambda j: (0, j))],
        out_specs=pl.BlockSpec((cc.shape[0], d), lambda j: (0, j)),
        name="mod",
    )(cc, w_mod, b_mod)


def _ctx_body(ctx_ref, g_ref, sh_ref, sc_ref, w_ref, cw_ref, cb_ref, wgf_ref, wgb_ref,
              ba_ref, bx_ref, lam_ref, hf_ref, hb_ref):
    d = ctx_ref.shape[2]
    u = _modulate(_rmsnorm(ctx_ref[0], g_ref[...]), sh_ref[...], sc_ref[...])
    xr = _dot(u.astype(BF16), w_ref[...].astype(BF16))
    z8 = jnp.zeros((SUBLANES, d), F32)
    xc = _conv(z8, xr, z8, cw_ref[...], cb_ref[...])
    h0 = jnp.zeros((1, d), F32)
    for di, (wg_ref, out_ref) in enumerate(((wgf_ref, hf_ref), (wgb_ref, hb_ref))):
        sp = _softplus(-lam_ref[di:di + 1, :])
        a, b = _lru_coeffs(xc, wg_ref, ba_ref[di:di + 1, :], bx_ref[di:di + 1, :], sp)
        out_ref[0] = _scan_tile(a, b, h0, di == 1, lambda gi, hg: None)


def _ctx_call(ctx, g, sh, sc, w_in, cw, cb, wgf, wgb, ba, bx, lam):
    bsz, tc, d = ctx.shape
    full = lambda shp: pl.BlockSpec(shp, lambda b: (0,) * len(shp))
    return pl.pallas_call(
        _ctx_body,
        out_shape=(jax.ShapeDtypeStruct((bsz, 1, d), F32),) * 2,
        grid=(bsz,),
        in_specs=[pl.BlockSpec((1, tc, d), lambda b: (b, 0, 0)),
                  full((1, d)), full((1, d)), full((1, d)),
                  pl.BlockSpec((d, d), lambda b: (0, 0)),
                  full(cw.shape), full((1, d)), full(wgf.shape), full(wgb.shape),
                  full(ba.shape), full(bx.shape), full(lam.shape)],
        out_specs=(pl.BlockSpec((1, 1, d), lambda b: (b, 0, 0)),) * 2,
        compiler_params=pltpu.CompilerParams(vmem_limit_bytes=VMEM_LIMIT),
        name="ctx",
    )(ctx, g, sh, sc, w_in, cw, cb, wgf, wgb, ba, bx, lam)


def _mix_fwd_body(x_ref, xn_ref, g_ref, sh_ref, sc_ref, win_ref, cw_ref, cb_ref, wgf_ref,
                  ba_ref, bx_ref, lam_ref, h0_ref,
                  hf_ref, xc_ref, gg_ref, xp_ref, mr_ref, mp_ref, tail_s, h_s, win_s, wtmp_s, wsem):
    i = pl.program_id(1)
    last = pl.num_programs(1) - 1
    tt, d = x_ref.shape[1], x_ref.shape[2]

    @pl.when((i == 0) & (pl.program_id(0) == 0))
    def _():
        _load_cast(win_ref, win_s, wtmp_s, wsem)

    @pl.when(i == 0)
    def _():
        tail_s[...] = jnp.zeros_like(tail_s)
        h_s[...] = h0_ref[0]

    g, sh, sc = g_ref[...], sh_ref[0], sc_ref[0]
    ub = _modulate(_rmsnorm(x_ref[0], g), sh, sc).astype(BF16)
    unb = _modulate(_rmsnorm(xn_ref[0], g), sh, sc).astype(BF16)
    w_rnn = win_s[:, 0:d]
    xr = _dot(ub, w_rnn)
    xrn = jnp.where(i == last, 0.0, _dot(unb, w_rnn))
    xc = _conv(tail_s[...], xr, xrn, cw_ref[...], cb_ref[...])
    tail_s[...] = xr[tt - SUBLANES:tt, :]
    xc_ref[0] = xc.astype(BF16)
    for j, ref in enumerate((gg_ref, xp_ref, mr_ref, mp_ref)):
        ref[0] = _dot(ub, win_s[:, (j + 1) * d:(j + 2) * d]).astype(BF16)

    sp = _softplus(-lam_ref[0:1, :])
    a, b = _lru_coeffs(xc, wgf_ref, ba_ref[0:1, :], bx_ref[0:1, :], sp)

    def store(gi, hg):
        hf_ref[0, gi * SUBLANES:(gi + 1) * SUBLANES, :] = hg

    h_s[...] = _scan_tile(a, b, h_s[...], False, store)


def _mix_fwd_call(x, g, sh, sc, w_in, cw, cb, wgf, ba, bx, lam, h0f):
    bsz, t, d = x.shape
    nt = t // TT
    nblk8 = t // SUBLANES
    per_b = pl.BlockSpec((1, 1, d), lambda b, i: (b, 0, 0))
    full = lambda shp: pl.BlockSpec(shp, lambda b, i: (0,) * len(shp))
    tile = pl.BlockSpec((1, TT, d), lambda b, i: (b, i, 0))
    return pl.pallas_call(
        _mix_fwd_body,
        out_shape=(jax.ShapeDtypeStruct((bsz, t, d), F32),) + (jax.ShapeDtypeStruct((bsz, t, d), BF16),) * 5,
        grid=(bsz, nt),
        in_specs=[tile,
                  pl.BlockSpec((1, SUBLANES, d),
                               lambda b, i: (b, jnp.minimum((i + 1) * (TT // SUBLANES), nblk8 - 1), 0)),
                  full((1, d)), per_b, per_b,
                  pl.BlockSpec(memory_space=pl.ANY), full(cw.shape), full((1, d)), full(wgf.shape),
                  full(ba.shape), full(bx.shape), full(lam.shape), per_b],
        out_specs=(tile,) * 6,
        scratch_shapes=[pltpu.VMEM((SUBLANES, d), F32), pltpu.VMEM((1, d), F32),
                        pltpu.VMEM(w_in.shape, BF16), pltpu.VMEM((d, d), F32), pltpu.SemaphoreType.DMA],
        compiler_params=pltpu.CompilerParams(
            dimension_semantics=("arbitrary", "arbitrary"), vmem_limit_bytes=VMEM_LIMIT),
        name="mix_fwd",
    )(x, x, g, sh, sc, w_in, cw, cb, wgf, ba, bx, lam, h0f)


def _pool_consts():
    p = np.arange(POOL_CHUNK)
    pos, line = p % GRID_W, p // GRID_W
    mats, cnts = [], []
    for w in POOL_WINDOWS:
        lo = np.clip(pos - w // 2, 0, GRID_W)
        hi = np.clip(pos + w - w // 2, 0, GRID_W)
        m = (line[:, None] == line[None, :]) & (pos[None, :] >= lo[:, None]) & (pos[None, :] < hi[:, None])
        mats.append(m.astype(np.float32))
        cnts.append((hi - lo).astype(np.float32)[:, None])
    return jnp.asarray(np.stack(mats), BF16), jnp.asarray(np.stack(cnts), F32)


def _mix_bwd_body(hf_ref, xc_ref, gg_ref, xp_ref, mr_ref, mp_ref, x_ref,
                  wgb_ref, ba_ref, bx_ref, lam_ref, h0_ref,
                  pm_ref, pc_ref, pw_ref, ps_ref, wr_ref, wp_ref, wo_ref,
                  npost_ref, npre_ref, g1_ref, sh2_ref, sc2_ref, rw_ref, rb_ref,
                  xmid_ref, v_ref, code_ref, wts_ref, cnt_ref,
                  hb_s, h_s, cnt_s, w3_s, wtmp_s, wsem):
    b_id, i = pl.program_id(0), pl.program_id(1)
    tt, d = x_ref.shape[1], x_ref.shape[2]
    n_exp = rw_ref.shape[1]

    @pl.when((i == 0) & (b_id == 0))
    def _():
        for k, w_hbm in enumerate((wr_ref, wp_ref, wo_ref)):
            _load_cast(w_hbm, w3_s.at[k], wtmp_s, wsem)

    @pl.when(i == 0)
    def _():
        h_s[...] = h0_ref[0]

    @pl.when((i == 0) & (b_id == 0))
    def _():
        cnt_s[...] = jnp.zeros_like(cnt_s)

    xc = xc_ref[0].astype(F32)
    sp = _softplus(-lam_ref[1:2, :])
    a, b = _lru_coeffs(xc, wgb_ref, ba_ref[1:2, :], bx_ref[1:2, :], sp)

    def store(gi, hg):
        hb_s[gi * SUBLANES:(gi + 1) * SUBLANES, :] = hg

    h_s[...] = _scan_tile(a, b, h_s[...], True, store)

    gg = gg_ref[0].astype(F32)
    gelu = gg * _sigmoid(gg * (1.5957691216057308 + 0.07135481627260025 * (gg * gg)))
    y_rnn = (hf_ref[0] + hb_s[...]) * gelu

    xpb = xp_ref[0]
    grp = d // len(POOL_WINDOWS)
    y_parts = []
    for gi in range(len(POOL_WINDOWS)):
        cols = slice(gi * grp, (gi + 1) * grp)
        rows = []
        for c0 in range(0, tt, POOL_CHUNK):
            xg = xpb[c0:c0 + POOL_CHUNK, cols]
            mean = _dot(pm_ref[gi], xg) / pc_ref[gi]
            rows.append((mean - xg.astype(F32)).astype(BF16))
        dg = rows[0] if len(rows) == 1 else jnp.concatenate(rows, axis=0)
        y_parts.append(_dot(dg, pw_ref[gi]) * ps_ref[:, cols])
    y_pool = jnp.concatenate(y_parts, axis=1)

    merged = (_sigmoid(mr_ref[0].astype(F32)) * _dot(y_rnn.astype(BF16), w3_s[0])
              + _sigmoid(mp_ref[0].astype(F32)) * _dot(y_pool.astype(BF16), w3_s[1]))
    m_lat = _dot(merged.astype(BF16), w3_s[2])
    x_mid = x_ref[0] + g1_ref[0] * _rmsnorm(m_lat, npost_ref[...])
    xmid_ref[0] = x_mid
    v = _modulate(_rmsnorm(x_mid, npre_ref[...]), sh2_ref[0], sc2_ref[0])
    v_ref[0] = v.reshape(tt, SUBLANES, LANES)

    logits = _dot3(v, rw_ref[...]) + rb_ref[...]
    lane = lax.broadcasted_iota(jnp.int32, (tt, n_exp), 1).astype(F32)
    work = logits
    vals, idxs, sels = [], [], []
    for _ in range(TOP_K):
        m = jnp.max(work, axis=-1, keepdims=True)
        idx = jnp.min(jnp.where(work == m, lane, float(n_exp)), axis=-1, keepdims=True)
        sel = lane == idx
        vals.append(m)
        idxs.append(idx)
        sels.append(sel)
        work = jnp.where(sel, -jnp.inf, work)
    exps = [jnp.exp(vk - vals[0]) for vk in vals]
    den = exps[0] + exps[1] + exps[2] + exps[3]
    anyf = jnp.zeros((tt, n_exp), F32)
    for sel in sels:
        anyf = anyf + sel.astype(F32)
    r_i = lax.broadcasted_iota(jnp.int32, (tt, tt), 0)
    c_i = lax.broadcasted_iota(jnp.int32, (tt, tt), 1)
    lower = (c_i < r_i).astype(BF16)
    before = _dot(lower, anyf.astype(BF16)) + cnt_s[...]
    cnt_new = cnt_s[...] + jnp.sum(anyf, axis=0, keepdims=True)
    cnt_s[...] = cnt_new
    cnt_ref[...] = cnt_new

    lane_o = lax.broadcasted_iota(jnp.int32, (tt, LANES), 1)
    code_o = jnp.zeros((tt, LANES), jnp.int32)
    wts_o = jnp.zeros((tt, LANES), F32)
    for k in range(TOP_K):
        rk = jnp.sum(jnp.where(sels[k], before, 0.0), axis=-1, keepdims=True)
        code = idxs[k].astype(jnp.int32) * RANK_RADIX + rk.astype(jnp.int32)
        code_o = jnp.where(lane_o == k, code, code_o)
        wts_o = jnp.where(lane_o == k, exps[k] / den, wts_o)
    code_ref[0] = code_o
    wts_ref[0] = wts_o


def _mix_bwd_call(hf, xc, gg, xp, mr, mp, x, wgb, ba, bx, lam, h0b, pm, pc, pw, ps, wr, wp, wo,
                  npost, npre, g1, sh2, sc2, rw, rb):
    bsz, t, d = x.shape
    nt = t // TT
    n_exp = rw.shape[1]
    rev = lambda b, i: (b, nt - 1 - i, 0)
    tile = pl.BlockSpec((1, TT, d), rev)
    small = pl.BlockSpec((1, TT, LANES), rev)
    per_b = pl.BlockSpec((1, 1, d), lambda b, i: (b, 0, 0))
    full = lambda shp: pl.BlockSpec(shp, lambda b, i: (0,) * len(shp))
    return pl.pallas_call(
        _mix_bwd_body,
        out_shape=(jax.ShapeDtypeStruct((bsz, t, d), F32),
                   jax.ShapeDtypeStruct((bsz, t, SUBLANES, LANES), F32),
                   jax.ShapeDtypeStruct((bsz, t, LANES), jnp.int32),
                   jax.ShapeDtypeStruct((bsz, t, LANES), F32),
                   jax.ShapeDtypeStruct((1, n_exp), F32)),
        grid=(bsz, nt),
        in_specs=[tile] * 7 + [full(wgb.shape), full(ba.shape), full(bx.shape), full(lam.shape), per_b,
                               full(pm.shape), full(pc.shape), full(pw.shape), full((1, d)),
                               pl.BlockSpec(memory_space=pl.ANY), pl.BlockSpec(memory_space=pl.ANY),
                               pl.BlockSpec(memory_space=pl.ANY),
                               full((1, d)), full((1, d)), per_b, per_b, per_b,
                               full(rw.shape), full((1, n_exp))],
        out_specs=(tile, pl.BlockSpec((1, TT, SUBLANES, LANES), lambda b, i: (b, nt - 1 - i, 0, 0)),
                   small, small, full((1, n_exp))),
        scratch_shapes=[pltpu.VMEM((TT, d), F32), pltpu.VMEM((1, d), F32), pltpu.VMEM((1, n_exp), F32),
                        pltpu.VMEM((3, d, d), BF16), pltpu.VMEM((d, d), F32), pltpu.SemaphoreType.DMA],
        compiler_params=pltpu.CompilerParams(
            dimension_semantics=("arbitrary", "arbitrary"), vmem_limit_bytes=VMEM_LIMIT),
        name="mix_bwd",
    )(hf, xc, gg, xp, mr, mp, x, wgb, ba, bx, lam, h0b, pm, pc, pw, ps, wr, wp, wo,
      npost, npre, g1, sh2, sc2, rw, rb)


def _invert_body(n_tok, pos_ref, pad_ref, inv_ref):
    i = pl.program_id(0)
    nb = pos_ref.shape[0]
    base = i * nb

    def body(q, carry):
        inv_ref[pos_ref[q]] = base + q
        return carry

    lax.fori_loop(0, nb, body, 0, unroll=ISSUE_UNROLL)

    def body_pad(q, carry):
        p = pad_ref[q]
        inv_ref[p] = TOP_K * n_tok + lax.rem(p, TM)
        return carry

    lax.fori_loop(0, pad_ref.shape[0], body_pad, 0, unroll=ISSUE_UNROLL)


def _invert_call(pos_kmajor, pad_slots, n_tok, n_slots):
    steps = pos_kmajor.shape[0] // INVERT_BLOCK
    pad_block = pad_slots.shape[0] // steps
    return pl.pallas_call(
        functools.partial(_invert_body, n_tok),
        out_shape=jax.ShapeDtypeStruct((n_slots,), jnp.int32),
        grid=(steps,),
        in_specs=[pl.BlockSpec((INVERT_BLOCK,), lambda i: (i,), memory_space=pltpu.SMEM),
                  pl.BlockSpec((pad_block,), lambda i: (i,), memory_space=pltpu.SMEM)],
        out_specs=pl.BlockSpec(memory_space=pltpu.SMEM),
        compiler_params=pltpu.CompilerParams(dimension_semantics=("arbitrary",)),
        name="invert",
    )(pos_kmajor, pad_slots)


def _expert_body(te_ref, nvalid_ref, slot_ref, nxt_ref,
                 gcur_ref, gnxt_ref, sprev_ref, scur_ref,
                 v_hbm, w1_hbm, b1_ref, w2_hbm, b2_ref, y_hbm,
                 w1f_s, w2f_s, w1b_s, w2b_s, xbuf, ybuf, wsem, gsem, ssem):
    j = pl.program_id(0)
    n_exp = w1_hbm.shape[0]
    ff = w2_hbm.shape[1]
    tm = xbuf.shape[1]
    d = w1_hbm.shape[1]
    n_valid = nvalid_ref[0]
    e = te_ref[j]
    prev = te_ref[jnp.maximum(j - 1, 0)]
    cur = lax.rem(j, 2)
    nxt = 1 - cur

    def weight_copies(expert, slot):
        return (pltpu.make_async_copy(w1_hbm.at[expert], w1f_s.at[slot], wsem.at[slot, 0]),
                pltpu.make_async_copy(w2_hbm.at[expert], w2f_s.at[slot], wsem.at[slot, 1]))

    def gather(idx_ref, buf, r):
        return pltpu.make_async_copy(v_hbm.at[idx_ref[r]], xbuf.at[buf, r], gsem.at[buf])

    def scatter(idx_ref, buf, r):
        return pltpu.make_async_copy(ybuf.at[buf, r], y_hbm.at[idx_ref[r]], ssem.at[buf])

    @pl.when(j == 0)
    def _():
        for c in weight_copies(e, slot_ref[e]):
            c.start()
        for r in range(tm):
            gather(gcur_ref, 0, r).start(priority=r % DMA_PRIORITIES)
        for r in range(tm):
            gather(gcur_ref, 0, r).wait()
        ybuf[...] = jnp.zeros_like(ybuf)
        dump = pltpu.make_async_copy(ybuf.at[0], y_hbm.at[pl.ds(y_hbm.shape[0] - tm, tm)], ssem.at[0])
        dump.start()
        dump.wait()

    @pl.when((j == 0) | (e != prev))
    def _():
        slot = slot_ref[e]
        for c in weight_copies(e, slot):
            c.wait()
        w1b_s[...] = w1f_s[slot].astype(BF16)
        w2b_s[...] = w2f_s[slot].astype(BF16)

        @pl.when(nxt_ref[e] < n_exp)
        def _():
            for c in weight_copies(nxt_ref[e], 1 - slot):
                c.start()

    def tile(is_last):
        for r in range(tm):
            scatter(sprev_ref, nxt, r).start(priority=r % DMA_PRIORITIES)
        x = xbuf[cur].reshape(tm, d)
        z = _dot(x.astype(BF16), w1b_s[...]) + b1_ref[0]
        if not is_last:
            for r in range(tm):
                gather(gnxt_ref, nxt, r).start(priority=r % DMA_PRIORITIES)
        glu = jnp.minimum(z[:, :ff], SWIGLU_LIMIT)
        lin = jnp.clip(z[:, ff:], -SWIGLU_LIMIT, SWIGLU_LIMIT)
        act = glu * _sigmoid(SWIGLU_ALPHA * glu) * (lin + 1.0)
        y = _dot(act.astype(BF16), w2b_s[...]) + b2_ref[0]
        ybuf[cur] = y.reshape(ybuf.shape[1:])
        for r in range(tm):
            scatter(sprev_ref, nxt, r).wait()
        if not is_last:
            for r in range(tm):
                gather(gnxt_ref, nxt, r).wait()
        if is_last:
            for r in range(tm):
                scatter(scur_ref, cur, r).start(priority=r % DMA_PRIORITIES)
            for r in range(tm):
                scatter(scur_ref, cur, r).wait()

    @pl.when(j < n_valid - 1)
    def _():
        tile(False)

    @pl.when(j == n_valid - 1)
    def _():
        tile(True)


def _expert_call(te, nvalid, slot, nxt, gidx, sdst, v3, w1, b1, w2, b2, n_rows_out):
    n_slots = gidx.shape[0]
    slab = v3.shape[1:]
    n_exp, d, ff2 = w1.shape
    ff = w2.shape[1]
    n_tiles = n_slots // TM
    last = lambda nv: nv[0] - 1
    smem = lambda f: pl.BlockSpec((TM,), f, memory_space=pltpu.SMEM)
    grid_spec = pltpu.PrefetchScalarGridSpec(
        num_scalar_prefetch=4,
        grid=(n_tiles,),
        in_specs=[smem(lambda j, te, nv, *_: (jnp.minimum(j, last(nv)),)),
                  smem(lambda j, te, nv, *_: (jnp.minimum(j + 1, last(nv)),)),
                  smem(lambda j, te, nv, *_: (jnp.minimum(jnp.maximum(j - 1, 0), last(nv)),)),
                  smem(lambda j, te, nv, *_: (jnp.minimum(j, last(nv)),)),
                  pl.BlockSpec(memory_space=pl.ANY),
                  pl.BlockSpec(memory_space=pl.ANY),
                  pl.BlockSpec((1, 1, ff2), lambda j, te, *_: (te[j], 0, 0)),
                  pl.BlockSpec(memory_space=pl.ANY),
                  pl.BlockSpec((1, 1, d), lambda j, te, *_: (te[j], 0, 0))],
        out_specs=pl.BlockSpec(memory_space=pl.ANY),
        scratch_shapes=[pltpu.VMEM((2, d, ff2), F32), pltpu.VMEM((2, ff, d), F32),
                        pltpu.VMEM((d, ff2), BF16), pltpu.VMEM((ff, d), BF16),
                        pltpu.VMEM((2, TM) + slab, F32), pltpu.VMEM((2, TM) + slab, F32),
                        pltpu.SemaphoreType.DMA((2, 2)), pltpu.SemaphoreType.DMA((2,)),
                        pltpu.SemaphoreType.DMA((2,))],
    )
    return pl.pallas_call(
        _expert_body,
        out_shape=jax.ShapeDtypeStruct((n_rows_out,) + slab, F32),
        grid_spec=grid_spec,
        compiler_params=pltpu.CompilerParams(
            dimension_semantics=("arbitrary",), vmem_limit_bytes=VMEM_LIMIT),
        name="experts",
    )(te, nvalid, slot, nxt, gidx, gidx, sdst, sdst, v3, w1, b1, w2, b2)


def _combine_body(*refs):
    y_refs = refs[:TOP_K]
    wts_ref, xmid_ref, g2_ref, npost_ref, o_ref = refs[TOP_K:]
    w = wts_ref[...]
    f = w[:, 0:1] * y_refs[0][...].reshape(xmid_ref.shape)
    for k in range(1, TOP_K):
        f = f + w[:, k:k + 1] * y_refs[k][...].reshape(xmid_ref.shape)
    o_ref[...] = xmid_ref[...] + g2_ref[0] * _rmsnorm(f, npost_ref[...])


def _combine_call(y4, wts, xmid2, g2, npost, t_per_batch):
    n, d = xmid2.shape
    steps = n // TD
    slab = y4.shape[1:]
    return pl.pallas_call(
        _combine_body,
        out_shape=jax.ShapeDtypeStruct((n, d), F32),
        grid=(steps,),
        in_specs=[pl.BlockSpec((TD,) + slab, lambda i, k=k: (k * steps + i, 0, 0)) for k in range(TOP_K)] + [
                  pl.BlockSpec((TD, LANES), lambda i: (i, 0)),
                  pl.BlockSpec((TD, d), lambda i: (i, 0)),
                  pl.BlockSpec((1, 1, d), lambda i: ((i * TD) // t_per_batch, 0, 0)),
                  pl.BlockSpec((1, d), lambda i: (0, 0))],
        out_specs=pl.BlockSpec((TD, d), lambda i: (i, 0)),
        compiler_params=pltpu.CompilerParams(dimension_semantics=("arbitrary",)),
        name="combine",
    )(*([y4] * TOP_K), wts, xmid2, g2, npost)


def _count_le(ends, q):
    return jnp.sum((ends[None, :] <= q[:, None]).astype(jnp.int32), axis=1)


def _gate_weights(wa, wx):
    return jnp.concatenate([wa, wx], axis=-1).astype(BF16)


def kernel(x, c, ctx, c_ctx, w_mod, b_mod, norm_pre_mix, norm_post_mix, norm_pre_ffn, norm_post_ffn, w_in, conv_w, conv_b, lru_wa, lru_ba, lru_wx, lru_bx, lru_lambda, pool_w, pool_scale, w_rnn_proj, w_pool_proj, w_out, router_w, router_b, exp_w1, exp_b1, exp_w2, exp_b2):
    bsz, t, d = x.shape
    assert w_mod.shape[0] == 1, "single-layer block"
    assert t % TT == 0 and TT % POOL_CHUNK == 0 and POOL_CHUNK % GRID_W == 0 and t % TD == 0
    assert d == SUBLANES * LANES, "a token row must be exactly one (8, 128) slab"
    n = bsz * t
    n_exp = router_w.shape[-1]
    assert (n * TOP_K) % TM == 0 and (n * TOP_K) % INVERT_BLOCK == 0 and n <= RANK_RADIX
    row = lambda a: a.reshape(1, -1)

    pad = jnp.zeros((SUBLANES - (bsz + 1) % SUBLANES, d), F32)
    cc = jnp.concatenate([c, c_ctx[None, :], pad], axis=0)
    mod = _mod_call(cc, w_mod[0], row(b_mod[0]))
    lat = lambda j: mod[:bsz, j * d:(j + 1) * d].reshape(bsz, 1, d)
    sh1, sc1, g1, sh2, sc2, g2 = (lat(j) for j in range(6))
    csh1, csc1 = mod[bsz:bsz + 1, 0:d], mod[bsz:bsz + 1, d:2 * d]

    wgf = _gate_weights(lru_wa[0, 0], lru_wx[0, 0])
    wgb = _gate_weights(lru_wa[0, 1], lru_wx[0, 1])
    ba, bx, lam = lru_ba[0], lru_bx[0], lru_lambda[0]
    cw, cb = conv_w[0], row(conv_b[0])
    npre_mix = row(norm_pre_mix[0])

    h0f, h0b = _ctx_call(ctx, npre_mix, csh1, csc1, w_in[0], cw, cb, wgf, wgb, ba, bx, lam)

    hf, xc, gg, xp, mr, mp = _mix_fwd_call(x, npre_mix, sh1, sc1, w_in[0], cw, cb, wgf, ba, bx, lam, h0f)

    pm, pc = _pool_consts()
    xmid, v, code_o, wts_o, cnt = _mix_bwd_call(
        hf, xc, gg, xp, mr, mp, x, wgb, ba, bx, lam, h0b,
        pm, pc, pool_w[0].astype(BF16), row(pool_scale[0]),
        w_rnn_proj[0], w_pool_proj[0], w_out[0],
        row(norm_post_mix[0]), row(norm_pre_ffn[0]), g1, sh2, sc2, router_w[0], row(router_b[0]))

    code = code_o.reshape(n, LANES)[:, :TOP_K].T
    idx, rank = code // RANK_RADIX, code % RANK_RADIX
    counts = cnt[0].astype(jnp.int32)
    tiles_e = (counts + TM - 1) // TM
    tile_end = jnp.cumsum(tiles_e)
    offs = (tile_end - tiles_e) * TM
    e_ids = jnp.arange(n_exp, dtype=jnp.int32)
    off_of = jnp.sum(jnp.where(idx[None] == e_ids[:, None, None], offs[:, None, None], 0), axis=0)
    pos_kmajor = (off_of + rank).reshape(TOP_K * n)
    n_tiles = (n * TOP_K) // TM + n_exp
    n_valid = tile_end[-1]
    jj = jnp.arange(n_tiles, dtype=jnp.int32)
    blk = jnp.minimum(jj, n_valid - 1)
    te = jnp.minimum(_count_le(tile_end, blk), n_exp - 1)

    pad_e = tiles_e * TM - counts
    pad_end = jnp.cumsum(pad_e)
    qq = jnp.arange(n_exp * TM, dtype=jnp.int32)
    pad_start = pad_end - pad_e
    inside = (qq[None, :] >= pad_start[:, None]) & (qq[None, :] < pad_end[:, None])
    slot_in = jnp.sum(jnp.where(inside, (offs + counts - pad_start)[:, None] + qq[None, :], 0), axis=0)
    slot_tail = n_valid * TM + qq - pad_end[-1]
    pad_slots = jnp.where(qq < pad_end[-1], slot_in, slot_tail).astype(jnp.int32)

    n_slots = n_tiles * TM
    inv = _invert_call(pos_kmajor, pad_slots, n, n_slots)
    gidx = jnp.where(inv < TOP_K * n, inv % n, 0)
    has = tiles_e > 0
    w_slot = ((jnp.cumsum(has.astype(jnp.int32)) - has.astype(jnp.int32)) % 2).astype(jnp.int32)
    later = (e_ids[None, :] > e_ids[:, None]) & has[None, :]
    w_next = jnp.min(jnp.where(later, e_ids[None, :], n_exp), axis=1).astype(jnp.int32)
    y4 = _expert_call(te, n_valid.reshape(1), w_slot, w_next, gidx, inv, v.reshape((n,) + v.shape[2:]),
                      exp_w1[0], exp_b1[0].reshape(n_exp, 1, -1), exp_w2[0], exp_b2[0].reshape(n_exp, 1, -1),
                      TOP_K * n + TM)
    out = _combine_call(y4, wts_o.reshape(n, LANES), xmid.reshape(n, d), g2, row(norm_post_ffn[0]), t)
    return out.reshape(bsz, t, d)
```

```python
import functools

import numpy as np
import jax
import jax.numpy as jnp
from jax import lax
from jax.experimental import pallas as pl
from jax.experimental.pallas import tpu as pltpu

F32 = jnp.float32
BF16 = jnp.bfloat16

RNN_HEADS = 4
CONV_W = 4
LRU_C = 8.0
POOL_WINDOWS = (2, 4, 8, 16)
GRID_W = 64
TOP_K = 4
SWIGLU_LIMIT = 7.0
SWIGLU_ALPHA = 1.702
EPS = 1e-6
SQRT_FLOOR = 1e-30

SUBLANES = 8
LANES = 128
VMEM_LIMIT = 56 * 1024 * 1024

TT = 256
POOL_CHUNK = 256
TM = 256
TD = 128
INVERT_BLOCK = 2048
ISSUE_UNROLL = 8
DMA_PRIORITIES = 2
RANK_RADIX = 1 << 16


def _sigmoid(x):
    return 1.0 / (1.0 + jnp.exp(-x))


def _softplus(z):
    return jnp.maximum(z, 0.0) + jnp.log1p(jnp.exp(-jnp.abs(z)))


def _sqrt_nonneg(x):
    return x * lax.rsqrt(jnp.maximum(x, SQRT_FLOOR))


def _rmsnorm(x, g):
    ms = jnp.mean(x * x, axis=-1, keepdims=True)
    return (x * lax.rsqrt(ms + EPS)) * g


def _modulate(u, shift, scale):
    return u * (1.0 + scale) + shift


def _dot(a, b):
    return jnp.dot(a, b, preferred_element_type=F32)


def _split(x):
    hi = x.astype(BF16)
    return hi, (x - hi.astype(F32)).astype(BF16)


def _dot3(a, b):
    ah, al = _split(a)
    bh, bl = _split(b)
    return _dot(ah, bh) + (_dot(al, bh) + _dot(ah, bl))


def _load_cast(w_hbm, dst_s, tmp_s, sem):
    c = tmp_s.shape[1]
    for j in range(w_hbm.shape[1] // c):
        cp = pltpu.make_async_copy(w_hbm.at[:, pl.ds(j * c, c)], tmp_s, sem)
        cp.start()
        cp.wait()
        dst_s[:, j * c:(j + 1) * c] = tmp_s[...].astype(BF16)


def _conv(prev8, xr, next8, cw, cb):
    t = xr.shape[0]
    ext = jnp.concatenate([prev8, xr, next8], axis=0)
    acc = cb + cw[0:1, :] * ext[SUBLANES - 2:SUBLANES - 2 + t, :]
    for k in range(1, CONV_W):
        off = SUBLANES - 2 + k
        acc = acc + cw[k:k + 1, :] * ext[off:off + t, :]
    return acc


def _lru_coeffs(xc, wg_ref, ba, bx, sp):
    hd = xc.shape[1] // RNN_HEADS
    xcb = xc.astype(BF16)
    a_parts, b_parts = [], []
    for h in range(RNN_HEADS):
        cols = slice(h * hd, (h + 1) * hd)
        z = _dot(xcb[:, cols], wg_ref[h])
        r = _sigmoid(z[:, :hd] + ba[:, cols])
        i = _sigmoid(z[:, hd:] + bx[:, cols])
        a = jnp.exp((-LRU_C) * r * sp[:, cols])
        b = _sqrt_nonneg(1.0 - a * a) * (i * xc[:, cols])
        a_parts.append(a)
        b_parts.append(b)
    return jnp.concatenate(a_parts, axis=1), jnp.concatenate(b_parts, axis=1)


def _scan_tile(a, b, h_in, reverse, store):
    t, c = a.shape
    g = t // SUBLANES
    a3 = a.reshape(g, SUBLANES, c)
    b3 = b.reshape(g, SUBLANES, c)
    row = lax.broadcasted_iota(jnp.int32, (g, SUBLANES, c), 1)
    for s in (1, 2, 4):
        if reverse:
            shift, m = SUBLANES - s, row < SUBLANES - s
        else:
            shift, m = s, row >= s
        ra = pltpu.roll(a3, shift, axis=1)
        rb = pltpu.roll(b3, shift, axis=1)
        b3 = a3 * jnp.where(m, rb, 0.0) + b3
        a3 = a3 * jnp.where(m, ra, 1.0)
    h = h_in
    order = range(g - 1, -1, -1) if reverse else range(g)
    for gi in order:
        hg = a3[gi] * h + b3[gi]
        store(gi, hg)
        h = hg[0:1, :] if reverse else hg[SUBLANES - 1:SUBLANES, :]
    return h


def _mod_body(c_ref, w_ref, b_ref, o_ref):
    c = c_ref[...]
    s = c * _sigmoid(c)
    o_ref[...] = _dot3(s, w_ref[...]) + b_ref[...]


def _mod_call(cc, w_mod, b_mod):
    d = cc.shape[1]
    n = w_mod.shape[1] // d
    return pl.pallas_call(
        _mod_body,
        out_shape=jax.ShapeDtypeStruct((cc.shape[0], n * d), F32),
        grid=(n,),
        in_specs=[pl.BlockSpec(cc.shape, lambda j: (0, 0)),
                  pl.BlockSpec((d, d), lambda j: (0, j)),
                  pl.BlockSpec((1, d), lambda j: (0, j))],
        out_specs=pl.BlockSpec((cc.shape[0], d), lambda j: (0, j)),
        name="mod",
    )(cc, w_mod, b_mod)


def _ctx_body(ctx_ref, g_ref, sh_ref, sc_ref, w_ref, cw_ref, cb_ref, wgf_ref, wgb_ref,
              ba_ref, bx_ref, lam_ref, hf_ref, hb_ref):
    d = ctx_ref.shape[2]
    u = _modulate(_rmsnorm(ctx_ref[0], g_ref[...]), sh_ref[...], sc_ref[...])
    xr = _dot(u.astype(BF16), w_ref[...].astype(BF16))
    z8 = jnp.zeros((SUBLANES, d), F32)
    xc = _conv(z8, xr, z8, cw_ref[...], cb_ref[...])
    h0 = jnp.zeros((1, d), F32)
    for di, (wg_ref, out_ref) in enumerate(((wgf_ref, hf_ref), (wgb_ref, hb_ref))):
        sp = _softplus(-lam_ref[di:di + 1, :])
        a, b = _lru_coeffs(xc, wg_ref, ba_ref[di:di + 1, :], bx_ref[di:di + 1, :], sp)
        out_ref[0] = _scan_tile(a, b, h0, di == 1, lambda gi, hg: None)


def _ctx_call(ctx, g, sh, sc, w_in, cw, cb, wgf, wgb, ba, bx, lam):
    bsz, tc, d = ctx.shape
    full = lambda shp: pl.BlockSpec(shp, lambda b: (0,) * len(shp))
    return pl.pallas_call(
        _ctx_body,
        out_shape=(jax.ShapeDtypeStruct((bsz, 1, d), F32),) * 2,
        grid=(bsz,),
        in_specs=[pl.BlockSpec((1, tc, d), lambda b: (b, 0, 0)),
                  full((1, d)), full((1, d)), full((1, d)),
                  pl.BlockSpec((d, d), lambda b: (0, 0)),
                  full(cw.shape), full((1, d)), full(wgf.shape), full(wgb.shape),
                  full(ba.shape), full(bx.shape), full(lam.shape)],
        out_specs=(pl.BlockSpec((1, 1, d), lambda b: (b, 0, 0)),) * 2,
        compiler_params=pltpu.CompilerParams(vmem_limit_bytes=VMEM_LIMIT),
        name="ctx",
    )(ctx, g, sh, sc, w_in, cw, cb, wgf, wgb, ba, bx, lam)


def _mix_fwd_body(x_ref, xn_ref, g_ref, sh_ref, sc_ref, win_ref, cw_ref, cb_ref, wgf_ref,
                  ba_ref, bx_ref, lam_ref, h0_ref,
                  hf_ref, xc_ref, gg_ref, xp_ref, mr_ref, mp_ref, tail_s, h_s, win_s, wtmp_s, wsem):
    i = pl.program_id(1)
    last = pl.num_programs(1) - 1
    tt, d = x_ref.shape[1], x_ref.shape[2]

    @pl.when((i == 0) & (pl.program_id(0) == 0))
    def _():
        _load_cast(win_ref, win_s, wtmp_s, wsem)

    @pl.when(i == 0)
    def _():
        tail_s[...] = jnp.zeros_like(tail_s)
        h_s[...] = h0_ref[0]

    g, sh, sc = g_ref[...], sh_ref[0], sc_ref[0]
    ub = _modulate(_rmsnorm(x_ref[0], g), sh, sc).astype(BF16)
    unb = _modulate(_rmsnorm(xn_ref[0], g), sh, sc).astype(BF16)
    w_rnn = win_s[:, 0:d]
    xr = _dot(ub, w_rnn)
    xrn = jnp.where(i == last, 0.0, _dot(unb, w_rnn))
    xc = _conv(tail_s[...], xr, xrn, cw_ref[...], cb_ref[...])
    tail_s[...] = xr[tt - SUBLANES:tt, :]
    xc_ref[0] = xc.astype(BF16)
    for j, ref in enumerate((gg_ref, xp_ref, mr_ref, mp_ref)):
        ref[0] = _dot(ub, win_s[:, (j + 1) * d:(j + 2) * d]).astype(BF16)

    sp = _softplus(-lam_ref[0:1, :])
    a, b = _lru_coeffs(xc, wgf_ref, ba_ref[0:1, :], bx_ref[0:1, :], sp)

    def store(gi, hg):
        hf_ref[0, gi * SUBLANES:(gi + 1) * SUBLANES, :] = hg

    h_s[...] = _scan_tile(a, b, h_s[...], False, store)


def _mix_fwd_call(x, g, sh, sc, w_in, cw, cb, wgf, ba, bx, lam, h0f):
    bsz, t, d = x.shape
    nt = t // TT
    nblk8 = t // SUBLANES
    per_b = pl.BlockSpec((1, 1, d), lambda b, i: (b, 0, 0))
    full = lambda shp: pl.BlockSpec(shp, lambda b, i: (0,) * len(shp))
    tile = pl.BlockSpec((1, TT, d), lambda b, i: (b, i, 0))
    return pl.pallas_call(
        _mix_fwd_body,
        out_shape=(jax.ShapeDtypeStruct((bsz, t, d), F32),) + (jax.ShapeDtypeStruct((bsz, t, d), BF16),) * 5,
        grid=(bsz, nt),
        in_specs=[tile,
                  pl.BlockSpec((1, SUBLANES, d),
                               lambda b, i: (b, jnp.minimum((i + 1) * (TT // SUBLANES), nblk8 - 1), 0)),
                  full((1, d)), per_b, per_b,
                  pl.BlockSpec(memory_space=pl.ANY), full(cw.shape), full((1, d)), full(wgf.shape),
                  full(ba.shape), full(bx.shape), full(lam.shape), per_b],
        out_specs=(tile,) * 6,
        scratch_shapes=[pltpu.VMEM((SUBLANES, d), F32), pltpu.VMEM((1, d), F32),
                        pltpu.VMEM(w_in.shape, BF16), pltpu.VMEM((d, d), F32), pltpu.SemaphoreType.DMA],
        compiler_params=pltpu.CompilerParams(
            dimension_semantics=("arbitrary", "arbitrary"), vmem_limit_bytes=VMEM_LIMIT),
        name="mix_fwd",
    )(x, x, g, sh, sc, w_in, cw, cb, wgf, ba, bx, lam, h0f)


def _pool_consts():
    p = np.arange(POOL_CHUNK)
    pos, line = p % GRID_W, p // GRID_W
    mats, cnts = [], []
    for w in POOL_WINDOWS:
        lo = np.clip(pos - w // 2, 0, GRID_W)
        hi = np.clip(pos + w - w // 2, 0, GRID_W)
        m = (line[:, None] == line[None, :]) & (pos[None, :] >= lo[:, None]) & (pos[None, :] < hi[:, None])
        mats.append(m.astype(np.float32))
        cnts.append((hi - lo).astype(np.float32)[:, None])
    return jnp.asarray(np.stack(mats), BF16), jnp.asarray(np.stack(cnts), F32)


def _mix_bwd_body(hf_ref, xc_ref, gg_ref, xp_ref, mr_ref, mp_ref, x_ref,
                  wgb_ref, ba_ref, bx_ref, lam_ref, h0_ref,
                  pm_ref, pc_ref, pw_ref, ps_ref, wr_ref, wp_ref, wo_ref,
                  npost_ref, npre_ref, g1_ref, sh2_ref, sc2_ref, rw_ref, rb_ref,
                  xmid_ref, v_ref, code_ref, wts_ref, cnt_ref,
                  hb_s, h_s, cnt_s, w3_s, wtmp_s, wsem):
    b_id, i = pl.program_id(0), pl.program_id(1)
    tt, d = x_ref.shape[1], x_ref.shape[2]
    n_exp = rw_ref.shape[1]

    @pl.when((i == 0) & (b_id == 0))
    def _():
        for k, w_hbm in enumerate((wr_ref, wp_ref, wo_ref)):
            _load_cast(w_hbm, w3_s.at[k], wtmp_s, wsem)

    @pl.when(i == 0)
    def _():
        h_s[...] = h0_ref[0]

    @pl.when((i == 0) & (b_id == 0))
    def _():
        cnt_s[...] = jnp.zeros_like(cnt_s)

    xc = xc_ref[0].astype(F32)
    sp = _softplus(-lam_ref[1:2, :])
    a, b = _lru_coeffs(xc, wgb_ref, ba_ref[1:2, :], bx_ref[1:2, :], sp)

    def store(gi, hg):
        hb_s[gi * SUBLANES:(gi + 1) * SUBLANES, :] = hg

    h_s[...] = _scan_tile(a, b, h_s[...], True, store)

    gg = gg_ref[0].astype(F32)
    gelu = gg * _sigmoid(gg * (1.5957691216057308 + 0.07135481627260025 * (gg * gg)))
    y_rnn = (hf_ref[0] + hb_s[...]) * gelu

    xpb = xp_ref[0]
    grp = d // len(POOL_WINDOWS)
    y_parts = []
    for gi in range(len(POOL_WINDOWS)):
        cols = slice(gi * grp, (gi + 1) * grp)
        rows = []
        for c0 in range(0, tt, POOL_CHUNK):
            xg = xpb[c0:c0 + POOL_CHUNK, cols]
            mean = _dot(pm_ref[gi], xg) / pc_ref[gi]
            rows.append((mean - xg.astype(F32)).astype(BF16))
        dg = rows[0] if len(rows) == 1 else jnp.concatenate(rows, axis=0)
        y_parts.append(_dot(dg, pw_ref[gi]) * ps_ref[:, cols])
    y_pool = jnp.concatenate(y_parts, axis=1)

    merged = (_sigmoid(mr_ref[0].astype(F32)) * _dot(y_rnn.astype(BF16), w3_s[0])
              + _sigmoid(mp_ref[0].astype(F32)) * _dot(y_pool.astype(BF16), w3_s[1]))
    m_lat = _dot(merged.astype(BF16), w3_s[2])
    x_mid = x_ref[0] + g1_ref[0] * _rmsnorm(m_lat, npost_ref[...])
    xmid_ref[0] = x_mid
    v = _modulate(_rmsnorm(x_mid, npre_ref[...]), sh2_ref[0], sc2_ref[0])
    v_ref[0] = v.reshape(tt, SUBLANES, LANES)

    logits = _dot3(v, rw_ref[...]) + rb_ref[...]
    lane = lax.broadcasted_iota(jnp.int32, (tt, n_exp), 1).astype(F32)
    work = logits
    vals, idxs, sels = [], [], []
    for _ in range(TOP_K):
        m = jnp.max(work, axis=-1, keepdims=True)
        idx = jnp.min(jnp.where(work == m, lane, float(n_exp)), axis=-1, keepdims=True)
        sel = lane == idx
        vals.append(m)
        idxs.append(idx)
        sels.append(sel)
        work = jnp.where(sel, -jnp.inf, work)
    exps = [jnp.exp(vk - vals[0]) for vk in vals]
    den = exps[0] + exps[1] + exps[2] + exps[3]
    anyf = jnp.zeros((tt, n_exp), F32)
    for sel in sels:
        anyf = anyf + sel.astype(F32)
    r_i = lax.broadcasted_iota(jnp.int32, (tt, tt), 0)
    c_i = lax.broadcasted_iota(jnp.int32, (tt, tt), 1)
    lower = (c_i < r_i).astype(BF16)
    before = _dot(lower, anyf.astype(BF16)) + cnt_s[...]
    cnt_new = cnt_s[...] + jnp.sum(anyf, axis=0, keepdims=True)
    cnt_s[...] = cnt_new
    cnt_ref[...] = cnt_new

    lane_o = lax.broadcasted_iota(jnp.int32, (tt, LANES), 1)
    code_o = jnp.zeros((tt, LANES), jnp.int32)
    wts_o = jnp.zeros((tt, LANES), F32)
    for k in range(TOP_K):
        rk = jnp.sum(jnp.where(sels[k], before, 0.0), axis=-1, keepdims=True)
        code = idxs[k].astype(jnp.int32) * RANK_RADIX + rk.astype(jnp.int32)
        code_o = jnp.where(lane_o == k, code, code_o)
        wts_o = jnp.where(lane_o == k, exps[k] / den, wts_o)
    code_ref[0] = code_o
    wts_ref[0] = wts_o


def _mix_bwd_call(hf, xc, gg, xp, mr, mp, x, wgb, ba, bx, lam, h0b, pm, pc, pw, ps, wr, wp, wo,
                  npost, npre, g1, sh2, sc2, rw, rb):
    bsz, t, d = x.shape
    nt = t // TT
    n_exp = rw.shape[1]
    rev = lambda b, i: (b, nt - 1 - i, 0)
    tile = pl.BlockSpec((1, TT, d), rev)
    small = pl.BlockSpec((1, TT, LANES), rev)
    per_b = pl.BlockSpec((1, 1, d), lambda b, i: (b, 0, 0))
    full = lambda shp: pl.BlockSpec(shp, lambda b, i: (0,) * len(shp))
    return pl.pallas_call(
        _mix_bwd_body,
        out_shape=(jax.ShapeDtypeStruct((bsz, t, d), F32),
                   jax.ShapeDtypeStruct((bsz, t, SUBLANES, LANES), F32),
                   jax.ShapeDtypeStruct((bsz, t, LANES), jnp.int32),
                   jax.ShapeDtypeStruct((bsz, t, LANES), F32),
                   jax.ShapeDtypeStruct((1, n_exp), F32)),
        grid=(bsz, nt),
        in_specs=[tile] * 7 + [full(wgb.shape), full(ba.shape), full(bx.shape), full(lam.shape), per_b,
                               full(pm.shape), full(pc.shape), full(pw.shape), full((1, d)),
                               pl.BlockSpec(memory_space=pl.ANY), pl.BlockSpec(memory_space=pl.ANY),
                               pl.BlockSpec(memory_space=pl.ANY),
                               full((1, d)), full((1, d)), per_b, per_b, per_b,
                               full(rw.shape), full((1, n_exp))],
        out_specs=(tile, pl.BlockSpec((1, TT, SUBLANES, LANES), lambda b, i: (b, nt - 1 - i, 0, 0)),
                   small, small, full((1, n_exp))),
        scratch_shapes=[pltpu.VMEM((TT, d), F32), pltpu.VMEM((1, d), F32), pltpu.VMEM((1, n_exp), F32),
                        pltpu.VMEM((3, d, d), BF16), pltpu.VMEM((d, d), F32), pltpu.SemaphoreType.DMA],
        compiler_params=pltpu.CompilerParams(
            dimension_semantics=("arbitrary", "arbitrary"), vmem_limit_bytes=VMEM_LIMIT),
        name="mix_bwd",
    )(hf, xc, gg, xp, mr, mp, x, wgb, ba, bx, lam, h0b, pm, pc, pw, ps, wr, wp, wo,
      npost, npre, g1, sh2, sc2, rw, rb)


def _invert_body(n_tok, pos_ref, pad_ref, inv_ref):
    i = pl.program_id(0)
    nb = pos_ref.shape[0]
    base = i * nb

    def body(q, carry):
        inv_ref[pos_ref[q]] = base + q
        return carry

    lax.fori_loop(0, nb, body, 0, unroll=ISSUE_UNROLL)

    def body_pad(q, carry):
        p = pad_ref[q]
        inv_ref[p] = TOP_K * n_tok + lax.rem(p, 2 * TM)
        return carry

    lax.fori_loop(0, pad_ref.shape[0], body_pad, 0, unroll=ISSUE_UNROLL)


def _invert_call(pos_kmajor, pad_slots, n_tok, n_slots):
    steps = pos_kmajor.shape[0] // INVERT_BLOCK
    pad_block = pad_slots.shape[0] // steps
    return pl.pallas_call(
        functools.partial(_invert_body, n_tok),
        out_shape=jax.ShapeDtypeStruct((n_slots,), jnp.int32),
        grid=(steps,),
        in_specs=[pl.BlockSpec((INVERT_BLOCK,), lambda i: (i,), memory_space=pltpu.SMEM),
                  pl.BlockSpec((pad_block,), lambda i: (i,), memory_space=pltpu.SMEM)],
        out_specs=pl.BlockSpec(memory_space=pltpu.SMEM),
        compiler_params=pltpu.CompilerParams(dimension_semantics=("arbitrary",)),
        name="invert",
    )(pos_kmajor, pad_slots)


def _expert_body(te_ref, nvalid_ref, slot_ref, nxt_ref,
                 gcur_ref, gnxt_ref, sprev_ref, scur_ref,
                 v_hbm, w1_hbm, b1_ref, w2_hbm, b2_ref, y_hbm,
                 w1f_s, w2f_s, w1b_s, w2b_s, xbuf, ybuf, wsem, gsem, ssem):
    j = pl.program_id(0)
    n_exp = w1_hbm.shape[0]
    ff = w2_hbm.shape[1]
    tm = xbuf.shape[1]
    d = w1_hbm.shape[1]
    n_valid = nvalid_ref[0]
    e = te_ref[j]
    prev = te_ref[jnp.maximum(j - 1, 0)]
    cur = lax.rem(j, 2)
    nxt = 1 - cur

    def weight_copies(expert, slot):
        return (pltpu.make_async_copy(w1_hbm.at[expert], w1f_s.at[slot], wsem.at[slot, 0]),
                pltpu.make_async_copy(w2_hbm.at[expert], w2f_s.at[slot], wsem.at[slot, 1]))

    def gather(idx_ref, buf, r):
        return pltpu.make_async_copy(v_hbm.at[idx_ref[r]], xbuf.at[buf, r], gsem.at[buf])

    def scatter(idx_ref, buf, r):
        return pltpu.make_async_copy(ybuf.at[buf, r], y_hbm.at[idx_ref[r]], ssem.at[buf])

    def dump_row(r):
        return pltpu.make_async_copy(ybuf.at[0, r], y_hbm.at[y_hbm.shape[0] - 2 * tm + r], ssem.at[0])

    @pl.when(j == 0)
    def _():
        for c in weight_copies(e, slot_ref[e]):
            c.start()
        for r in range(tm):
            gather(gcur_ref, 0, r).start(priority=r % DMA_PRIORITIES)
        ybuf[...] = jnp.zeros_like(ybuf)
        for r in range(tm):
            dump_row(r).start(priority=r % DMA_PRIORITIES)

    @pl.when((j == 0) | (e != prev))
    def _():
        slot = slot_ref[e]
        for c in weight_copies(e, slot):
            c.wait()
        w1b_s[...] = w1f_s[slot].astype(BF16)
        w2b_s[...] = w2f_s[slot].astype(BF16)

        @pl.when(nxt_ref[e] < n_exp)
        def _():
            for c in weight_copies(nxt_ref[e], 1 - slot):
                c.start()

    def tile(is_last):
        for r in range(tm):
            gather(gcur_ref, cur, r).wait()
        for r in range(tm):
            scatter(sprev_ref, nxt, r).start(priority=r % DMA_PRIORITIES)
        x = xbuf[cur].reshape(tm, d)
        z = _dot(x.astype(BF16), w1b_s[...]) + b1_ref[0]
        if not is_last:
            for r in range(tm):
                gather(gnxt_ref, nxt, r).start(priority=r % DMA_PRIORITIES)
        glu = jnp.minimum(z[:, :ff], SWIGLU_LIMIT)
        lin = jnp.clip(z[:, ff:], -SWIGLU_LIMIT, SWIGLU_LIMIT)
        act = glu * _sigmoid(SWIGLU_ALPHA * glu) * (lin + 1.0)
        y = _dot(act.astype(BF16), w2b_s[...]) + b2_ref[0]
        for r in range(tm):
            scatter(sprev_ref, cur, r).wait()
        ybuf[cur] = y.reshape(ybuf.shape[1:])
        if is_last:
            for r in range(tm):
                scatter(sprev_ref, nxt, r).wait()
            for r in range(tm):
                scatter(scur_ref, cur, r).start(priority=r % DMA_PRIORITIES)
            for r in range(tm):
                scatter(scur_ref, cur, r).wait()

    @pl.when(j < n_valid - 1)
    def _():
        tile(False)

    @pl.when(j == n_valid - 1)
    def _():
        tile(True)


def _expert_call(te, nvalid, slot, nxt, gidx, sdst, v3, w1, b1, w2, b2, n_rows_out):
    n_slots = gidx.shape[0]
    slab = v3.shape[1:]
    n_exp, d, ff2 = w1.shape
    ff = w2.shape[1]
    n_tiles = n_slots // TM
    last = lambda nv: nv[0] - 1
    smem = lambda f: pl.BlockSpec((TM,), f, memory_space=pltpu.SMEM)
    grid_spec = pltpu.PrefetchScalarGridSpec(
        num_scalar_prefetch=4,
        grid=(n_tiles,),
        in_specs=[smem(lambda j, te, nv, *_: (jnp.minimum(j, last(nv)),)),
                  smem(lambda j, te, nv, *_: (jnp.minimum(j + 1, last(nv)),)),
                  smem(lambda j, te, nv, *_: (jnp.minimum(j, last(nv)),)),
                  smem(lambda j, te, nv, *_: (jnp.minimum(j, last(nv)) + 1,)),
                  pl.BlockSpec(memory_space=pl.ANY),
                  pl.BlockSpec(memory_space=pl.ANY),
                  pl.BlockSpec((1, 1, ff2), lambda j, te, *_: (te[j], 0, 0)),
                  pl.BlockSpec(memory_space=pl.ANY),
                  pl.BlockSpec((1, 1, d), lambda j, te, *_: (te[j], 0, 0))],
        out_specs=pl.BlockSpec(memory_space=pl.ANY),
        scratch_shapes=[pltpu.VMEM((2, d, ff2), F32), pltpu.VMEM((2, ff, d), F32),
                        pltpu.VMEM((d, ff2), BF16), pltpu.VMEM((ff, d), BF16),
                        pltpu.VMEM((2, TM) + slab, F32), pltpu.VMEM((2, TM) + slab, F32),
                        pltpu.SemaphoreType.DMA((2, 2)), pltpu.SemaphoreType.DMA((2,)),
                        pltpu.SemaphoreType.DMA((2,))],
    )
    return pl.pallas_call(
        _expert_body,
        out_shape=jax.ShapeDtypeStruct((n_rows_out,) + slab, F32),
        grid_spec=grid_spec,
        compiler_params=pltpu.CompilerParams(
            dimension_semantics=("arbitrary",), vmem_limit_bytes=VMEM_LIMIT),
        name="experts",
    )(te, nvalid, slot, nxt, gidx, gidx, sdst, sdst, v3, w1, b1, w2, b2)


def _combine_body(*refs):
    y_refs = refs[:TOP_K]
    wts_ref, xmid_ref, g2_ref, npost_ref, o_ref = refs[TOP_K:]
    w = wts_ref[...]
    f = w[:, 0:1] * y_refs[0][...].reshape(xmid_ref.shape)
    for k in range(1, TOP_K):
        f = f + w[:, k:k + 1] * y_refs[k][...].reshape(xmid_ref.shape)
    o_ref[...] = xmid_ref[...] + g2_ref[0] * _rmsnorm(f, npost_ref[...])


def _combine_call(y4, wts, xmid2, g2, npost, t_per_batch):
    n, d = xmid2.shape
    steps = n // TD
    slab = y4.shape[1:]
    return pl.pallas_call(
        _combine_body,
        out_shape=jax.ShapeDtypeStruct((n, d), F32),
        grid=(steps,),
        in_specs=[pl.BlockSpec((TD,) + slab, lambda i, k=k: (k * steps + i, 0, 0)) for k in range(TOP_K)] + [
                  pl.BlockSpec((TD, LANES), lambda i: (i, 0)),
                  pl.BlockSpec((TD, d), lambda i: (i, 0)),
                  pl.BlockSpec((1, 1, d), lambda i: ((i * TD) // t_per_batch, 0, 0)),
                  pl.BlockSpec((1, d), lambda i: (0, 0))],
        out_specs=pl.BlockSpec((TD, d), lambda i: (i, 0)),
        compiler_params=pltpu.CompilerParams(dimension_semantics=("arbitrary",)),
        name="combine",
    )(*([y4] * TOP_K), wts, xmid2, g2, npost)


def _count_le(ends, q):
    return jnp.sum((ends[None, :] <= q[:, None]).astype(jnp.int32), axis=1)


def _gate_weights(wa, wx):
    return jnp.concatenate([wa, wx], axis=-1).astype(BF16)


def kernel(x, c, ctx, c_ctx, w_mod, b_mod, norm_pre_mix, norm_post_mix, norm_pre_ffn, norm_post_ffn, w_in, conv_w, conv_b, lru_wa, lru_ba, lru_wx, lru_bx, lru_lambda, pool_w, pool_scale, w_rnn_proj, w_pool_proj, w_out, router_w, router_b, exp_w1, exp_b1, exp_w2, exp_b2):
    bsz, t, d = x.shape
    assert w_mod.shape[0] == 1, "single-layer block"
    assert t % TT == 0 and TT % POOL_CHUNK == 0 and POOL_CHUNK % GRID_W == 0 and t % TD == 0
    assert d == SUBLANES * LANES, "a token row must be exactly one (8, 128) slab"
    n = bsz * t
    n_exp = router_w.shape[-1]
    assert (n * TOP_K) % TM == 0 and (n * TOP_K) % INVERT_BLOCK == 0 and n <= RANK_RADIX
    row = lambda a: a.reshape(1, -1)

    pad = jnp.zeros((SUBLANES - (bsz + 1) % SUBLANES, d), F32)
    cc = jnp.concatenate([c, c_ctx[None, :], pad], axis=0)
    mod = _mod_call(cc, w_mod[0], row(b_mod[0]))
    lat = lambda j: mod[:bsz, j * d:(j + 1) * d].reshape(bsz, 1, d)
    sh1, sc1, g1, sh2, sc2, g2 = (lat(j) for j in range(6))
    csh1, csc1 = mod[bsz:bsz + 1, 0:d], mod[bsz:bsz + 1, d:2 * d]

    wgf = _gate_weights(lru_wa[0, 0], lru_wx[0, 0])
    wgb = _gate_weights(lru_wa[0, 1], lru_wx[0, 1])
    ba, bx, lam = lru_ba[0], lru_bx[0], lru_lambda[0]
    cw, cb = conv_w[0], row(conv_b[0])
    npre_mix = row(norm_pre_mix[0])

    h0f, h0b = _ctx_call(ctx, npre_mix, csh1, csc1, w_in[0], cw, cb, wgf, wgb, ba, bx, lam)

    hf, xc, gg, xp, mr, mp = _mix_fwd_call(x, npre_mix, sh1, sc1, w_in[0], cw, cb, wgf, ba, bx, lam, h0f)

    pm, pc = _pool_consts()
    xmid, v, code_o, wts_o, cnt = _mix_bwd_call(
        hf, xc, gg, xp, mr, mp, x, wgb, ba, bx, lam, h0b,
        pm, pc, pool_w[0].astype(BF16), row(pool_scale[0]),
        w_rnn_proj[0], w_pool_proj[0], w_out[0],
        row(norm_post_mix[0]), row(norm_pre_ffn[0]), g1, sh2, sc2, router_w[0], row(router_b[0]))

    code = code_o.reshape(n, LANES)[:, :TOP_K].T
    idx, rank = code // RANK_RADIX, code % RANK_RADIX
    counts = cnt[0].astype(jnp.int32)
    tiles_e = (counts + TM - 1) // TM
    tile_end = jnp.cumsum(tiles_e)
    offs = (tile_end - tiles_e) * TM
    e_ids = jnp.arange(n_exp, dtype=jnp.int32)
    off_of = jnp.sum(jnp.where(idx[None] == e_ids[:, None, None], offs[:, None, None], 0), axis=0)
    pos_kmajor = (off_of + rank).reshape(TOP_K * n)
    n_tiles = (n * TOP_K) // TM + n_exp
    n_valid = tile_end[-1]
    jj = jnp.arange(n_tiles, dtype=jnp.int32)
    blk = jnp.minimum(jj, n_valid - 1)
    te = jnp.minimum(_count_le(tile_end, blk), n_exp - 1)

    pad_e = tiles_e * TM - counts
    pad_end = jnp.cumsum(pad_e)
    qq = jnp.arange(n_exp * TM, dtype=jnp.int32)
    pad_start = pad_end - pad_e
    inside = (qq[None, :] >= pad_start[:, None]) & (qq[None, :] < pad_end[:, None])
    slot_in = jnp.sum(jnp.where(inside, (offs + counts - pad_start)[:, None] + qq[None, :], 0), axis=0)
    slot_tail = n_valid * TM + qq - pad_end[-1]
    pad_slots = jnp.where(qq < pad_end[-1], slot_in, slot_tail).astype(jnp.int32)

    n_slots = n_tiles * TM
    inv = _invert_call(pos_kmajor, pad_slots, n, n_slots)
    gidx = jnp.where(inv < TOP_K * n, inv % n, 0)
    has = tiles_e > 0
    w_slot = ((jnp.cumsum(has.astype(jnp.int32)) - has.astype(jnp.int32)) % 2).astype(jnp.int32)
    later = (e_ids[None, :] > e_ids[:, None]) & has[None, :]
    w_next = jnp.min(jnp.where(later, e_ids[None, :], n_exp), axis=1).astype(jnp.int32)
    sdst = jnp.concatenate([TOP_K * n + TM + jnp.arange(TM, dtype=jnp.int32), inv])
    y4 = _expert_call(te, n_valid.reshape(1), w_slot, w_next, gidx, sdst, v.reshape((n,) + v.shape[2:]),
                      exp_w1[0], exp_b1[0].reshape(n_exp, 1, -1), exp_w2[0], exp_b2[0].reshape(n_exp, 1, -1),
                      TOP_K * n + 2 * TM)
    out = _combine_call(y4, wts_o.reshape(n, LANES), xmid.reshape(n, d), g2, row(norm_post_ffn[0]), t)
    return out.reshape(bsz, t, d)
```

```python
import functools

import numpy as np
import jax
import jax.numpy as jnp
from jax import lax
from jax.experimental import pallas as pl
from jax.experimental.pallas import tpu as pltpu

F32 = jnp.float32
BF16 = jnp.bfloat16

RNN_HEADS = 4
CONV_W = 4
LRU_C = 8.0
POOL_WINDOWS = (2, 4, 8, 16)
GRID_W = 64
TOP_K = 4
SWIGLU_LIMIT = 7.0
SWIGLU_ALPHA = 1.702
EPS = 1e-6
SQRT_FLOOR = 1e-30

SUBLANES = 8
LANES = 128
VMEM_LIMIT = 56 * 1024 * 1024

TT = 256
POOL_CHUNK = 256
TM = 256
TD = 128
PAD_BLOCK = 128
DISPATCH_BUFS = 3
ISSUE_UNROLL = 8
DMA_PRIORITIES = 2
RANK_RADIX = 1 << 16


def _sigmoid(x):
    return 1.0 / (1.0 + jnp.exp(-x))


def _softplus(z):
    return jnp.maximum(z, 0.0) + jnp.log1p(jnp.exp(-jnp.abs(z)))


def _sqrt_nonneg(x):
    return x * lax.rsqrt(jnp.maximum(x, SQRT_FLOOR))


def _rmsnorm(x, g):
    ms = jnp.mean(x * x, axis=-1, keepdims=True)
    return (x * lax.rsqrt(ms + EPS)) * g


def _modulate(u, shift, scale):
    return u * (1.0 + scale) + shift


def _dot(a, b):
    return jnp.dot(a, b, preferred_element_type=F32)


def _split(x):
    hi = x.astype(BF16)
    return hi, (x - hi.astype(F32)).astype(BF16)


def _dot3(a, b):
    ah, al = _split(a)
    bh, bl = _split(b)
    return _dot(ah, bh) + (_dot(al, bh) + _dot(ah, bl))


def _load_cast(w_hbm, dst_s, tmp_s, sem):
    c = tmp_s.shape[1]
    for j in range(w_hbm.shape[1] // c):
        cp = pltpu.make_async_copy(w_hbm.at[:, pl.ds(j * c, c)], tmp_s, sem)
        cp.start()
        cp.wait()
        dst_s[:, j * c:(j + 1) * c] = tmp_s[...].astype(BF16)


def _conv(prev8, xr, next8, cw, cb):
    t = xr.shape[0]
    ext = jnp.concatenate([prev8, xr, next8], axis=0)
    acc = cb + cw[0:1, :] * ext[SUBLANES - 2:SUBLANES - 2 + t, :]
    for k in range(1, CONV_W):
        off = SUBLANES - 2 + k
        acc = acc + cw[k:k + 1, :] * ext[off:off + t, :]
    return acc


def _lru_coeffs(xc, wg_ref, ba, bx, sp):
    hd = xc.shape[1] // RNN_HEADS
    xcb = xc.astype(BF16)
    a_parts, b_parts = [], []
    for h in range(RNN_HEADS):
        cols = slice(h * hd, (h + 1) * hd)
        z = _dot(xcb[:, cols], wg_ref[h])
        r = _sigmoid(z[:, :hd] + ba[:, cols])
        i = _sigmoid(z[:, hd:] + bx[:, cols])
        a = jnp.exp((-LRU_C) * r * sp[:, cols])
        b = _sqrt_nonneg(1.0 - a * a) * (i * xc[:, cols])
        a_parts.append(a)
        b_parts.append(b)
    return jnp.concatenate(a_parts, axis=1), jnp.concatenate(b_parts, axis=1)


def _scan_tile(a, b, h_in, reverse, store):
    t, c = a.shape
    g = t // SUBLANES
    a3 = a.reshape(g, SUBLANES, c)
    b3 = b.reshape(g, SUBLANES, c)
    row = lax.broadcasted_iota(jnp.int32, (g, SUBLANES, c), 1)
    for s in (1, 2, 4):
        if reverse:
            shift, m = SUBLANES - s, row < SUBLANES - s
        else:
            shift, m = s, row >= s
        ra = pltpu.roll(a3, shift, axis=1)
        rb = pltpu.roll(b3, shift, axis=1)
        b3 = a3 * jnp.where(m, rb, 0.0) + b3
        a3 = a3 * jnp.where(m, ra, 1.0)
    h = h_in
    order = range(g - 1, -1, -1) if reverse else range(g)
    for gi in order:
        hg = a3[gi] * h + b3[gi]
        store(gi, hg)
        h = hg[0:1, :] if reverse else hg[SUBLANES - 1:SUBLANES, :]
    return h


def _mod_body(c_ref, w_ref, b_ref, o_ref):
    c = c_ref[...]
    s = c * _sigmoid(c)
    o_ref[...] = _dot3(s, w_ref[...]) + b_ref[...]


def _mod_call(cc, w_mod, b_mod):
    d = cc.shape[1]
    n = w_mod.shape[1] // d
    return pl.pallas_call(
        _mod_body,
        out_shape=jax.ShapeDtypeStruct((cc.shape[0], n * d), F32),
        grid=(n,),
        in_specs=[pl.BlockSpec(cc.shape, lambda j: (0, 0)),
                  pl.BlockSpec((d, d), lambda j: (0, j)),
                  pl.BlockSpec((1, d), lambda j: (0, j))],
        out_specs=pl.BlockSpec((cc.shape[0], d), lambda j: (0, j)),
        name="mod",
    )(cc, w_mod, b_mod)


def _ctx_body(ctx_ref, g_ref, sh_ref, sc_ref, w_ref, cw_ref, cb_ref, wgf_ref, wgb_ref,
              ba_ref, bx_ref, lam_ref, hf_ref, hb_ref):
    d = ctx_ref.shape[2]
    u = _modulate(_rmsnorm(ctx_ref[0], g_ref[...]), sh_ref[...], sc_ref[...])
    xr = _dot(u.astype(BF16), w_ref[...].astype(BF16))
    z8 = jnp.zeros((SUBLANES, d), F32)
    xc = _conv(z8, xr, z8, cw_ref[...], cb_ref[...])
    h0 = jnp.zeros((1, d), F32)
    for di, (wg_ref, out_ref) in enumerate(((wgf_ref, hf_ref), (wgb_ref, hb_ref))):
        sp = _softplus(-lam_ref[di:di + 1, :])
        a, b = _lru_coeffs(xc, wg_ref, ba_ref[di:di + 1, :], bx_ref[di:di + 1, :], sp)
        out_ref[0] = _scan_tile(a, b, h0, di == 1, lambda gi, hg: None)


def _ctx_call(ctx, g, sh, sc, w_in, cw, cb, wgf, wgb, ba, bx, lam):
    bsz, tc, d = ctx.shape
    full = lambda shp: pl.BlockSpec(shp, lambda b: (0,) * len(shp))
    return pl.pallas_call(
        _ctx_body,
        out_shape=(jax.ShapeDtypeStruct((bsz, 1, d), F32),) * 2,
        grid=(bsz,),
        in_specs=[pl.BlockSpec((1, tc, d), lambda b: (b, 0, 0)),
                  full((1, d)), full((1, d)), full((1, d)),
                  pl.BlockSpec((d, d), lambda b: (0, 0)),
                  full(cw.shape), full((1, d)), full(wgf.shape), full(wgb.shape),
                  full(ba.shape), full(bx.shape), full(lam.shape)],
        out_specs=(pl.BlockSpec((1, 1, d), lambda b: (b, 0, 0)),) * 2,
        compiler_params=pltpu.CompilerParams(vmem_limit_bytes=VMEM_LIMIT),
        name="ctx",
    )(ctx, g, sh, sc, w_in, cw, cb, wgf, wgb, ba, bx, lam)


def _mix_fwd_body(x_ref, xn_ref, g_ref, sh_ref, sc_ref, win_ref, cw_ref, cb_ref, wgf_ref,
                  ba_ref, bx_ref, lam_ref, h0_ref,
                  hf_ref, xc_ref, gg_ref, xp_ref, mr_ref, mp_ref, tail_s, h_s, win_s, wtmp_s, wsem):
    i = pl.program_id(1)
    last = pl.num_programs(1) - 1
    tt, d = x_ref.shape[1], x_ref.shape[2]

    @pl.when((i == 0) & (pl.program_id(0) == 0))
    def _():
        _load_cast(win_ref, win_s, wtmp_s, wsem)

    @pl.when(i == 0)
    def _():
        tail_s[...] = jnp.zeros_like(tail_s)
        h_s[...] = h0_ref[0]

    g, sh, sc = g_ref[...], sh_ref[0], sc_ref[0]
    ub = _modulate(_rmsnorm(x_ref[0], g), sh, sc).astype(BF16)
    unb = _modulate(_rmsnorm(xn_ref[0], g), sh, sc).astype(BF16)
    w_rnn = win_s[:, 0:d]
    xr = _dot(ub, w_rnn)
    xrn = jnp.where(i == last, 0.0, _dot(unb, w_rnn))
    xc = _conv(tail_s[...], xr, xrn, cw_ref[...], cb_ref[...])
    tail_s[...] = xr[tt - SUBLANES:tt, :]
    xc_ref[0] = xc.astype(BF16)
    for j, ref in enumerate((gg_ref, xp_ref, mr_ref, mp_ref)):
        ref[0] = _dot(ub, win_s[:, (j + 1) * d:(j + 2) * d]).astype(BF16)

    sp = _softplus(-lam_ref[0:1, :])
    a, b = _lru_coeffs(xc, wgf_ref, ba_ref[0:1, :], bx_ref[0:1, :], sp)

    def store(gi, hg):
        hf_ref[0, gi * SUBLANES:(gi + 1) * SUBLANES, :] = hg

    h_s[...] = _scan_tile(a, b, h_s[...], False, store)


def _mix_fwd_call(x, g, sh, sc, w_in, cw, cb, wgf, ba, bx, lam, h0f):
    bsz, t, d = x.shape
    nt = t // TT
    nblk8 = t // SUBLANES
    per_b = pl.BlockSpec((1, 1, d), lambda b, i: (b, 0, 0))
    full = lambda shp: pl.BlockSpec(shp, lambda b, i: (0,) * len(shp))
    tile = pl.BlockSpec((1, TT, d), lambda b, i: (b, i, 0))
    return pl.pallas_call(
        _mix_fwd_body,
        out_shape=(jax.ShapeDtypeStruct((bsz, t, d), F32),) + (jax.ShapeDtypeStruct((bsz, t, d), BF16),) * 5,
        grid=(bsz, nt),
        in_specs=[tile,
                  pl.BlockSpec((1, SUBLANES, d),
                               lambda b, i: (b, jnp.minimum((i + 1) * (TT // SUBLANES), nblk8 - 1), 0)),
                  full((1, d)), per_b, per_b,
                  pl.BlockSpec(memory_space=pl.ANY), full(cw.shape), full((1, d)), full(wgf.shape),
                  full(ba.shape), full(bx.shape), full(lam.shape), per_b],
        out_specs=(tile,) * 6,
        scratch_shapes=[pltpu.VMEM((SUBLANES, d), F32), pltpu.VMEM((1, d), F32),
                        pltpu.VMEM(w_in.shape, BF16), pltpu.VMEM((d, d), F32), pltpu.SemaphoreType.DMA],
        compiler_params=pltpu.CompilerParams(
            dimension_semantics=("arbitrary", "arbitrary"), vmem_limit_bytes=VMEM_LIMIT),
        name="mix_fwd",
    )(x, x, g, sh, sc, w_in, cw, cb, wgf, ba, bx, lam, h0f)


def _pool_consts():
    p = np.arange(POOL_CHUNK)
    pos, line = p % GRID_W, p // GRID_W
    mats, cnts = [], []
    for w in POOL_WINDOWS:
        lo = np.clip(pos - w // 2, 0, GRID_W)
        hi = np.clip(pos + w - w // 2, 0, GRID_W)
        m = (line[:, None] == line[None, :]) & (pos[None, :] >= lo[:, None]) & (pos[None, :] < hi[:, None])
        mats.append(m.astype(np.float32))
        cnts.append((hi - lo).astype(np.float32)[:, None])
    return jnp.asarray(np.stack(mats), BF16), jnp.asarray(np.stack(cnts), F32)


def _mix_bwd_body(hf_ref, xc_ref, gg_ref, xp_ref, mr_ref, mp_ref, x_ref,
                  wgb_ref, ba_ref, bx_ref, lam_ref, h0_ref,
                  pm_ref, pc_ref, pw_ref, ps_ref, wr_ref, wp_ref, wo_ref,
                  npost_ref, npre_ref, g1_ref, sh2_ref, sc2_ref, rw_ref, rb_ref,
                  xmid_ref, v_ref, code_ref, wts_ref, cnt_ref,
                  hb_s, h_s, cnt_s, w3_s, wtmp_s, wsem):
    b_id, i = pl.program_id(0), pl.program_id(1)
    tt, d = x_ref.shape[1], x_ref.shape[2]
    n_exp = rw_ref.shape[1]

    @pl.when((i == 0) & (b_id == 0))
    def _():
        for k, w_hbm in enumerate((wr_ref, wp_ref, wo_ref)):
            _load_cast(w_hbm, w3_s.at[k], wtmp_s, wsem)

    @pl.when(i == 0)
    def _():
        h_s[...] = h0_ref[0]

    @pl.when((i == 0) & (b_id == 0))
    def _():
        cnt_s[...] = jnp.zeros_like(cnt_s)

    xc = xc_ref[0].astype(F32)
    sp = _softplus(-lam_ref[1:2, :])
    a, b = _lru_coeffs(xc, wgb_ref, ba_ref[1:2, :], bx_ref[1:2, :], sp)

    def store(gi, hg):
        hb_s[gi * SUBLANES:(gi + 1) * SUBLANES, :] = hg

    h_s[...] = _scan_tile(a, b, h_s[...], True, store)

    gg = gg_ref[0].astype(F32)
    gelu = gg * _sigmoid(gg * (1.5957691216057308 + 0.07135481627260025 * (gg * gg)))
    y_rnn = (hf_ref[0] + hb_s[...]) * gelu

    xpb = xp_ref[0]
    grp = d // len(POOL_WINDOWS)
    y_parts = []
    for gi in range(len(POOL_WINDOWS)):
        cols = slice(gi * grp, (gi + 1) * grp)
        rows = []
        for c0 in range(0, tt, POOL_CHUNK):
            xg = xpb[c0:c0 + POOL_CHUNK, cols]
            mean = _dot(pm_ref[gi], xg) / pc_ref[gi]
            rows.append((mean - xg.astype(F32)).astype(BF16))
        dg = rows[0] if len(rows) == 1 else jnp.concatenate(rows, axis=0)
        y_parts.append(_dot(dg, pw_ref[gi]) * ps_ref[:, cols])
    y_pool = jnp.concatenate(y_parts, axis=1)

    merged = (_sigmoid(mr_ref[0].astype(F32)) * _dot(y_rnn.astype(BF16), w3_s[0])
              + _sigmoid(mp_ref[0].astype(F32)) * _dot(y_pool.astype(BF16), w3_s[1]))
    m_lat = _dot(merged.astype(BF16), w3_s[2])
    x_mid = x_ref[0] + g1_ref[0] * _rmsnorm(m_lat, npost_ref[...])
    xmid_ref[0] = x_mid
    v = _modulate(_rmsnorm(x_mid, npre_ref[...]), sh2_ref[0], sc2_ref[0])
    v_ref[0] = v.reshape(tt, SUBLANES, LANES)

    logits = _dot3(v, rw_ref[...]) + rb_ref[...]
    lane = lax.broadcasted_iota(jnp.int32, (tt, n_exp), 1).astype(F32)
    work = logits
    vals, idxs, sels = [], [], []
    for _ in range(TOP_K):
        m = jnp.max(work, axis=-1, keepdims=True)
        idx = jnp.min(jnp.where(work == m, lane, float(n_exp)), axis=-1, keepdims=True)
        sel = lane == idx
        vals.append(m)
        idxs.append(idx)
        sels.append(sel)
        work = jnp.where(sel, -jnp.inf, work)
    exps = [jnp.exp(vk - vals[0]) for vk in vals]
    den = exps[0] + exps[1] + exps[2] + exps[3]
    anyf = jnp.zeros((tt, n_exp), F32)
    for sel in sels:
        anyf = anyf + sel.astype(F32)
    r_i = lax.broadcasted_iota(jnp.int32, (tt, tt), 0)
    c_i = lax.broadcasted_iota(jnp.int32, (tt, tt), 1)
    lower = (c_i < r_i).astype(BF16)
    before = _dot(lower, anyf.astype(BF16)) + cnt_s[...]
    cnt_new = cnt_s[...] + jnp.sum(anyf, axis=0, keepdims=True)
    cnt_s[...] = cnt_new
    cnt_ref[...] = cnt_new

    lane_o = lax.broadcasted_iota(jnp.int32, (tt, LANES), 1)
    code_o = jnp.zeros((tt, LANES), jnp.int32)
    wts_o = jnp.zeros((tt, LANES), F32)
    for k in range(TOP_K):
        rk = jnp.sum(jnp.where(sels[k], before, 0.0), axis=-1, keepdims=True)
        code = idxs[k].astype(jnp.int32) * RANK_RADIX + rk.astype(jnp.int32)
        code_o = jnp.where(lane_o == k, code, code_o)
        wts_o = jnp.where(lane_o == k, exps[k] / den, wts_o)
    code_ref[0] = code_o
    wts_ref[0] = wts_o


def _mix_bwd_call(hf, xc, gg, xp, mr, mp, x, wgb, ba, bx, lam, h0b, pm, pc, pw, ps, wr, wp, wo,
                  npost, npre, g1, sh2, sc2, rw, rb):
    bsz, t, d = x.shape
    nt = t // TT
    n_exp = rw.shape[1]
    rev = lambda b, i: (b, nt - 1 - i, 0)
    tile = pl.BlockSpec((1, TT, d), rev)
    small = pl.BlockSpec((1, TT, LANES), rev)
    per_b = pl.BlockSpec((1, 1, d), lambda b, i: (b, 0, 0))
    full = lambda shp: pl.BlockSpec(shp, lambda b, i: (0,) * len(shp))
    return pl.pallas_call(
        _mix_bwd_body,
        out_shape=(jax.ShapeDtypeStruct((bsz, t, d), F32),
                   jax.ShapeDtypeStruct((bsz, t, SUBLANES, LANES), F32),
                   jax.ShapeDtypeStruct((bsz, t, LANES), jnp.int32),
                   jax.ShapeDtypeStruct((bsz, t, LANES), F32),
                   jax.ShapeDtypeStruct((1, n_exp), F32)),
        grid=(bsz, nt),
        in_specs=[tile] * 7 + [full(wgb.shape), full(ba.shape), full(bx.shape), full(lam.shape), per_b,
                               full(pm.shape), full(pc.shape), full(pw.shape), full((1, d)),
                               pl.BlockSpec(memory_space=pl.ANY), pl.BlockSpec(memory_space=pl.ANY),
                               pl.BlockSpec(memory_space=pl.ANY),
                               full((1, d)), full((1, d)), per_b, per_b, per_b,
                               full(rw.shape), full((1, n_exp))],
        out_specs=(tile, pl.BlockSpec((1, TT, SUBLANES, LANES), lambda b, i: (b, nt - 1 - i, 0, 0)),
                   small, small, full((1, n_exp))),
        scratch_shapes=[pltpu.VMEM((TT, d), F32), pltpu.VMEM((1, d), F32), pltpu.VMEM((1, n_exp), F32),
                        pltpu.VMEM((3, d, d), BF16), pltpu.VMEM((d, d), F32), pltpu.SemaphoreType.DMA],
        compiler_params=pltpu.CompilerParams(
            dimension_semantics=("arbitrary", "arbitrary"), vmem_limit_bytes=VMEM_LIMIT),
        name="mix_bwd",
    )(hf, xc, gg, xp, mr, mp, x, wgb, ba, bx, lam, h0b, pm, pc, pw, ps, wr, wp, wo,
      npost, npre, g1, sh2, sc2, rw, rb)


def _dispatch_body(n_pad_blocks, *refs):
    pos_refs = refs[:TOP_K]
    pad_ref, zero_ref, v_hbm, xs_hbm, vbuf, lsem, ssem = refs[TOP_K:]
    i = pl.program_id(0)
    steps = pl.num_programs(0)
    n_pad = pad_ref.shape[0]
    slot = lax.rem(i, DISPATCH_BUFS)

    def load(step):
        s = lax.rem(step, DISPATCH_BUFS)
        return pltpu.make_async_copy(v_hbm.at[pl.ds(step * TD, TD)], vbuf.at[s], lsem.at[s])

    def row_copy(s, src_row, dst_row):
        return pltpu.make_async_copy(vbuf.at[s, src_row], xs_hbm.at[dst_row], ssem.at[s])

    def pad_copy(s, dst_row):
        return pltpu.make_async_copy(zero_ref.at[0], xs_hbm.at[dst_row], ssem.at[s])

    def wait_scatter(step):
        s = lax.rem(step, DISPATCH_BUFS)

        def body(t, carry):
            for _ in range(TOP_K):
                row_copy(s, 0, 0).wait()
            return carry

        lax.fori_loop(0, TD, body, 0, unroll=ISSUE_UNROLL)

        @pl.when(step < n_pad_blocks)
        def _():
            def body_pad(q, carry):
                pad_copy(s, 0).wait()
                return carry

            lax.fori_loop(0, n_pad, body_pad, 0, unroll=ISSUE_UNROLL)

    @pl.when(i == 0)
    def _():
        load(0).start()

    @pl.when(i >= 2)
    def _():
        wait_scatter(i - 2)

    @pl.when(i + 1 < steps)
    def _():
        load(i + 1).start()

    load(i).wait()

    def issue(t, carry):
        for k in range(TOP_K):
            row_copy(slot, t, pos_refs[k][t]).start(priority=k % DMA_PRIORITIES)
        return carry

    lax.fori_loop(0, TD, issue, 0, unroll=ISSUE_UNROLL)

    @pl.when(i < n_pad_blocks)
    def _():
        def issue_pad(q, carry):
            pad_copy(slot, pad_ref[q]).start()
            return carry

        lax.fori_loop(0, n_pad, issue_pad, 0, unroll=ISSUE_UNROLL)

    @pl.when(i == steps - 1)
    def _():
        @pl.when(i >= 1)
        def _():
            wait_scatter(i - 1)

        wait_scatter(i)


def _pos_specs(steps, step_of):
    return [pl.BlockSpec((TD,), lambda i, k=k: (k * steps + step_of(i),), memory_space=pltpu.SMEM)
            for k in range(TOP_K)]


def _dispatch_call(pos_kmajor, pad_slots, v3, n_slots):
    n = v3.shape[0]
    slab = v3.shape[1:]
    steps = n // TD
    n_pad_blocks = pad_slots.shape[0] // PAD_BLOCK
    assert n_pad_blocks <= steps
    return pl.pallas_call(
        functools.partial(_dispatch_body, n_pad_blocks),
        out_shape=jax.ShapeDtypeStruct((n_slots,) + slab, F32),
        grid=(steps,),
        in_specs=_pos_specs(steps, lambda i: i) + [
            pl.BlockSpec((PAD_BLOCK,), lambda i: (jnp.minimum(i, n_pad_blocks - 1),), memory_space=pltpu.SMEM),
            pl.BlockSpec((1,) + slab, lambda i: (0, 0, 0)),
            pl.BlockSpec(memory_space=pl.ANY)],
        out_specs=pl.BlockSpec(memory_space=pl.ANY),
        scratch_shapes=[pltpu.VMEM((DISPATCH_BUFS, TD) + slab, F32),
                        pltpu.SemaphoreType.DMA((DISPATCH_BUFS,)), pltpu.SemaphoreType.DMA((DISPATCH_BUFS,))],
        compiler_params=pltpu.CompilerParams(dimension_semantics=("arbitrary",)),
        name="dispatch",
    )(*([pos_kmajor] * TOP_K), pad_slots, jnp.zeros((1,) + slab, F32), v3)


def _expert_body(te_ref, blk_ref, nvalid_ref, slot_ref, nxt_ref,
                 xs_ref, w1_hbm, b1_ref, w2_hbm, b2_ref, y_ref, w1f_s, w2f_s, w1b_s, w2b_s, sem):
    j = pl.program_id(0)
    n_exp = w1_hbm.shape[0]
    ff = w2_hbm.shape[1]
    e = te_ref[j]
    prev = te_ref[jnp.maximum(j - 1, 0)]

    def weight_copies(expert, slot):
        return (pltpu.make_async_copy(w1_hbm.at[expert], w1f_s.at[slot], sem.at[slot, 0]),
                pltpu.make_async_copy(w2_hbm.at[expert], w2f_s.at[slot], sem.at[slot, 1]))

    @pl.when(j == 0)
    def _():
        for c in weight_copies(e, slot_ref[e]):
            c.start()

    @pl.when((j == 0) | (e != prev))
    def _():
        slot = slot_ref[e]
        for c in weight_copies(e, slot):
            c.wait()
        w1b_s[...] = w1f_s[slot].astype(BF16)
        w2b_s[...] = w2f_s[slot].astype(BF16)

        @pl.when(nxt_ref[e] < n_exp)
        def _():
            for c in weight_copies(nxt_ref[e], 1 - slot):
                c.start()

    @pl.when(j < nvalid_ref[0])
    def _():
        tm = xs_ref.shape[0]
        x = xs_ref[...].reshape(tm, w1_hbm.shape[1])
        z = _dot(x.astype(BF16), w1b_s[...]) + b1_ref[0]
        glu = jnp.minimum(z[:, :ff], SWIGLU_LIMIT)
        lin = jnp.clip(z[:, ff:], -SWIGLU_LIMIT, SWIGLU_LIMIT)
        act = glu * _sigmoid(SWIGLU_ALPHA * glu) * (lin + 1.0)
        y = _dot(act.astype(BF16), w2b_s[...]) + b2_ref[0]
        y_ref[...] = y.reshape(y_ref.shape)

    @pl.when(j >= nvalid_ref[0])
    def _():
        y_ref[...] = jnp.zeros_like(y_ref)


def _expert_call(te, blk, nvalid, slot, nxt, xs, w1, b1, w2, b2):
    n_slots = xs.shape[0]
    slab = xs.shape[1:]
    n_exp, d, ff2 = w1.shape
    ff = w2.shape[1]
    grid_spec = pltpu.PrefetchScalarGridSpec(
        num_scalar_prefetch=5,
        grid=(n_slots // TM,),
        in_specs=[pl.BlockSpec((TM,) + slab, lambda j, te, blk, *_: (blk[j], 0, 0)),
                  pl.BlockSpec(memory_space=pl.ANY),
                  pl.BlockSpec((1, 1, ff2), lambda j, te, *_: (te[j], 0, 0)),
                  pl.BlockSpec(memory_space=pl.ANY),
                  pl.BlockSpec((1, 1, d), lambda j, te, *_: (te[j], 0, 0))],
        out_specs=pl.BlockSpec((TM,) + slab, lambda j, *_: (j, 0, 0)),
        scratch_shapes=[pltpu.VMEM((2, d, ff2), F32), pltpu.VMEM((2, ff, d), F32),
                        pltpu.VMEM((d, ff2), BF16), pltpu.VMEM((ff, d), BF16),
                        pltpu.SemaphoreType.DMA((2, 2))],
    )
    return pl.pallas_call(
        _expert_body,
        out_shape=jax.ShapeDtypeStruct((n_slots,) + slab, F32),
        grid_spec=grid_spec,
        compiler_params=pltpu.CompilerParams(
            dimension_semantics=("arbitrary",), vmem_limit_bytes=VMEM_LIMIT),
        name="experts",
    )(te, blk, nvalid, slot, nxt, xs, w1, b1, w2, b2)


def _combine_body(*refs):
    pos_refs, posn_refs = refs[:TOP_K], refs[TOP_K:2 * TOP_K]
    wts_ref, xmid_ref, g2_ref, npost_ref, y_hbm, o_ref, buf, sem = refs[2 * TOP_K:]
    i = pl.program_id(0)
    cur = lax.rem(i, 2)

    def row_copy(p_refs, slot, t, k):
        return pltpu.make_async_copy(y_hbm.at[p_refs[k][t]], buf.at[slot, k, t], sem.at[slot])

    def issue(p_refs, slot):
        def body(t, carry):
            for k in range(TOP_K):
                row_copy(p_refs, slot, t, k).start(priority=k % DMA_PRIORITIES)
            return carry

        lax.fori_loop(0, TD, body, 0, unroll=ISSUE_UNROLL)

    @pl.when(i == 0)
    def _():
        issue(pos_refs, 0)

    def drain(t, carry):
        for k in range(TOP_K):
            row_copy(pos_refs, cur, t, k).wait()
        return carry

    lax.fori_loop(0, TD, drain, 0, unroll=ISSUE_UNROLL)

    def step(fetch_next):
        w = wts_ref[...]
        f = w[:, 0:1] * buf[cur, 0].reshape(xmid_ref.shape)
        for k in range(1, TOP_K):
            f = f + w[:, k:k + 1] * buf[cur, k].reshape(xmid_ref.shape)
        res = xmid_ref[...] + g2_ref[0] * _rmsnorm(f, npost_ref[...])
        if fetch_next:
            for t in range(TD):
                for k in range(TOP_K):
                    row_copy(posn_refs, 1 - cur, t, k).start(priority=k % DMA_PRIORITIES)
        o_ref[...] = res

    @pl.when(i + 1 < pl.num_programs(0))
    def _():
        step(True)

    @pl.when(i + 1 == pl.num_programs(0))
    def _():
        step(False)


def _combine_call(pos_kmajor, wts, xmid2, g2, npost, ys, t_per_batch):
    n, d = xmid2.shape
    steps = n // TD
    return pl.pallas_call(
        _combine_body,
        out_shape=jax.ShapeDtypeStruct((n, d), F32),
        grid=(steps,),
        in_specs=_pos_specs(steps, lambda i: i) + _pos_specs(steps, lambda i: jnp.minimum(i + 1, steps - 1)) + [
                  pl.BlockSpec((TD, LANES), lambda i: (i, 0)),
                  pl.BlockSpec((TD, d), lambda i: (i, 0)),
                  pl.BlockSpec((1, 1, d), lambda i: ((i * TD) // t_per_batch, 0, 0)),
                  pl.BlockSpec((1, d), lambda i: (0, 0)),
                  pl.BlockSpec(memory_space=pl.ANY)],
        out_specs=pl.BlockSpec((TD, d), lambda i: (i, 0)),
        scratch_shapes=[pltpu.VMEM((2, TOP_K, TD) + ys.shape[1:], F32), pltpu.SemaphoreType.DMA((2,))],
        compiler_params=pltpu.CompilerParams(dimension_semantics=("arbitrary",)),
        name="combine",
    )(*([pos_kmajor] * (2 * TOP_K)), wts, xmid2, g2, npost, ys)


def _count_le(ends, q):
    return jnp.sum((ends[None, :] <= q[:, None]).astype(jnp.int32), axis=1)


def _gate_weights(wa, wx):
    return jnp.concatenate([wa, wx], axis=-1).astype(BF16)


def kernel(x, c, ctx, c_ctx, w_mod, b_mod, norm_pre_mix, norm_post_mix, norm_pre_ffn, norm_post_ffn, w_in, conv_w, conv_b, lru_wa, lru_ba, lru_wx, lru_bx, lru_lambda, pool_w, pool_scale, w_rnn_proj, w_pool_proj, w_out, router_w, router_b, exp_w1, exp_b1, exp_w2, exp_b2):
    bsz, t, d = x.shape
    assert w_mod.shape[0] == 1, "single-layer block"
    assert t % TT == 0 and TT % POOL_CHUNK == 0 and POOL_CHUNK % GRID_W == 0 and t % TD == 0
    assert d == SUBLANES * LANES, "a token row must be exactly one (8, 128) slab"
    n = bsz * t
    n_exp = router_w.shape[-1]
    assert (n * TOP_K) % TM == 0 and (n_exp * TM) % PAD_BLOCK == 0 and n <= RANK_RADIX
    row = lambda a: a.reshape(1, -1)

    pad = jnp.zeros((SUBLANES - (bsz + 1) % SUBLANES, d), F32)
    cc = jnp.concatenate([c, c_ctx[None, :], pad], axis=0)
    mod = _mod_call(cc, w_mod[0], row(b_mod[0]))
    lat = lambda j: mod[:bsz, j * d:(j + 1) * d].reshape(bsz, 1, d)
    sh1, sc1, g1, sh2, sc2, g2 = (lat(j) for j in range(6))
    csh1, csc1 = mod[bsz:bsz + 1, 0:d], mod[bsz:bsz + 1, d:2 * d]

    wgf = _gate_weights(lru_wa[0, 0], lru_wx[0, 0])
    wgb = _gate_weights(lru_wa[0, 1], lru_wx[0, 1])
    ba, bx, lam = lru_ba[0], lru_bx[0], lru_lambda[0]
    cw, cb = conv_w[0], row(conv_b[0])
    npre_mix = row(norm_pre_mix[0])

    h0f, h0b = _ctx_call(ctx, npre_mix, csh1, csc1, w_in[0], cw, cb, wgf, wgb, ba, bx, lam)

    hf, xc, gg, xp, mr, mp = _mix_fwd_call(x, npre_mix, sh1, sc1, w_in[0], cw, cb, wgf, ba, bx, lam, h0f)

    pm, pc = _pool_consts()
    xmid, v, code_o, wts_o, cnt = _mix_bwd_call(
        hf, xc, gg, xp, mr, mp, x, wgb, ba, bx, lam, h0b,
        pm, pc, pool_w[0].astype(BF16), row(pool_scale[0]),
        w_rnn_proj[0], w_pool_proj[0], w_out[0],
        row(norm_post_mix[0]), row(norm_pre_ffn[0]), g1, sh2, sc2, router_w[0], row(router_b[0]))

    code = code_o.reshape(n, LANES)[:, :TOP_K].T
    idx, rank = code // RANK_RADIX, code % RANK_RADIX
    counts = cnt[0].astype(jnp.int32)
    tiles_e = (counts + TM - 1) // TM
    tile_end = jnp.cumsum(tiles_e)
    offs = (tile_end - tiles_e) * TM
    e_ids = jnp.arange(n_exp, dtype=jnp.int32)
    off_of = jnp.sum(jnp.where(idx[None] == e_ids[:, None, None], offs[:, None, None], 0), axis=0)
    pos_kmajor = (off_of + rank).reshape(TOP_K * n)
    n_tiles = (n * TOP_K) // TM + n_exp
    n_valid = tile_end[-1]
    jj = jnp.arange(n_tiles, dtype=jnp.int32)
    blk = jnp.minimum(jj, n_valid - 1)
    te = jnp.minimum(_count_le(tile_end, blk), n_exp - 1)

    pad_e = tiles_e * TM - counts
    pad_end = jnp.cumsum(pad_e)
    qq = jnp.arange(n_exp * TM, dtype=jnp.int32)
    pad_start = pad_end - pad_e
    inside = (qq[None, :] >= pad_start[:, None]) & (qq[None, :] < pad_end[:, None])
    slot_in = jnp.sum(jnp.where(inside, (offs + counts - pad_start)[:, None] + qq[None, :], 0), axis=0)
    slot_tail = n_valid * TM + qq - pad_end[-1]
    pad_slots = jnp.where(qq < pad_end[-1], slot_in, slot_tail).astype(jnp.int32)

    xs = _dispatch_call(pos_kmajor, pad_slots, v.reshape((n,) + v.shape[2:]), n_tiles * TM)
    has = tiles_e > 0
    w_slot = ((jnp.cumsum(has.astype(jnp.int32)) - has.astype(jnp.int32)) % 2).astype(jnp.int32)
    later = (e_ids[None, :] > e_ids[:, None]) & has[None, :]
    w_next = jnp.min(jnp.where(later, e_ids[None, :], n_exp), axis=1).astype(jnp.int32)
    ys = _expert_call(te, blk, n_valid.reshape(1), w_slot, w_next, xs, exp_w1[0],
                      exp_b1[0].reshape(n_exp, 1, -1), exp_w2[0], exp_b2[0].reshape(n_exp, 1, -1))
    out = _combine_call(pos_kmajor, wts_o.reshape(n, LANES), xmid.reshape(n, d), g2, row(norm_post_ffn[0]), ys, t)
    return out.reshape(bsz, t, d)
```

```python
import functools

import numpy as np
import jax
import jax.numpy as jnp
from jax import lax
from jax.experimental import pallas as pl
from jax.experimental.pallas import tpu as pltpu

F32 = jnp.float32
BF16 = jnp.bfloat16

RNN_HEADS = 4
CONV_W = 4
LRU_C = 8.0
POOL_WINDOWS = (2, 4, 8, 16)
GRID_W = 64
TOP_K = 4
SWIGLU_LIMIT = 7.0
SWIGLU_ALPHA = 1.702
EPS = 1e-6
SQRT_FLOOR = 1e-30

SUBLANES = 8
LANES = 128
VMEM_LIMIT = 56 * 1024 * 1024

TT = 256
POOL_CHUNK = 256
TM = 256
TD = 128
PAD_BLOCK = 128
DISPATCH_BUFS = 3
ISSUE_UNROLL = 8
DMA_PRIORITIES = 2
RANK_RADIX = 1 << 16


def _sigmoid(x):
    return 1.0 / (1.0 + jnp.exp(-x))


def _softplus(z):
    return jnp.maximum(z, 0.0) + jnp.log1p(jnp.exp(-jnp.abs(z)))


def _sqrt_nonneg(x):
    return x * lax.rsqrt(jnp.maximum(x, SQRT_FLOOR))


def _rmsnorm(x, g):
    ms = jnp.mean(x * x, axis=-1, keepdims=True)
    return (x * lax.rsqrt(ms + EPS)) * g


def _modulate(u, shift, scale):
    return u * (1.0 + scale) + shift


def _dot(a, b):
    return jnp.dot(a, b, preferred_element_type=F32)


def _split(x):
    hi = x.astype(BF16)
    return hi, (x - hi.astype(F32)).astype(BF16)


def _dot3(a, b):
    ah, al = _split(a)
    bh, bl = _split(b)
    return _dot(ah, bh) + (_dot(al, bh) + _dot(ah, bl))


def _load_cast(w_hbm, dst_s, tmp_s, sem):
    c = tmp_s.shape[1]
    for j in range(w_hbm.shape[1] // c):
        cp = pltpu.make_async_copy(w_hbm.at[:, pl.ds(j * c, c)], tmp_s, sem)
        cp.start()
        cp.wait()
        dst_s[:, j * c:(j + 1) * c] = tmp_s[...].astype(BF16)


def _conv(prev8, xr, next8, cw, cb):
    t = xr.shape[0]
    ext = jnp.concatenate([prev8, xr, next8], axis=0)
    acc = cb + cw[0:1, :] * ext[SUBLANES - 2:SUBLANES - 2 + t, :]
    for k in range(1, CONV_W):
        off = SUBLANES - 2 + k
        acc = acc + cw[k:k + 1, :] * ext[off:off + t, :]
    return acc


def _lru_coeffs(xc, wg_ref, ba, bx, sp):
    hd = xc.shape[1] // RNN_HEADS
    xcb = xc.astype(BF16)
    a_parts, b_parts = [], []
    for h in range(RNN_HEADS):
        cols = slice(h * hd, (h + 1) * hd)
        z = _dot(xcb[:, cols], wg_ref[h])
        r = _sigmoid(z[:, :hd] + ba[:, cols])
        i = _sigmoid(z[:, hd:] + bx[:, cols])
        a = jnp.exp((-LRU_C) * r * sp[:, cols])
        b = _sqrt_nonneg(1.0 - a * a) * (i * xc[:, cols])
        a_parts.append(a)
        b_parts.append(b)
    return jnp.concatenate(a_parts, axis=1), jnp.concatenate(b_parts, axis=1)


def _scan_tile(a, b, h_in, reverse, store):
    t, c = a.shape
    g = t // SUBLANES
    a3 = a.reshape(g, SUBLANES, c)
    b3 = b.reshape(g, SUBLANES, c)
    row = lax.broadcasted_iota(jnp.int32, (g, SUBLANES, c), 1)
    for s in (1, 2, 4):
        if reverse:
            shift, m = SUBLANES - s, row < SUBLANES - s
        else:
            shift, m = s, row >= s
        ra = pltpu.roll(a3, shift, axis=1)
        rb = pltpu.roll(b3, shift, axis=1)
        b3 = a3 * jnp.where(m, rb, 0.0) + b3
        a3 = a3 * jnp.where(m, ra, 1.0)
    h = h_in
    order = range(g - 1, -1, -1) if reverse else range(g)
    for gi in order:
        hg = a3[gi] * h + b3[gi]
        store(gi, hg)
        h = hg[0:1, :] if reverse else hg[SUBLANES - 1:SUBLANES, :]
    return h


def _mod_body(c_ref, w_ref, b_ref, o_ref):
    c = c_ref[...]
    s = c * _sigmoid(c)
    o_ref[...] = _dot3(s, w_ref[...]) + b_ref[...]


def _mod_call(cc, w_mod, b_mod):
    d = cc.shape[1]
    n = w_mod.shape[1] // d
    return pl.pallas_call(
        _mod_body,
        out_shape=jax.ShapeDtypeStruct((cc.shape[0], n * d), F32),
        grid=(n,),
        in_specs=[pl.BlockSpec(cc.shape, lambda j: (0, 0)),
                  pl.BlockSpec((d, d), lambda j: (0, j)),
                  pl.BlockSpec((1, d), lambda j: (0, j))],
        out_specs=pl.BlockSpec((cc.shape[0], d), lambda j: (0, j)),
        name="mod",
    )(cc, w_mod, b_mod)


def _ctx_body(ctx_ref, g_ref, sh_ref, sc_ref, w_ref, cw_ref, cb_ref, wgf_ref, wgb_ref,
              ba_ref, bx_ref, lam_ref, hf_ref, hb_ref):
    d = ctx_ref.shape[2]
    u = _modulate(_rmsnorm(ctx_ref[0], g_ref[...]), sh_ref[...], sc_ref[...])
    xr = _dot(u.astype(BF16), w_ref[...].astype(BF16))
    z8 = jnp.zeros((SUBLANES, d), F32)
    xc = _conv(z8, xr, z8, cw_ref[...], cb_ref[...])
    h0 = jnp.zeros((1, d), F32)
    for di, (wg_ref, out_ref) in enumerate(((wgf_ref, hf_ref), (wgb_ref, hb_ref))):
        sp = _softplus(-lam_ref[di:di + 1, :])
        a, b = _lru_coeffs(xc, wg_ref, ba_ref[di:di + 1, :], bx_ref[di:di + 1, :], sp)
        out_ref[0] = _scan_tile(a, b, h0, di == 1, lambda gi, hg: None)


def _ctx_call(ctx, g, sh, sc, w_in, cw, cb, wgf, wgb, ba, bx, lam):
    bsz, tc, d = ctx.shape
    full = lambda shp: pl.BlockSpec(shp, lambda b: (0,) * len(shp))
    return pl.pallas_call(
        _ctx_body,
        out_shape=(jax.ShapeDtypeStruct((bsz, 1, d), F32),) * 2,
        grid=(bsz,),
        in_specs=[pl.BlockSpec((1, tc, d), lambda b: (b, 0, 0)),
                  full((1, d)), full((1, d)), full((1, d)),
                  pl.BlockSpec((d, d), lambda b: (0, 0)),
                  full(cw.shape), full((1, d)), full(wgf.shape), full(wgb.shape),
                  full(ba.shape), full(bx.shape), full(lam.shape)],
        out_specs=(pl.BlockSpec((1, 1, d), lambda b: (b, 0, 0)),) * 2,
        compiler_params=pltpu.CompilerParams(vmem_limit_bytes=VMEM_LIMIT),
        name="ctx",
    )(ctx, g, sh, sc, w_in, cw, cb, wgf, wgb, ba, bx, lam)


def _mix_fwd_body(x_ref, xn_ref, g_ref, sh_ref, sc_ref, win_ref, cw_ref, cb_ref, wgf_ref,
                  ba_ref, bx_ref, lam_ref, h0_ref,
                  hf_ref, xc_ref, gg_ref, xp_ref, mr_ref, mp_ref, tail_s, h_s, win_s, wtmp_s, wsem):
    i = pl.program_id(0)
    last = pl.num_programs(0) - 1
    bsz, tt, d = x_ref.shape

    @pl.when(i == 0)
    def _():
        _load_cast(win_ref, win_s, wtmp_s, wsem)
        tail_s[...] = jnp.zeros_like(tail_s)
        h_s[...] = h0_ref[...]

    g = g_ref[...]
    w_rnn = win_s[:, 0:d]
    sp = _softplus(-lam_ref[0:1, :])
    for bi in range(bsz):
        sh, sc = sh_ref[bi], sc_ref[bi]
        ub = _modulate(_rmsnorm(x_ref[bi], g), sh, sc).astype(BF16)
        unb = _modulate(_rmsnorm(xn_ref[bi], g), sh, sc).astype(BF16)
        xr = _dot(ub, w_rnn)
        xrn = jnp.where(i == last, 0.0, _dot(unb, w_rnn))
        xc = _conv(tail_s[bi], xr, xrn, cw_ref[...], cb_ref[...])
        tail_s[bi] = xr[tt - SUBLANES:tt, :]
        xc_ref[bi] = xc.astype(BF16)
        for j, ref in enumerate((gg_ref, xp_ref, mr_ref, mp_ref)):
            ref[bi] = _dot(ub, win_s[:, (j + 1) * d:(j + 2) * d]).astype(BF16)

        a, b = _lru_coeffs(xc, wgf_ref, ba_ref[0:1, :], bx_ref[0:1, :], sp)

        def store(gi, hg, bi=bi):
            hf_ref[bi, gi * SUBLANES:(gi + 1) * SUBLANES, :] = hg

        h_s[bi] = _scan_tile(a, b, h_s[bi], False, store)


def _mix_fwd_call(x, g, sh, sc, w_in, cw, cb, wgf, ba, bx, lam, h0f):
    bsz, t, d = x.shape
    nt = t // TT
    nblk8 = t // SUBLANES
    per_b = pl.BlockSpec((bsz, 1, d), lambda i: (0, 0, 0))
    full = lambda shp: pl.BlockSpec(shp, lambda i: (0,) * len(shp))
    tile = pl.BlockSpec((bsz, TT, d), lambda i: (0, i, 0))
    return pl.pallas_call(
        _mix_fwd_body,
        out_shape=(jax.ShapeDtypeStruct((bsz, t, d), F32),) + (jax.ShapeDtypeStruct((bsz, t, d), BF16),) * 5,
        grid=(nt,),
        in_specs=[tile,
                  pl.BlockSpec((bsz, SUBLANES, d),
                               lambda i: (0, jnp.minimum((i + 1) * (TT // SUBLANES), nblk8 - 1), 0)),
                  full((1, d)), per_b, per_b,
                  pl.BlockSpec(memory_space=pl.ANY), full(cw.shape), full((1, d)), full(wgf.shape),
                  full(ba.shape), full(bx.shape), full(lam.shape), per_b],
        out_specs=(tile,) * 6,
        scratch_shapes=[pltpu.VMEM((bsz, SUBLANES, d), F32), pltpu.VMEM((bsz, 1, d), F32),
                        pltpu.VMEM(w_in.shape, BF16), pltpu.VMEM((d, d), F32), pltpu.SemaphoreType.DMA],
        compiler_params=pltpu.CompilerParams(
            dimension_semantics=("arbitrary",), vmem_limit_bytes=VMEM_LIMIT),
        name="mix_fwd",
    )(x, x, g, sh, sc, w_in, cw, cb, wgf, ba, bx, lam, h0f)


def _pool_consts():
    p = np.arange(POOL_CHUNK)
    pos, line = p % GRID_W, p // GRID_W
    mats, cnts = [], []
    for w in POOL_WINDOWS:
        lo = np.clip(pos - w // 2, 0, GRID_W)
        hi = np.clip(pos + w - w // 2, 0, GRID_W)
        m = (line[:, None] == line[None, :]) & (pos[None, :] >= lo[:, None]) & (pos[None, :] < hi[:, None])
        mats.append(m.astype(np.float32))
        cnts.append((hi - lo).astype(np.float32)[:, None])
    return jnp.asarray(np.stack(mats), BF16), jnp.asarray(np.stack(cnts), F32)


def _mix_bwd_body(hf_ref, xc_ref, gg_ref, xp_ref, mr_ref, mp_ref, x_ref,
                  wgb_ref, ba_ref, bx_ref, lam_ref, h0_ref,
                  pm_ref, pc_ref, pw_ref, ps_ref, wr_ref, wp_ref, wo_ref,
                  npost_ref, npre_ref, g1_ref, sh2_ref, sc2_ref, rw_ref, rb_ref,
                  xmid_ref, v_ref, code_ref, wts_ref, cnt_ref,
                  hb_s, h_s, cnt_s, w3_s, wtmp_s, wsem):
    i = pl.program_id(0)
    bsz = x_ref.shape[0]

    @pl.when(i == 0)
    def _():
        for k, w_hbm in enumerate((wr_ref, wp_ref, wo_ref)):
            _load_cast(w_hbm, w3_s.at[k], wtmp_s, wsem)
        h_s[...] = h0_ref[...]
        cnt_s[...] = jnp.zeros_like(cnt_s)

    for bi in range(bsz):
        _mix_bwd_row(bi, hf_ref, xc_ref, gg_ref, xp_ref, mr_ref, mp_ref, x_ref,
                     wgb_ref, ba_ref, bx_ref, lam_ref, pm_ref, pc_ref, pw_ref, ps_ref,
                     npost_ref, npre_ref, g1_ref, sh2_ref, sc2_ref, rw_ref, rb_ref,
                     xmid_ref, v_ref, code_ref, wts_ref, cnt_ref, hb_s, h_s, cnt_s, w3_s)


def _mix_bwd_row(bi, hf_ref, xc_ref, gg_ref, xp_ref, mr_ref, mp_ref, x_ref,
                 wgb_ref, ba_ref, bx_ref, lam_ref, pm_ref, pc_ref, pw_ref, ps_ref,
                 npost_ref, npre_ref, g1_ref, sh2_ref, sc2_ref, rw_ref, rb_ref,
                 xmid_ref, v_ref, code_ref, wts_ref, cnt_ref, hb_s, h_s, cnt_s, w3_s):
    tt, d = x_ref.shape[1], x_ref.shape[2]
    n_exp = rw_ref.shape[1]

    xc = xc_ref[bi].astype(F32)
    sp = _softplus(-lam_ref[1:2, :])
    a, b = _lru_coeffs(xc, wgb_ref, ba_ref[1:2, :], bx_ref[1:2, :], sp)

    def store(gi, hg):
        hb_s[bi, gi * SUBLANES:(gi + 1) * SUBLANES, :] = hg

    h_s[bi] = _scan_tile(a, b, h_s[bi], True, store)

    gg = gg_ref[bi].astype(F32)
    gelu = gg * _sigmoid(gg * (1.5957691216057308 + 0.07135481627260025 * (gg * gg)))
    y_rnn = (hf_ref[bi] + hb_s[bi]) * gelu

    xpb = xp_ref[bi]
    grp = d // len(POOL_WINDOWS)
    y_parts = []
    for gi in range(len(POOL_WINDOWS)):
        cols = slice(gi * grp, (gi + 1) * grp)
        rows = []
        for c0 in range(0, tt, POOL_CHUNK):
            xg = xpb[c0:c0 + POOL_CHUNK, cols]
            mean = _dot(pm_ref[gi], xg) / pc_ref[gi]
            rows.append((mean - xg.astype(F32)).astype(BF16))
        dg = rows[0] if len(rows) == 1 else jnp.concatenate(rows, axis=0)
        y_parts.append(_dot(dg, pw_ref[gi]) * ps_ref[:, cols])
    y_pool = jnp.concatenate(y_parts, axis=1)

    merged = (_sigmoid(mr_ref[bi].astype(F32)) * _dot(y_rnn.astype(BF16), w3_s[0])
              + _sigmoid(mp_ref[bi].astype(F32)) * _dot(y_pool.astype(BF16), w3_s[1]))
    m_lat = _dot(merged.astype(BF16), w3_s[2])
    x_mid = x_ref[bi] + g1_ref[bi] * _rmsnorm(m_lat, npost_ref[...])
    xmid_ref[bi] = x_mid
    v = _modulate(_rmsnorm(x_mid, npre_ref[...]), sh2_ref[bi], sc2_ref[bi])
    v_ref[bi] = v.reshape(tt, SUBLANES, LANES)

    logits = _dot3(v, rw_ref[...]) + rb_ref[...]
    lane = lax.broadcasted_iota(jnp.int32, (tt, n_exp), 1).astype(F32)
    work = logits
    vals, idxs, sels = [], [], []
    for _ in range(TOP_K):
        m = jnp.max(work, axis=-1, keepdims=True)
        idx = jnp.min(jnp.where(work == m, lane, float(n_exp)), axis=-1, keepdims=True)
        sel = lane == idx
        vals.append(m)
        idxs.append(idx)
        sels.append(sel)
        work = jnp.where(sel, -jnp.inf, work)
    exps = [jnp.exp(vk - vals[0]) for vk in vals]
    den = exps[0] + exps[1] + exps[2] + exps[3]
    anyf = jnp.zeros((tt, n_exp), F32)
    for sel in sels:
        anyf = anyf + sel.astype(F32)
    r_i = lax.broadcasted_iota(jnp.int32, (tt, tt), 0)
    c_i = lax.broadcasted_iota(jnp.int32, (tt, tt), 1)
    lower = (c_i < r_i).astype(BF16)
    before = _dot(lower, anyf.astype(BF16)) + cnt_s[...]
    cnt_new = cnt_s[...] + jnp.sum(anyf, axis=0, keepdims=True)
    cnt_s[...] = cnt_new
    cnt_ref[...] = cnt_new

    lane_o = lax.broadcasted_iota(jnp.int32, (tt, LANES), 1)
    code_o = jnp.zeros((tt, LANES), jnp.int32)
    wts_o = jnp.zeros((tt, LANES), F32)
    for k in range(TOP_K):
        rk = jnp.sum(jnp.where(sels[k], before, 0.0), axis=-1, keepdims=True)
        code = idxs[k].astype(jnp.int32) * RANK_RADIX + rk.astype(jnp.int32)
        code_o = jnp.where(lane_o == k, code, code_o)
        wts_o = jnp.where(lane_o == k, exps[k] / den, wts_o)
    code_ref[bi] = code_o
    wts_ref[bi] = wts_o


def _mix_bwd_call(hf, xc, gg, xp, mr, mp, x, wgb, ba, bx, lam, h0b, pm, pc, pw, ps, wr, wp, wo,
                  npost, npre, g1, sh2, sc2, rw, rb):
    bsz, t, d = x.shape
    nt = t // TT
    n_exp = rw.shape[1]
    rev = lambda i: (0, nt - 1 - i, 0)
    tile = pl.BlockSpec((bsz, TT, d), rev)
    small = pl.BlockSpec((bsz, TT, LANES), rev)
    per_b = pl.BlockSpec((bsz, 1, d), lambda i: (0, 0, 0))
    full = lambda shp: pl.BlockSpec(shp, lambda i: (0,) * len(shp))
    return pl.pallas_call(
        _mix_bwd_body,
        out_shape=(jax.ShapeDtypeStruct((bsz, t, d), F32),
                   jax.ShapeDtypeStruct((bsz, t, SUBLANES, LANES), F32),
                   jax.ShapeDtypeStruct((bsz, t, LANES), jnp.int32),
                   jax.ShapeDtypeStruct((bsz, t, LANES), F32),
                   jax.ShapeDtypeStruct((1, n_exp), F32)),
        grid=(nt,),
        in_specs=[tile] * 7 + [full(wgb.shape), full(ba.shape), full(bx.shape), full(lam.shape), per_b,
                               full(pm.shape), full(pc.shape), full(pw.shape), full((1, d)),
                               pl.BlockSpec(memory_space=pl.ANY), pl.BlockSpec(memory_space=pl.ANY),
                               pl.BlockSpec(memory_space=pl.ANY),
                               full((1, d)), full((1, d)), per_b, per_b, per_b,
                               full(rw.shape), full((1, n_exp))],
        out_specs=(tile, pl.BlockSpec((bsz, TT, SUBLANES, LANES), lambda i: (0, nt - 1 - i, 0, 0)),
                   small, small, full((1, n_exp))),
        scratch_shapes=[pltpu.VMEM((bsz, TT, d), F32), pltpu.VMEM((bsz, 1, d), F32), pltpu.VMEM((1, n_exp), F32),
                        pltpu.VMEM((3, d, d), BF16), pltpu.VMEM((d, d), F32), pltpu.SemaphoreType.DMA],
        compiler_params=pltpu.CompilerParams(
            dimension_semantics=("arbitrary",), vmem_limit_bytes=VMEM_LIMIT),
        name="mix_bwd",
    )(hf, xc, gg, xp, mr, mp, x, wgb, ba, bx, lam, h0b, pm, pc, pw, ps, wr, wp, wo,
      npost, npre, g1, sh2, sc2, rw, rb)


def _dispatch_body(n_pad_blocks, *refs):
    pos_refs = refs[:TOP_K]
    pad_ref, zero_ref, v_hbm, xs_hbm, vbuf, lsem, ssem = refs[TOP_K:]
    i = pl.program_id(0)
    steps = pl.num_programs(0)
    n_pad = pad_ref.shape[0]
    slot = lax.rem(i, DISPATCH_BUFS)

    def load(step):
        s = lax.rem(step, DISPATCH_BUFS)
        return pltpu.make_async_copy(v_hbm.at[pl.ds(step * TD, TD)], vbuf.at[s], lsem.at[s])

    def row_copy(s, src_row, dst_row):
        return pltpu.make_async_copy(vbuf.at[s, src_row], xs_hbm.at[dst_row], ssem.at[s])

    def pad_copy(s, dst_row):
        return pltpu.make_async_copy(zero_ref.at[0], xs_hbm.at[dst_row], ssem.at[s])

    def wait_scatter(step):
        s = lax.rem(step, DISPATCH_BUFS)

        def body(t, carry):
            for _ in range(TOP_K):
                row_copy(s, 0, 0).wait()
            return carry

        lax.fori_loop(0, TD, body, 0, unroll=ISSUE_UNROLL)

        @pl.when(step < n_pad_blocks)
        def _():
            def body_pad(q, carry):
                pad_copy(s, 0).wait()
                return carry

            lax.fori_loop(0, n_pad, body_pad, 0, unroll=ISSUE_UNROLL)

    @pl.when(i == 0)
    def _():
        load(0).start()

    @pl.when(i >= 2)
    def _():
        wait_scatter(i - 2)

    @pl.when(i + 1 < steps)
    def _():
        load(i + 1).start()

    load(i).wait()

    def issue(t, carry):
        for k in range(TOP_K):
            row_copy(slot, t, pos_refs[k][t]).start(priority=k % DMA_PRIORITIES)
        return carry

    lax.fori_loop(0, TD, issue, 0, unroll=ISSUE_UNROLL)

    @pl.when(i < n_pad_blocks)
    def _():
        def issue_pad(q, carry):
            pad_copy(slot, pad_ref[q]).start()
            return carry

        lax.fori_loop(0, n_pad, issue_pad, 0, unroll=ISSUE_UNROLL)

    @pl.when(i == steps - 1)
    def _():
        @pl.when(i >= 1)
        def _():
            wait_scatter(i - 1)

        wait_scatter(i)


def _pos_specs(steps, step_of):
    return [pl.BlockSpec((TD,), lambda i, k=k: (k * steps + step_of(i),), memory_space=pltpu.SMEM)
            for k in range(TOP_K)]


def _dispatch_call(pos_kmajor, pad_slots, v3, n_slots):
    n = v3.shape[0]
    slab = v3.shape[1:]
    steps = n // TD
    n_pad_blocks = pad_slots.shape[0] // PAD_BLOCK
    assert n_pad_blocks <= steps
    return pl.pallas_call(
        functools.partial(_dispatch_body, n_pad_blocks),
        out_shape=jax.ShapeDtypeStruct((n_slots,) + slab, F32),
        grid=(steps,),
        in_specs=_pos_specs(steps, lambda i: i) + [
            pl.BlockSpec((PAD_BLOCK,), lambda i: (jnp.minimum(i, n_pad_blocks - 1),), memory_space=pltpu.SMEM),
            pl.BlockSpec((1,) + slab, lambda i: (0, 0, 0)),
            pl.BlockSpec(memory_space=pl.ANY)],
        out_specs=pl.BlockSpec(memory_space=pl.ANY),
        scratch_shapes=[pltpu.VMEM((DISPATCH_BUFS, TD) + slab, F32),
                        pltpu.SemaphoreType.DMA((DISPATCH_BUFS,)), pltpu.SemaphoreType.DMA((DISPATCH_BUFS,))],
        compiler_params=pltpu.CompilerParams(dimension_semantics=("arbitrary",)),
        name="dispatch",
    )(*([pos_kmajor] * TOP_K), pad_slots, jnp.zeros((1,) + slab, F32), v3)


def _expert_body(te_ref, blk_ref, nvalid_ref, slot_ref, nxt_ref,
                 xs_ref, w1_hbm, b1_ref, w2_hbm, b2_ref, y_ref, w1f_s, w2f_s, w1b_s, w2b_s, sem):
    j = pl.program_id(0)
    n_exp = w1_hbm.shape[0]
    ff = w2_hbm.shape[1]
    e = te_ref[j]
    prev = te_ref[jnp.maximum(j - 1, 0)]

    def weight_copies(expert, slot):
        return (pltpu.make_async_copy(w1_hbm.at[expert], w1f_s.at[slot], sem.at[slot, 0]),
                pltpu.make_async_copy(w2_hbm.at[expert], w2f_s.at[slot], sem.at[slot, 1]))

    @pl.when(j == 0)
    def _():
        for c in weight_copies(e, slot_ref[e]):
            c.start()

    @pl.when((j == 0) | (e != prev))
    def _():
        slot = slot_ref[e]
        for c in weight_copies(e, slot):
            c.wait()
        w1b_s[...] = w1f_s[slot].astype(BF16)
        w2b_s[...] = w2f_s[slot].astype(BF16)

        @pl.when(nxt_ref[e] < n_exp)
        def _():
            for c in weight_copies(nxt_ref[e], 1 - slot):
                c.start()

    @pl.when(j < nvalid_ref[0])
    def _():
        tm = xs_ref.shape[0]
        x = xs_ref[...].reshape(tm, w1_hbm.shape[1])
        z = _dot(x.astype(BF16), w1b_s[...]) + b1_ref[0]
        glu = jnp.minimum(z[:, :ff], SWIGLU_LIMIT)
        lin = jnp.clip(z[:, ff:], -SWIGLU_LIMIT, SWIGLU_LIMIT)
        act = glu * _sigmoid(SWIGLU_ALPHA * glu) * (lin + 1.0)
        y = _dot(act.astype(BF16), w2b_s[...]) + b2_ref[0]
        y_ref[...] = y.reshape(y_ref.shape)

    @pl.when(j >= nvalid_ref[0])
    def _():
        y_ref[...] = jnp.zeros_like(y_ref)


def _expert_call(te, blk, nvalid, slot, nxt, xs, w1, b1, w2, b2):
    n_slots = xs.shape[0]
    slab = xs.shape[1:]
    n_exp, d, ff2 = w1.shape
    ff = w2.shape[1]
    grid_spec = pltpu.PrefetchScalarGridSpec(
        num_scalar_prefetch=5,
        grid=(n_slots // TM,),
        in_specs=[pl.BlockSpec((TM,) + slab, lambda j, te, blk, *_: (blk[j], 0, 0)),
                  pl.BlockSpec(memory_space=pl.ANY),
                  pl.BlockSpec((1, 1, ff2), lambda j, te, *_: (te[j], 0, 0)),
                  pl.BlockSpec(memory_space=pl.ANY),
                  pl.BlockSpec((1, 1, d), lambda j, te, *_: (te[j], 0, 0))],
        out_specs=pl.BlockSpec((TM,) + slab, lambda j, *_: (j, 0, 0)),
        scratch_shapes=[pltpu.VMEM((2, d, ff2), F32), pltpu.VMEM((2, ff, d), F32),
                        pltpu.VMEM((d, ff2), BF16), pltpu.VMEM((ff, d), BF16),
                        pltpu.SemaphoreType.DMA((2, 2))],
    )
    return pl.pallas_call(
        _expert_body,
        out_shape=jax.ShapeDtypeStruct((n_slots,) + slab, F32),
        grid_spec=grid_spec,
        compiler_params=pltpu.CompilerParams(
            dimension_semantics=("arbitrary",), vmem_limit_bytes=VMEM_LIMIT),
        name="experts",
    )(te, blk, nvalid, slot, nxt, xs, w1, b1, w2, b2)


def _combine_body(*refs):
    pos_refs, posn_refs = refs[:TOP_K], refs[TOP_K:2 * TOP_K]
    wts_ref, xmid_ref, g2_ref, npost_ref, y_hbm, o_ref, buf, sem = refs[2 * TOP_K:]
    i = pl.program_id(0)
    cur = lax.rem(i, 2)

    def row_copy(p_refs, slot, t, k):
        return pltpu.make_async_copy(y_hbm.at[p_refs[k][t]], buf.at[slot, k, t], sem.at[slot])

    def issue(p_refs, slot):
        def body(t, carry):
            for k in range(TOP_K):
                row_copy(p_refs, slot, t, k).start(priority=k % DMA_PRIORITIES)
            return carry

        lax.fori_loop(0, TD, body, 0, unroll=ISSUE_UNROLL)

    @pl.when(i == 0)
    def _():
        issue(pos_refs, 0)

    @pl.when(i + 1 < pl.num_programs(0))
    def _():
        issue(posn_refs, 1 - cur)

    def drain(t, carry):
        for k in range(TOP_K):
            row_copy(pos_refs, cur, t, k).wait()
        return carry

    lax.fori_loop(0, TD, drain, 0, unroll=ISSUE_UNROLL)

    w = wts_ref[...]
    f = w[:, 0:1] * buf[cur, 0].reshape(xmid_ref.shape)
    for k in range(1, TOP_K):
        f = f + w[:, k:k + 1] * buf[cur, k].reshape(xmid_ref.shape)
    o_ref[...] = xmid_ref[...] + g2_ref[0] * _rmsnorm(f, npost_ref[...])


def _combine_call(pos_kmajor, wts, xmid2, g2, npost, ys, t_per_batch):
    n, d = xmid2.shape
    steps = n // TD
    return pl.pallas_call(
        _combine_body,
        out_shape=jax.ShapeDtypeStruct((n, d), F32),
        grid=(steps,),
        in_specs=_pos_specs(steps, lambda i: i) + _pos_specs(steps, lambda i: jnp.minimum(i + 1, steps - 1)) + [
                  pl.BlockSpec((TD, LANES), lambda i: (i, 0)),
                  pl.BlockSpec((TD, d), lambda i: (i, 0)),
                  pl.BlockSpec((1, 1, d), lambda i: ((i * TD) // t_per_batch, 0, 0)),
                  pl.BlockSpec((1, d), lambda i: (0, 0)),
                  pl.BlockSpec(memory_space=pl.ANY)],
        out_specs=pl.BlockSpec((TD, d), lambda i: (i, 0)),
        scratch_shapes=[pltpu.VMEM((2, TOP_K, TD) + ys.shape[1:], F32), pltpu.SemaphoreType.DMA((2,))],
        compiler_params=pltpu.CompilerParams(dimension_semantics=("arbitrary",)),
        name="combine",
    )(*([pos_kmajor] * (2 * TOP_K)), wts, xmid2, g2, npost, ys)


def _count_le(ends, q):
    return jnp.sum((ends[None, :] <= q[:, None]).astype(jnp.int32), axis=1)


def _gate_weights(wa, wx):
    return jnp.concatenate([wa, wx], axis=-1).astype(BF16)


def kernel(x, c, ctx, c_ctx, w_mod, b_mod, norm_pre_mix, norm_post_mix, norm_pre_ffn, norm_post_ffn, w_in, conv_w, conv_b, lru_wa, lru_ba, lru_wx, lru_bx, lru_lambda, pool_w, pool_scale, w_rnn_proj, w_pool_proj, w_out, router_w, router_b, exp_w1, exp_b1, exp_w2, exp_b2):
    bsz, t, d = x.shape
    assert w_mod.shape[0] == 1, "single-layer block"
    assert t % TT == 0 and TT % POOL_CHUNK == 0 and POOL_CHUNK % GRID_W == 0 and t % TD == 0
    assert d == SUBLANES * LANES, "a token row must be exactly one (8, 128) slab"
    n = bsz * t
    n_exp = router_w.shape[-1]
    assert (n * TOP_K) % TM == 0 and (n_exp * TM) % PAD_BLOCK == 0 and n <= RANK_RADIX
    row = lambda a: a.reshape(1, -1)

    pad = jnp.zeros((SUBLANES - (bsz + 1) % SUBLANES, d), F32)
    cc = jnp.concatenate([c, c_ctx[None, :], pad], axis=0)
    mod = _mod_call(cc, w_mod[0], row(b_mod[0]))
    lat = lambda j: mod[:bsz, j * d:(j + 1) * d].reshape(bsz, 1, d)
    sh1, sc1, g1, sh2, sc2, g2 = (lat(j) for j in range(6))
    csh1, csc1 = mod[bsz:bsz + 1, 0:d], mod[bsz:bsz + 1, d:2 * d]

    wgf = _gate_weights(lru_wa[0, 0], lru_wx[0, 0])
    wgb = _gate_weights(lru_wa[0, 1], lru_wx[0, 1])
    ba, bx, lam = lru_ba[0], lru_bx[0], lru_lambda[0]
    cw, cb = conv_w[0], row(conv_b[0])
    npre_mix = row(norm_pre_mix[0])

    h0f, h0b = _ctx_call(ctx, npre_mix, csh1, csc1, w_in[0], cw, cb, wgf, wgb, ba, bx, lam)

    hf, xc, gg, xp, mr, mp = _mix_fwd_call(x, npre_mix, sh1, sc1, w_in[0], cw, cb, wgf, ba, bx, lam, h0f)

    pm, pc = _pool_consts()
    xmid, v, code_o, wts_o, cnt = _mix_bwd_call(
        hf, xc, gg, xp, mr, mp, x, wgb, ba, bx, lam, h0b,
        pm, pc, pool_w[0].astype(BF16), row(pool_scale[0]),
        w_rnn_proj[0], w_pool_proj[0], w_out[0],
        row(norm_post_mix[0]), row(norm_pre_ffn[0]), g1, sh2, sc2, router_w[0], row(router_b[0]))

    code = code_o.reshape(n, LANES)[:, :TOP_K].T
    idx, rank = code // RANK_RADIX, code % RANK_RADIX
    counts = cnt[0].astype(jnp.int32)
    tiles_e = (counts + TM - 1) // TM
    tile_end = jnp.cumsum(tiles_e)
    offs = (tile_end - tiles_e) * TM
    e_ids = jnp.arange(n_exp, dtype=jnp.int32)
    off_of = jnp.sum(jnp.where(idx[None] == e_ids[:, None, None], offs[:, None, None], 0), axis=0)
    pos_kmajor = (off_of + rank).reshape(TOP_K * n)
    n_tiles = (n * TOP_K) // TM + n_exp
    n_valid = tile_end[-1]
    jj = jnp.arange(n_tiles, dtype=jnp.int32)
    blk = jnp.minimum(jj, n_valid - 1)
    te = jnp.minimum(_count_le(tile_end, blk), n_exp - 1)

    pad_e = tiles_e * TM - counts
    pad_end = jnp.cumsum(pad_e)
    qq = jnp.arange(n_exp * TM, dtype=jnp.int32)
    pad_start = pad_end - pad_e
    inside = (qq[None, :] >= pad_start[:, None]) & (qq[None, :] < pad_end[:, None])
    slot_in = jnp.sum(jnp.where(inside, (offs + counts - pad_start)[:, None] + qq[None, :], 0), axis=0)
    slot_tail = n_valid * TM + qq - pad_end[-1]
    pad_slots = jnp.where(qq < pad_end[-1], slot_in, slot_tail).astype(jnp.int32)

    xs = _dispatch_call(pos_kmajor, pad_slots, v.reshape((n,) + v.shape[2:]), n_tiles * TM)
    has = tiles_e > 0
    w_slot = ((jnp.cumsum(has.astype(jnp.int32)) - has.astype(jnp.int32)) % 2).astype(jnp.int32)
    later = (e_ids[None, :] > e_ids[:, None]) & has[None, :]
    w_next = jnp.min(jnp.where(later, e_ids[None, :], n_exp), axis=1).astype(jnp.int32)
    ys = _expert_call(te, blk, n_valid.reshape(1), w_slot, w_next, xs, exp_w1[0],
                      exp_b1[0].reshape(n_exp, 1, -1), exp_w2[0], exp_b2[0].reshape(n_exp, 1, -1))
    out = _combine_call(pos_kmajor, wts_o.reshape(n, LANES), xmid.reshape(n, d), g2, row(norm_post_ffn[0]), ys, t)
    return out.reshape(bsz, t, d)
```

```python
import functools

import numpy as np
import jax
import jax.numpy as jnp
from jax import lax
from jax.experimental import pallas as pl
from jax.experimental.pallas import tpu as pltpu

F32 = jnp.float32
BF16 = jnp.bfloat16

RNN_HEADS = 4
CONV_W = 4
LRU_C = 8.0
POOL_WINDOWS = (2, 4, 8, 16)
GRID_W = 64
TOP_K = 4
SWIGLU_LIMIT = 7.0
SWIGLU_ALPHA = 1.702
EPS = 1e-6
SQRT_FLOOR = 1e-30

SUBLANES = 8
LANES = 128
VMEM_LIMIT = 56 * 1024 * 1024

TT = 256
POOL_CHUNK = 256
TM = 512
TD = 128
PAD_BLOCK = 128
DISPATCH_BUFS = 3
ISSUE_UNROLL = 8
DMA_PRIORITIES = 2
RANK_RADIX = 1 << 16


def _sigmoid(x):
    return 1.0 / (1.0 + jnp.exp(-x))


def _softplus(z):
    return jnp.maximum(z, 0.0) + jnp.log1p(jnp.exp(-jnp.abs(z)))


def _sqrt_nonneg(x):
    return x * lax.rsqrt(jnp.maximum(x, SQRT_FLOOR))


def _rmsnorm(x, g):
    ms = jnp.mean(x * x, axis=-1, keepdims=True)
    return (x * lax.rsqrt(ms + EPS)) * g


def _modulate(u, shift, scale):
    return u * (1.0 + scale) + shift


def _dot(a, b):
    return jnp.dot(a, b, preferred_element_type=F32)


def _split(x):
    hi = x.astype(BF16)
    return hi, (x - hi.astype(F32)).astype(BF16)


def _dot3(a, b):
    ah, al = _split(a)
    bh, bl = _split(b)
    return _dot(ah, bh) + (_dot(al, bh) + _dot(ah, bl))


def _load_cast(w_hbm, dst_s, tmp_s, sem):
    c = tmp_s.shape[1]
    for j in range(w_hbm.shape[1] // c):
        cp = pltpu.make_async_copy(w_hbm.at[:, pl.ds(j * c, c)], tmp_s, sem)
        cp.start()
        cp.wait()
        dst_s[:, j * c:(j + 1) * c] = tmp_s[...].astype(BF16)


def _conv(prev8, xr, next8, cw, cb):
    t = xr.shape[0]
    ext = jnp.concatenate([prev8, xr, next8], axis=0)
    acc = cb + cw[0:1, :] * ext[SUBLANES - 2:SUBLANES - 2 + t, :]
    for k in range(1, CONV_W):
        off = SUBLANES - 2 + k
        acc = acc + cw[k:k + 1, :] * ext[off:off + t, :]
    return acc


def _lru_coeffs(xc, wg_ref, ba, bx, sp):
    hd = xc.shape[1] // RNN_HEADS
    xcb = xc.astype(BF16)
    a_parts, b_parts = [], []
    for h in range(RNN_HEADS):
        cols = slice(h * hd, (h + 1) * hd)
        z = _dot(xcb[:, cols], wg_ref[h])
        r = _sigmoid(z[:, :hd] + ba[:, cols])
        i = _sigmoid(z[:, hd:] + bx[:, cols])
        a = jnp.exp((-LRU_C) * r * sp[:, cols])
        b = _sqrt_nonneg(1.0 - a * a) * (i * xc[:, cols])
        a_parts.append(a)
        b_parts.append(b)
    return jnp.concatenate(a_parts, axis=1), jnp.concatenate(b_parts, axis=1)


def _scan_tile(a, b, h_in, reverse, store):
    t, c = a.shape
    g = t // SUBLANES
    a3 = a.reshape(g, SUBLANES, c)
    b3 = b.reshape(g, SUBLANES, c)
    row = lax.broadcasted_iota(jnp.int32, (g, SUBLANES, c), 1)
    for s in (1, 2, 4):
        if reverse:
            shift, m = SUBLANES - s, row < SUBLANES - s
        else:
            shift, m = s, row >= s
        ra = pltpu.roll(a3, shift, axis=1)
        rb = pltpu.roll(b3, shift, axis=1)
        b3 = a3 * jnp.where(m, rb, 0.0) + b3
        a3 = a3 * jnp.where(m, ra, 1.0)
    h = h_in
    order = range(g - 1, -1, -1) if reverse else range(g)
    for gi in order:
        hg = a3[gi] * h + b3[gi]
        store(gi, hg)
        h = hg[0:1, :] if reverse else hg[SUBLANES - 1:SUBLANES, :]
    return h


def _mod_body(c_ref, w_ref, b_ref, o_ref):
    c = c_ref[...]
    s = c * _sigmoid(c)
    o_ref[...] = _dot3(s, w_ref[...]) + b_ref[...]


def _mod_call(cc, w_mod, b_mod):
    d = cc.shape[1]
    n = w_mod.shape[1] // d
    return pl.pallas_call(
        _mod_body,
        out_shape=jax.ShapeDtypeStruct((cc.shape[0], n * d), F32),
        grid=(n,),
        in_specs=[pl.BlockSpec(cc.shape, lambda j: (0, 0)),
                  pl.BlockSpec((d, d), lambda j: (0, j)),
                  pl.BlockSpec((1, d), lambda j: (0, j))],
        out_specs=pl.BlockSpec((cc.shape[0], d), lambda j: (0, j)),
        name="mod",
    )(cc, w_mod, b_mod)


def _ctx_body(ctx_ref, g_ref, sh_ref, sc_ref, w_ref, cw_ref, cb_ref, wgf_ref, wgb_ref,
              ba_ref, bx_ref, lam_ref, hf_ref, hb_ref):
    d = ctx_ref.shape[2]
    u = _modulate(_rmsnorm(ctx_ref[0], g_ref[...]), sh_ref[...], sc_ref[...])
    xr = _dot(u.astype(BF16), w_ref[...].astype(BF16))
    z8 = jnp.zeros((SUBLANES, d), F32)
    xc = _conv(z8, xr, z8, cw_ref[...], cb_ref[...])
    h0 = jnp.zeros((1, d), F32)
    for di, (wg_ref, out_ref) in enumerate(((wgf_ref, hf_ref), (wgb_ref, hb_ref))):
        sp = _softplus(-lam_ref[di:di + 1, :])
        a, b = _lru_coeffs(xc, wg_ref, ba_ref[di:di + 1, :], bx_ref[di:di + 1, :], sp)
        out_ref[0] = _scan_tile(a, b, h0, di == 1, lambda gi, hg: None)


def _ctx_call(ctx, g, sh, sc, w_in, cw, cb, wgf, wgb, ba, bx, lam):
    bsz, tc, d = ctx.shape
    full = lambda shp: pl.BlockSpec(shp, lambda b: (0,) * len(shp))
    return pl.pallas_call(
        _ctx_body,
        out_shape=(jax.ShapeDtypeStruct((bsz, 1, d), F32),) * 2,
        grid=(bsz,),
        in_specs=[pl.BlockSpec((1, tc, d), lambda b: (b, 0, 0)),
                  full((1, d)), full((1, d)), full((1, d)),
                  pl.BlockSpec((d, d), lambda b: (0, 0)),
                  full(cw.shape), full((1, d)), full(wgf.shape), full(wgb.shape),
                  full(ba.shape), full(bx.shape), full(lam.shape)],
        out_specs=(pl.BlockSpec((1, 1, d), lambda b: (b, 0, 0)),) * 2,
        compiler_params=pltpu.CompilerParams(vmem_limit_bytes=VMEM_LIMIT),
        name="ctx",
    )(ctx, g, sh, sc, w_in, cw, cb, wgf, wgb, ba, bx, lam)


def _mix_fwd_body(x_ref, xn_ref, g_ref, sh_ref, sc_ref, win_ref, cw_ref, cb_ref, wgf_ref,
                  ba_ref, bx_ref, lam_ref, h0_ref,
                  hf_ref, xc_ref, gg_ref, xp_ref, mr_ref, mp_ref, tail_s, h_s, win_s, wtmp_s, wsem):
    i = pl.program_id(0)
    last = pl.num_programs(0) - 1
    bsz, tt, d = x_ref.shape

    @pl.when(i == 0)
    def _():
        _load_cast(win_ref, win_s, wtmp_s, wsem)
        tail_s[...] = jnp.zeros_like(tail_s)
        h_s[...] = h0_ref[...]

    g = g_ref[...]
    w_rnn = win_s[:, 0:d]
    sp = _softplus(-lam_ref[0:1, :])
    for bi in range(bsz):
        sh, sc = sh_ref[bi], sc_ref[bi]
        ub = _modulate(_rmsnorm(x_ref[bi], g), sh, sc).astype(BF16)
        unb = _modulate(_rmsnorm(xn_ref[bi], g), sh, sc).astype(BF16)
        xr = _dot(ub, w_rnn)
        xrn = jnp.where(i == last, 0.0, _dot(unb, w_rnn))
        xc = _conv(tail_s[bi], xr, xrn, cw_ref[...], cb_ref[...])
        tail_s[bi] = xr[tt - SUBLANES:tt, :]
        xc_ref[bi] = xc.astype(BF16)
        for j, ref in enumerate((gg_ref, xp_ref, mr_ref, mp_ref)):
            ref[bi] = _dot(ub, win_s[:, (j + 1) * d:(j + 2) * d]).astype(BF16)

        a, b = _lru_coeffs(xc, wgf_ref, ba_ref[0:1, :], bx_ref[0:1, :], sp)

        def store(gi, hg, bi=bi):
            hf_ref[bi, gi * SUBLANES:(gi + 1) * SUBLANES, :] = hg

        h_s[bi] = _scan_tile(a, b, h_s[bi], False, store)


def _mix_fwd_call(x, g, sh, sc, w_in, cw, cb, wgf, ba, bx, lam, h0f):
    bsz, t, d = x.shape
    nt = t // TT
    nblk8 = t // SUBLANES
    per_b = pl.BlockSpec((bsz, 1, d), lambda i: (0, 0, 0))
    full = lambda shp: pl.BlockSpec(shp, lambda i: (0,) * len(shp))
    tile = pl.BlockSpec((bsz, TT, d), lambda i: (0, i, 0))
    return pl.pallas_call(
        _mix_fwd_body,
        out_shape=(jax.ShapeDtypeStruct((bsz, t, d), F32),) + (jax.ShapeDtypeStruct((bsz, t, d), BF16),) * 5,
        grid=(nt,),
        in_specs=[tile,
                  pl.BlockSpec((bsz, SUBLANES, d),
                               lambda i: (0, jnp.minimum((i + 1) * (TT // SUBLANES), nblk8 - 1), 0)),
                  full((1, d)), per_b, per_b,
                  pl.BlockSpec(memory_space=pl.ANY), full(cw.shape), full((1, d)), full(wgf.shape),
                  full(ba.shape), full(bx.shape), full(lam.shape), per_b],
        out_specs=(tile,) * 6,
        scratch_shapes=[pltpu.VMEM((bsz, SUBLANES, d), F32), pltpu.VMEM((bsz, 1, d), F32),
                        pltpu.VMEM(w_in.shape, BF16), pltpu.VMEM((d, d), F32), pltpu.SemaphoreType.DMA],
        compiler_params=pltpu.CompilerParams(
            dimension_semantics=("arbitrary",), vmem_limit_bytes=VMEM_LIMIT),
        name="mix_fwd",
    )(x, x, g, sh, sc, w_in, cw, cb, wgf, ba, bx, lam, h0f)


def _pool_consts():
    p = np.arange(POOL_CHUNK)
    pos, line = p % GRID_W, p // GRID_W
    mats, cnts = [], []
    for w in POOL_WINDOWS:
        lo = np.clip(pos - w // 2, 0, GRID_W)
        hi = np.clip(pos + w - w // 2, 0, GRID_W)
        m = (line[:, None] == line[None, :]) & (pos[None, :] >= lo[:, None]) & (pos[None, :] < hi[:, None])
        mats.append(m.astype(np.float32))
        cnts.append((hi - lo).astype(np.float32)[:, None])
    return jnp.asarray(np.stack(mats), BF16), jnp.asarray(np.stack(cnts), F32)


def _mix_bwd_body(hf_ref, xc_ref, gg_ref, xp_ref, mr_ref, mp_ref, x_ref,
                  wgb_ref, ba_ref, bx_ref, lam_ref, h0_ref,
                  pm_ref, pc_ref, pw_ref, ps_ref, wr_ref, wp_ref, wo_ref,
                  npost_ref, npre_ref, g1_ref, sh2_ref, sc2_ref, rw_ref, rb_ref,
                  xmid_ref, v_ref, code_ref, wts_ref, cnt_ref,
                  hb_s, h_s, cnt_s, w3_s, wtmp_s, wsem):
    i = pl.program_id(0)
    bsz = x_ref.shape[0]

    @pl.when(i == 0)
    def _():
        for k, w_hbm in enumerate((wr_ref, wp_ref, wo_ref)):
            _load_cast(w_hbm, w3_s.at[k], wtmp_s, wsem)
        h_s[...] = h0_ref[...]
        cnt_s[...] = jnp.zeros_like(cnt_s)

    for bi in range(bsz):
        _mix_bwd_row(bi, hf_ref, xc_ref, gg_ref, xp_ref, mr_ref, mp_ref, x_ref,
                     wgb_ref, ba_ref, bx_ref, lam_ref, pm_ref, pc_ref, pw_ref, ps_ref,
                     npost_ref, npre_ref, g1_ref, sh2_ref, sc2_ref, rw_ref, rb_ref,
                     xmid_ref, v_ref, code_ref, wts_ref, cnt_ref, hb_s, h_s, cnt_s, w3_s)


def _mix_bwd_row(bi, hf_ref, xc_ref, gg_ref, xp_ref, mr_ref, mp_ref, x_ref,
                 wgb_ref, ba_ref, bx_ref, lam_ref, pm_ref, pc_ref, pw_ref, ps_ref,
                 npost_ref, npre_ref, g1_ref, sh2_ref, sc2_ref, rw_ref, rb_ref,
                 xmid_ref, v_ref, code_ref, wts_ref, cnt_ref, hb_s, h_s, cnt_s, w3_s):
    tt, d = x_ref.shape[1], x_ref.shape[2]
    n_exp = rw_ref.shape[1]

    xc = xc_ref[bi].astype(F32)
    sp = _softplus(-lam_ref[1:2, :])
    a, b = _lru_coeffs(xc, wgb_ref, ba_ref[1:2, :], bx_ref[1:2, :], sp)

    def store(gi, hg):
        hb_s[bi, gi * SUBLANES:(gi + 1) * SUBLANES, :] = hg

    h_s[bi] = _scan_tile(a, b, h_s[bi], True, store)

    gg = gg_ref[bi].astype(F32)
    gelu = gg * _sigmoid(gg * (1.5957691216057308 + 0.07135481627260025 * (gg * gg)))
    y_rnn = (hf_ref[bi] + hb_s[bi]) * gelu

    xpb = xp_ref[bi]
    grp = d // len(POOL_WINDOWS)
    y_parts = []
    for gi in range(len(POOL_WINDOWS)):
        cols = slice(gi * grp, (gi + 1) * grp)
        rows = []
        for c0 in range(0, tt, POOL_CHUNK):
            xg = xpb[c0:c0 + POOL_CHUNK, cols]
            mean = _dot(pm_ref[gi], xg) / pc_ref[gi]
            rows.append((mean - xg.astype(F32)).astype(BF16))
        dg = rows[0] if len(rows) == 1 else jnp.concatenate(rows, axis=0)
        y_parts.append(_dot(dg, pw_ref[gi]) * ps_ref[:, cols])
    y_pool = jnp.concatenate(y_parts, axis=1)

    merged = (_sigmoid(mr_ref[bi].astype(F32)) * _dot(y_rnn.astype(BF16), w3_s[0])
              + _sigmoid(mp_ref[bi].astype(F32)) * _dot(y_pool.astype(BF16), w3_s[1]))
    m_lat = _dot(merged.astype(BF16), w3_s[2])
    x_mid = x_ref[bi] + g1_ref[bi] * _rmsnorm(m_lat, npost_ref[...])
    xmid_ref[bi] = x_mid
    v = _modulate(_rmsnorm(x_mid, npre_ref[...]), sh2_ref[bi], sc2_ref[bi])
    v_ref[bi] = v.reshape(tt, SUBLANES, LANES)

    logits = _dot3(v, rw_ref[...]) + rb_ref[...]
    lane = lax.broadcasted_iota(jnp.int32, (tt, n_exp), 1).astype(F32)
    work = logits
    vals, idxs, sels = [], [], []
    for _ in range(TOP_K):
        m = jnp.max(work, axis=-1, keepdims=True)
        idx = jnp.min(jnp.where(work == m, lane, float(n_exp)), axis=-1, keepdims=True)
        sel = lane == idx
        vals.append(m)
        idxs.append(idx)
        sels.append(sel)
        work = jnp.where(sel, -jnp.inf, work)
    exps = [jnp.exp(vk - vals[0]) for vk in vals]
    den = exps[0] + exps[1] + exps[2] + exps[3]
    anyf = jnp.zeros((tt, n_exp), F32)
    for sel in sels:
        anyf = anyf + sel.astype(F32)
    r_i = lax.broadcasted_iota(jnp.int32, (tt, tt), 0)
    c_i = lax.broadcasted_iota(jnp.int32, (tt, tt), 1)
    lower = (c_i < r_i).astype(BF16)
    before = _dot(lower, anyf.astype(BF16)) + cnt_s[...]
    cnt_new = cnt_s[...] + jnp.sum(anyf, axis=0, keepdims=True)
    cnt_s[...] = cnt_new
    cnt_ref[...] = cnt_new

    lane_o = lax.broadcasted_iota(jnp.int32, (tt, LANES), 1)
    code_o = jnp.zeros((tt, LANES), jnp.int32)
    wts_o = jnp.zeros((tt, LANES), F32)
    for k in range(TOP_K):
        rk = jnp.sum(jnp.where(sels[k], before, 0.0), axis=-1, keepdims=True)
        code = idxs[k].astype(jnp.int32) * RANK_RADIX + rk.astype(jnp.int32)
        code_o = jnp.where(lane_o == k, code, code_o)
        wts_o = jnp.where(lane_o == k, exps[k] / den, wts_o)
    code_ref[bi] = code_o
    wts_ref[bi] = wts_o


def _mix_bwd_call(hf, xc, gg, xp, mr, mp, x, wgb, ba, bx, lam, h0b, pm, pc, pw, ps, wr, wp, wo,
                  npost, npre, g1, sh2, sc2, rw, rb):
    bsz, t, d = x.shape
    nt = t // TT
    n_exp = rw.shape[1]
    rev = lambda i: (0, nt - 1 - i, 0)
    tile = pl.BlockSpec((bsz, TT, d), rev)
    small = pl.BlockSpec((bsz, TT, LANES), rev)
    per_b = pl.BlockSpec((bsz, 1, d), lambda i: (0, 0, 0))
    full = lambda shp: pl.BlockSpec(shp, lambda i: (0,) * len(shp))
    return pl.pallas_call(
        _mix_bwd_body,
        out_shape=(jax.ShapeDtypeStruct((bsz, t, d), F32),
                   jax.ShapeDtypeStruct((bsz, t, SUBLANES, LANES), F32),
                   jax.ShapeDtypeStruct((bsz, t, LANES), jnp.int32),
                   jax.ShapeDtypeStruct((bsz, t, LANES), F32),
                   jax.ShapeDtypeStruct((1, n_exp), F32)),
        grid=(nt,),
        in_specs=[tile] * 7 + [full(wgb.shape), full(ba.shape), full(bx.shape), full(lam.shape), per_b,
                               full(pm.shape), full(pc.shape), full(pw.shape), full((1, d)),
                               pl.BlockSpec(memory_space=pl.ANY), pl.BlockSpec(memory_space=pl.ANY),
                               pl.BlockSpec(memory_space=pl.ANY),
                               full((1, d)), full((1, d)), per_b, per_b, per_b,
                               full(rw.shape), full((1, n_exp))],
        out_specs=(tile, pl.BlockSpec((bsz, TT, SUBLANES, LANES), lambda i: (0, nt - 1 - i, 0, 0)),
                   small, small, full((1, n_exp))),
        scratch_shapes=[pltpu.VMEM((bsz, TT, d), F32), pltpu.VMEM((bsz, 1, d), F32), pltpu.VMEM((1, n_exp), F32),
                        pltpu.VMEM((3, d, d), BF16), pltpu.VMEM((d, d), F32), pltpu.SemaphoreType.DMA],
        compiler_params=pltpu.CompilerParams(
            dimension_semantics=("arbitrary",), vmem_limit_bytes=VMEM_LIMIT),
        name="mix_bwd",
    )(hf, xc, gg, xp, mr, mp, x, wgb, ba, bx, lam, h0b, pm, pc, pw, ps, wr, wp, wo,
      npost, npre, g1, sh2, sc2, rw, rb)


def _dispatch_body(n_pad_blocks, *refs):
    pos_refs = refs[:TOP_K]
    pad_ref, zero_ref, v_hbm, xs_hbm, vbuf, lsem, ssem = refs[TOP_K:]
    i = pl.program_id(0)
    steps = pl.num_programs(0)
    n_pad = pad_ref.shape[0]
    slot = lax.rem(i, DISPATCH_BUFS)

    def load(step):
        s = lax.rem(step, DISPATCH_BUFS)
        return pltpu.make_async_copy(v_hbm.at[pl.ds(step * TD, TD)], vbuf.at[s], lsem.at[s])

    def row_copy(s, src_row, dst_row):
        return pltpu.make_async_copy(vbuf.at[s, src_row], xs_hbm.at[dst_row], ssem.at[s])

    def pad_copy(s, dst_row):
        return pltpu.make_async_copy(zero_ref.at[0], xs_hbm.at[dst_row], ssem.at[s])

    def wait_scatter(step):
        s = lax.rem(step, DISPATCH_BUFS)

        def body(t, carry):
            for _ in range(TOP_K):
                row_copy(s, 0, 0).wait()
            return carry

        lax.fori_loop(0, TD, body, 0, unroll=ISSUE_UNROLL)

        @pl.when(step < n_pad_blocks)
        def _():
            def body_pad(q, carry):
                pad_copy(s, 0).wait()
                return carry

            lax.fori_loop(0, n_pad, body_pad, 0, unroll=ISSUE_UNROLL)

    @pl.when(i == 0)
    def _():
        load(0).start()

    @pl.when(i >= 2)
    def _():
        wait_scatter(i - 2)

    @pl.when(i + 1 < steps)
    def _():
        load(i + 1).start()

    load(i).wait()

    def issue(t, carry):
        for k in range(TOP_K):
            row_copy(slot, t, pos_refs[k][t]).start(priority=k % DMA_PRIORITIES)
        return carry

    lax.fori_loop(0, TD, issue, 0, unroll=ISSUE_UNROLL)

    @pl.when(i < n_pad_blocks)
    def _():
        def issue_pad(q, carry):
            pad_copy(slot, pad_ref[q]).start()
            return carry

        lax.fori_loop(0, n_pad, issue_pad, 0, unroll=ISSUE_UNROLL)

    @pl.when(i == steps - 1)
    def _():
        @pl.when(i >= 1)
        def _():
            wait_scatter(i - 1)

        wait_scatter(i)


def _pos_specs(steps, step_of):
    return [pl.BlockSpec((TD,), lambda i, k=k: (k * steps + step_of(i),), memory_space=pltpu.SMEM)
            for k in range(TOP_K)]


def _dispatch_call(pos_kmajor, pad_slots, v3, n_slots):
    n = v3.shape[0]
    slab = v3.shape[1:]
    steps = n // TD
    n_pad_blocks = pad_slots.shape[0] // PAD_BLOCK
    assert n_pad_blocks <= steps
    return pl.pallas_call(
        functools.partial(_dispatch_body, n_pad_blocks),
        out_shape=jax.ShapeDtypeStruct((n_slots,) + slab, F32),
        grid=(steps,),
        in_specs=_pos_specs(steps, lambda i: i) + [
            pl.BlockSpec((PAD_BLOCK,), lambda i: (jnp.minimum(i, n_pad_blocks - 1),), memory_space=pltpu.SMEM),
            pl.BlockSpec((1,) + slab, lambda i: (0, 0, 0)),
            pl.BlockSpec(memory_space=pl.ANY)],
        out_specs=pl.BlockSpec(memory_space=pl.ANY),
        scratch_shapes=[pltpu.VMEM((DISPATCH_BUFS, TD) + slab, F32),
                        pltpu.SemaphoreType.DMA((DISPATCH_BUFS,)), pltpu.SemaphoreType.DMA((DISPATCH_BUFS,))],
        compiler_params=pltpu.CompilerParams(dimension_semantics=("arbitrary",)),
        name="dispatch",
    )(*([pos_kmajor] * TOP_K), pad_slots, jnp.zeros((1,) + slab, F32), v3)


def _expert_body(te_ref, blk_ref, nvalid_ref, slot_ref, nxt_ref,
                 xs_ref, w1_hbm, b1_ref, w2_hbm, b2_ref, y_ref, w1f_s, w2f_s, w1b_s, w2b_s, sem):
    j = pl.program_id(0)
    n_exp = w1_hbm.shape[0]
    ff = w2_hbm.shape[1]
    e = te_ref[j]
    prev = te_ref[jnp.maximum(j - 1, 0)]

    def weight_copies(expert, slot):
        return (pltpu.make_async_copy(w1_hbm.at[expert], w1f_s.at[slot], sem.at[slot, 0]),
                pltpu.make_async_copy(w2_hbm.at[expert], w2f_s.at[slot], sem.at[slot, 1]))

    @pl.when(j == 0)
    def _():
        for c in weight_copies(e, slot_ref[e]):
            c.start()

    @pl.when((j == 0) | (e != prev))
    def _():
        slot = slot_ref[e]
        for c in weight_copies(e, slot):
            c.wait()
        w1b_s[...] = w1f_s[slot].astype(BF16)
        w2b_s[...] = w2f_s[slot].astype(BF16)

        @pl.when(nxt_ref[e] < n_exp)
        def _():
            for c in weight_copies(nxt_ref[e], 1 - slot):
                c.start()

    @pl.when(j < nvalid_ref[0])
    def _():
        hm = xs_ref.shape[0] // 2
        halves = (slice(0, hm), slice(hm, 2 * hm))
        zs = [_dot(xs_ref[sl].reshape(hm, w1_hbm.shape[1]).astype(BF16), w1b_s[...]) + b1_ref[0]
              for sl in halves]
        for sl, z in zip(halves, zs):
            glu = jnp.minimum(z[:, :ff], SWIGLU_LIMIT)
            lin = jnp.clip(z[:, ff:], -SWIGLU_LIMIT, SWIGLU_LIMIT)
            act = glu * _sigmoid(SWIGLU_ALPHA * glu) * (lin + 1.0)
            y = _dot(act.astype(BF16), w2b_s[...]) + b2_ref[0]
            y_ref[sl] = y.reshape((hm,) + y_ref.shape[1:])

    @pl.when(j >= nvalid_ref[0])
    def _():
        y_ref[...] = jnp.zeros_like(y_ref)


def _expert_call(te, blk, nvalid, slot, nxt, xs, w1, b1, w2, b2):
    n_slots = xs.shape[0]
    slab = xs.shape[1:]
    n_exp, d, ff2 = w1.shape
    ff = w2.shape[1]
    grid_spec = pltpu.PrefetchScalarGridSpec(
        num_scalar_prefetch=5,
        grid=(n_slots // TM,),
        in_specs=[pl.BlockSpec((TM,) + slab, lambda j, te, blk, *_: (blk[j], 0, 0)),
                  pl.BlockSpec(memory_space=pl.ANY),
                  pl.BlockSpec((1, 1, ff2), lambda j, te, *_: (te[j], 0, 0)),
                  pl.BlockSpec(memory_space=pl.ANY),
                  pl.BlockSpec((1, 1, d), lambda j, te, *_: (te[j], 0, 0))],
        out_specs=pl.BlockSpec((TM,) + slab, lambda j, *_: (j, 0, 0)),
        scratch_shapes=[pltpu.VMEM((2, d, ff2), F32), pltpu.VMEM((2, ff, d), F32),
                        pltpu.VMEM((d, ff2), BF16), pltpu.VMEM((ff, d), BF16),
                        pltpu.SemaphoreType.DMA((2, 2))],
    )
    return pl.pallas_call(
        _expert_body,
        out_shape=jax.ShapeDtypeStruct((n_slots,) + slab, F32),
        grid_spec=grid_spec,
        compiler_params=pltpu.CompilerParams(
            dimension_semantics=("arbitrary",), vmem_limit_bytes=VMEM_LIMIT),
        name="experts",
    )(te, blk, nvalid, slot, nxt, xs, w1, b1, w2, b2)


def _combine_body(*refs):
    pos_refs, posn_refs = refs[:TOP_K], refs[TOP_K:2 * TOP_K]
    wts_ref, xmid_ref, g2_ref, npost_ref, y_hbm, o_ref, buf, sem = refs[2 * TOP_K:]
    i = pl.program_id(0)
    cur = lax.rem(i, 2)

    def row_copy(p_refs, slot, t, k):
        return pltpu.make_async_copy(y_hbm.at[p_refs[k][t]], buf.at[slot, k, t], sem.at[slot])

    def issue(p_refs, slot):
        def body(t, carry):
            for k in range(TOP_K):
                row_copy(p_refs, slot, t, k).start(priority=k % DMA_PRIORITIES)
            return carry

        lax.fori_loop(0, TD, body, 0, unroll=ISSUE_UNROLL)

    @pl.when(i == 0)
    def _():
        issue(pos_refs, 0)

    @pl.when(i + 1 < pl.num_programs(0))
    def _():
        issue(posn_refs, 1 - cur)

    def drain(t, carry):
        for k in range(TOP_K):
            row_copy(pos_refs, cur, t, k).wait()
        return carry

    lax.fori_loop(0, TD, drain, 0, unroll=ISSUE_UNROLL)

    w = wts_ref[...]
    f = w[:, 0:1] * buf[cur, 0].reshape(xmid_ref.shape)
    for k in range(1, TOP_K):
        f = f + w[:, k:k + 1] * buf[cur, k].reshape(xmid_ref.shape)
    o_ref[...] = xmid_ref[...] + g2_ref[0] * _rmsnorm(f, npost_ref[...])


def _combine_call(pos_kmajor, wts, xmid2, g2, npost, ys, t_per_batch):
    n, d = xmid2.shape
    steps = n // TD
    return pl.pallas_call(
        _combine_body,
        out_shape=jax.ShapeDtypeStruct((n, d), F32),
        grid=(steps,),
        in_specs=_pos_specs(steps, lambda i: i) + _pos_specs(steps, lambda i: jnp.minimum(i + 1, steps - 1)) + [
                  pl.BlockSpec((TD, LANES), lambda i: (i, 0)),
                  pl.BlockSpec((TD, d), lambda i: (i, 0)),
                  pl.BlockSpec((1, 1, d), lambda i: ((i * TD) // t_per_batch, 0, 0)),
                  pl.BlockSpec((1, d), lambda i: (0, 0)),
                  pl.BlockSpec(memory_space=pl.ANY)],
        out_specs=pl.BlockSpec((TD, d), lambda i: (i, 0)),
        scratch_shapes=[pltpu.VMEM((2, TOP_K, TD) + ys.shape[1:], F32), pltpu.SemaphoreType.DMA((2,))],
        compiler_params=pltpu.CompilerParams(dimension_semantics=("arbitrary",)),
        name="combine",
    )(*([pos_kmajor] * (2 * TOP_K)), wts, xmid2, g2, npost, ys)


def _count_le(ends, q):
    return jnp.sum((ends[None, :] <= q[:, None]).astype(jnp.int32), axis=1)


def _gate_weights(wa, wx):
    return jnp.concatenate([wa, wx], axis=-1).astype(BF16)


def kernel(x, c, ctx, c_ctx, w_mod, b_mod, norm_pre_mix, norm_post_mix, norm_pre_ffn, norm_post_ffn, w_in, conv_w, conv_b, lru_wa, lru_ba, lru_wx, lru_bx, lru_lambda, pool_w, pool_scale, w_rnn_proj, w_pool_proj, w_out, router_w, router_b, exp_w1, exp_b1, exp_w2, exp_b2):
    bsz, t, d = x.shape
    assert w_mod.shape[0] == 1, "single-layer block"
    assert t % TT == 0 and TT % POOL_CHUNK == 0 and POOL_CHUNK % GRID_W == 0 and t % TD == 0
    assert d == SUBLANES * LANES, "a token row must be exactly one (8, 128) slab"
    n = bsz * t
    n_exp = router_w.shape[-1]
    assert (n * TOP_K) % TM == 0 and (n_exp * TM) % PAD_BLOCK == 0 and n <= RANK_RADIX
    row = lambda a: a.reshape(1, -1)

    pad = jnp.zeros((SUBLANES - (bsz + 1) % SUBLANES, d), F32)
    cc = jnp.concatenate([c, c_ctx[None, :], pad], axis=0)
    mod = _mod_call(cc, w_mod[0], row(b_mod[0]))
    lat = lambda j: mod[:bsz, j * d:(j + 1) * d].reshape(bsz, 1, d)
    sh1, sc1, g1, sh2, sc2, g2 = (lat(j) for j in range(6))
    csh1, csc1 = mod[bsz:bsz + 1, 0:d], mod[bsz:bsz + 1, d:2 * d]

    wgf = _gate_weights(lru_wa[0, 0], lru_wx[0, 0])
    wgb = _gate_weights(lru_wa[0, 1], lru_wx[0, 1])
    ba, bx, lam = lru_ba[0], lru_bx[0], lru_lambda[0]
    cw, cb = conv_w[0], row(conv_b[0])
    npre_mix = row(norm_pre_mix[0])

    h0f, h0b = _ctx_call(ctx, npre_mix, csh1, csc1, w_in[0], cw, cb, wgf, wgb, ba, bx, lam)

    hf, xc, gg, xp, mr, mp = _mix_fwd_call(x, npre_mix, sh1, sc1, w_in[0], cw, cb, wgf, ba, bx, lam, h0f)

    pm, pc = _pool_consts()
    xmid, v, code_o, wts_o, cnt = _mix_bwd_call(
        hf, xc, gg, xp, mr, mp, x, wgb, ba, bx, lam, h0b,
        pm, pc, pool_w[0].astype(BF16), row(pool_scale[0]),
        w_rnn_proj[0], w_pool_proj[0], w_out[0],
        row(norm_post_mix[0]), row(norm_pre_ffn[0]), g1, sh2, sc2, router_w[0], row(router_b[0]))

    code = code_o.reshape(n, LANES)[:, :TOP_K].T
    idx, rank = code // RANK_RADIX, code % RANK_RADIX
    counts = cnt[0].astype(jnp.int32)
    tiles_e = (counts + TM - 1) // TM
    tile_end = jnp.cumsum(tiles_e)
    offs = (tile_end - tiles_e) * TM
    e_ids = jnp.arange(n_exp, dtype=jnp.int32)
    off_of = jnp.sum(jnp.where(idx[None] == e_ids[:, None, None], offs[:, None, None], 0), axis=0)
    pos_kmajor = (off_of + rank).reshape(TOP_K * n)
    n_tiles = (n * TOP_K) // TM + n_exp
    n_valid = tile_end[-1]
    jj = jnp.arange(n_tiles, dtype=jnp.int32)
    blk = jnp.minimum(jj, n_valid - 1)
    te = jnp.minimum(_count_le(tile_end, blk), n_exp - 1)

    pad_e = tiles_e * TM - counts
    pad_end = jnp.cumsum(pad_e)
    qq = jnp.arange(n_exp * TM, dtype=jnp.int32)
    pad_start = pad_end - pad_e
    inside = (qq[None, :] >= pad_start[:, None]) & (qq[None, :] < pad_end[:, None])
    slot_in = jnp.sum(jnp.where(inside, (offs + counts - pad_start)[:, None] + qq[None, :], 0), axis=0)
    slot_tail = n_valid * TM + qq - pad_end[-1]
    pad_slots = jnp.where(qq < pad_end[-1], slot_in, slot_tail).astype(jnp.int32)

    xs = _dispatch_call(pos_kmajor, pad_slots, v.reshape((n,) + v.shape[2:]), n_tiles * TM)
    has = tiles_e > 0
    w_slot = ((jnp.cumsum(has.astype(jnp.int32)) - has.astype(jnp.int32)) % 2).astype(jnp.int32)
    later = (e_ids[None, :] > e_ids[:, None]) & has[None, :]
    w_next = jnp.min(jnp.where(later, e_ids[None, :], n_exp), axis=1).astype(jnp.int32)
    ys = _expert_call(te, blk, n_valid.reshape(1), w_slot, w_next, xs, exp_w1[0],
                      exp_b1[0].reshape(n_exp, 1, -1), exp_w2[0], exp_b2[0].reshape(n_exp, 1, -1))
    out = _combine_call(pos_kmajor, wts_o.reshape(n, LANES), xmid.reshape(n, d), g2, row(norm_post_ffn[0]), ys, t)
    return out.reshape(bsz, t, d)
```

```python
import functools

import numpy as np
import jax
import jax.numpy as jnp
from jax import lax
from jax.experimental import pallas as pl
from jax.experimental.pallas import tpu as pltpu

F32 = jnp.float32
BF16 = jnp.bfloat16

RNN_HEADS = 4
CONV_W = 4
LRU_C = 8.0
POOL_WINDOWS = (2, 4, 8, 16)
GRID_W = 64
TOP_K = 4
SWIGLU_LIMIT = 7.0
SWIGLU_ALPHA = 1.702
EPS = 1e-6
SQRT_FLOOR = 1e-30

SUBLANES = 8
LANES = 128
VMEM_LIMIT = 56 * 1024 * 1024

TT = 256
POOL_CHUNK = 256
TM = 512
TD = 128
ZERO_ROWS = 256
DISPATCH_BUFS = 3
ISSUE_UNROLL = 8
DMA_PRIORITIES = 2
RANK_RADIX = 1 << 16


def _sigmoid(x):
    return 1.0 / (1.0 + jnp.exp(-x))


def _softplus(z):
    return jnp.maximum(z, 0.0) + jnp.log1p(jnp.exp(-jnp.abs(z)))


def _sqrt_nonneg(x):
    return x * lax.rsqrt(jnp.maximum(x, SQRT_FLOOR))


def _rmsnorm(x, g):
    ms = jnp.mean(x * x, axis=-1, keepdims=True)
    return (x * lax.rsqrt(ms + EPS)) * g


def _modulate(u, shift, scale):
    return u * (1.0 + scale) + shift


def _dot(a, b):
    return jnp.dot(a, b, preferred_element_type=F32)


def _split(x):
    hi = x.astype(BF16)
    return hi, (x - hi.astype(F32)).astype(BF16)


def _dot3(a, b):
    ah, al = _split(a)
    bh, bl = _split(b)
    return _dot(ah, bh) + (_dot(al, bh) + _dot(ah, bl))


def _load_cast(w_hbm, dst_s, tmp_s, sem):
    c = tmp_s.shape[1]
    for j in range(w_hbm.shape[1] // c):
        cp = pltpu.make_async_copy(w_hbm.at[:, pl.ds(j * c, c)], tmp_s, sem)
        cp.start()
        cp.wait()
        dst_s[:, j * c:(j + 1) * c] = tmp_s[...].astype(BF16)


def _conv(prev8, xr, next8, cw, cb):
    t = xr.shape[0]
    ext = jnp.concatenate([prev8, xr, next8], axis=0)
    acc = cb + cw[0:1, :] * ext[SUBLANES - 2:SUBLANES - 2 + t, :]
    for k in range(1, CONV_W):
        off = SUBLANES - 2 + k
        acc = acc + cw[k:k + 1, :] * ext[off:off + t, :]
    return acc


def _lru_coeffs(xc, wg_ref, ba, bx, sp):
    hd = xc.shape[1] // RNN_HEADS
    xcb = xc.astype(BF16)
    a_parts, b_parts = [], []
    for h in range(RNN_HEADS):
        cols = slice(h * hd, (h + 1) * hd)
        z = _dot(xcb[:, cols], wg_ref[h])
        r = _sigmoid(z[:, :hd] + ba[:, cols])
        i = _sigmoid(z[:, hd:] + bx[:, cols])
        a = jnp.exp((-LRU_C) * r * sp[:, cols])
        b = _sqrt_nonneg(1.0 - a * a) * (i * xc[:, cols])
        a_parts.append(a)
        b_parts.append(b)
    return jnp.concatenate(a_parts, axis=1), jnp.concatenate(b_parts, axis=1)


def _scan_tile(a, b, h_in, reverse, store):
    t, c = a.shape
    g = t // SUBLANES
    a3 = a.reshape(g, SUBLANES, c)
    b3 = b.reshape(g, SUBLANES, c)
    row = lax.broadcasted_iota(jnp.int32, (g, SUBLANES, c), 1)
    for s in (1, 2, 4):
        if reverse:
            shift, m = SUBLANES - s, row < SUBLANES - s
        else:
            shift, m = s, row >= s
        ra = pltpu.roll(a3, shift, axis=1)
        rb = pltpu.roll(b3, shift, axis=1)
        b3 = a3 * jnp.where(m, rb, 0.0) + b3
        a3 = a3 * jnp.where(m, ra, 1.0)
    h = h_in
    order = range(g - 1, -1, -1) if reverse else range(g)
    for gi in order:
        hg = a3[gi] * h + b3[gi]
        store(gi, hg)
        h = hg[0:1, :] if reverse else hg[SUBLANES - 1:SUBLANES, :]
    return h


def _mod_body(c_ref, w_ref, b_ref, o_ref):
    c = c_ref[...]
    s = c * _sigmoid(c)
    o_ref[...] = _dot3(s, w_ref[...]) + b_ref[...]


def _mod_call(cc, w_mod, b_mod):
    d = cc.shape[1]
    n = w_mod.shape[1] // d
    return pl.pallas_call(
        _mod_body,
        out_shape=jax.ShapeDtypeStruct((cc.shape[0], n * d), F32),
        grid=(n,),
        in_specs=[pl.BlockSpec(cc.shape, lambda j: (0, 0)),
                  pl.BlockSpec((d, d), lambda j: (0, j)),
                  pl.BlockSpec((1, d), lambda j: (0, j))],
        out_specs=pl.BlockSpec((cc.shape[0], d), lambda j: (0, j)),
        name="mod",
    )(cc, w_mod, b_mod)


def _ctx_body(ctx_ref, g_ref, sh_ref, sc_ref, w_ref, cw_ref, cb_ref, wgf_ref, wgb_ref,
              ba_ref, bx_ref, lam_ref, hf_ref, hb_ref):
    d = ctx_ref.shape[2]
    u = _modulate(_rmsnorm(ctx_ref[0], g_ref[...]), sh_ref[...], sc_ref[...])
    xr = _dot(u.astype(BF16), w_ref[...].astype(BF16))
    z8 = jnp.zeros((SUBLANES, d), F32)
    xc = _conv(z8, xr, z8, cw_ref[...], cb_ref[...])
    h0 = jnp.zeros((1, d), F32)
    for di, (wg_ref, out_ref) in enumerate(((wgf_ref, hf_ref), (wgb_ref, hb_ref))):
        sp = _softplus(-lam_ref[di:di + 1, :])
        a, b = _lru_coeffs(xc, wg_ref, ba_ref[di:di + 1, :], bx_ref[di:di + 1, :], sp)
        out_ref[0] = _scan_tile(a, b, h0, di == 1, lambda gi, hg: None)


def _ctx_call(ctx, g, sh, sc, w_in, cw, cb, wgf, wgb, ba, bx, lam):
    bsz, tc, d = ctx.shape
    full = lambda shp: pl.BlockSpec(shp, lambda b: (0,) * len(shp))
    return pl.pallas_call(
        _ctx_body,
        out_shape=(jax.ShapeDtypeStruct((bsz, 1, d), F32),) * 2,
        grid=(bsz,),
        in_specs=[pl.BlockSpec((1, tc, d), lambda b: (b, 0, 0)),
                  full((1, d)), full((1, d)), full((1, d)),
                  pl.BlockSpec((d, d), lambda b: (0, 0)),
                  full(cw.shape), full((1, d)), full(wgf.shape), full(wgb.shape),
                  full(ba.shape), full(bx.shape), full(lam.shape)],
        out_specs=(pl.BlockSpec((1, 1, d), lambda b: (b, 0, 0)),) * 2,
        compiler_params=pltpu.CompilerParams(vmem_limit_bytes=VMEM_LIMIT),
        name="ctx",
    )(ctx, g, sh, sc, w_in, cw, cb, wgf, wgb, ba, bx, lam)


def _mix_fwd_body(x_ref, xn_ref, g_ref, sh_ref, sc_ref, win_ref, cw_ref, cb_ref, wgf_ref,
                  ba_ref, bx_ref, lam_ref, h0_ref,
                  hf_ref, xc_ref, gg_ref, xp_ref, mr_ref, mp_ref, tail_s, h_s, win_s, wtmp_s, wsem):
    i = pl.program_id(0)
    last = pl.num_programs(0) - 1
    bsz, tt, d = x_ref.shape

    @pl.when(i == 0)
    def _():
        _load_cast(win_ref, win_s, wtmp_s, wsem)
        tail_s[...] = jnp.zeros_like(tail_s)
        h_s[...] = h0_ref[...]

    g = g_ref[...]
    w_rnn = win_s[:, 0:d]
    sp = _softplus(-lam_ref[0:1, :])
    for bi in range(bsz):
        sh, sc = sh_ref[bi], sc_ref[bi]
        ub = _modulate(_rmsnorm(x_ref[bi], g), sh, sc).astype(BF16)
        unb = _modulate(_rmsnorm(xn_ref[bi], g), sh, sc).astype(BF16)
        xr = _dot(ub, w_rnn)
        xrn = jnp.where(i == last, 0.0, _dot(unb, w_rnn))
        xc = _conv(tail_s[bi], xr, xrn, cw_ref[...], cb_ref[...])
        tail_s[bi] = xr[tt - SUBLANES:tt, :]
        xc_ref[bi] = xc.astype(BF16)
        for j, ref in enumerate((gg_ref, xp_ref, mr_ref, mp_ref)):
            ref[bi] = _dot(ub, win_s[:, (j + 1) * d:(j + 2) * d]).astype(BF16)

        a, b = _lru_coeffs(xc, wgf_ref, ba_ref[0:1, :], bx_ref[0:1, :], sp)

        def store(gi, hg, bi=bi):
            hf_ref[bi, gi * SUBLANES:(gi + 1) * SUBLANES, :] = hg

        h_s[bi] = _scan_tile(a, b, h_s[bi], False, store)


def _mix_fwd_call(x, g, sh, sc, w_in, cw, cb, wgf, ba, bx, lam, h0f):
    bsz, t, d = x.shape
    nt = t // TT
    nblk8 = t // SUBLANES
    per_b = pl.BlockSpec((bsz, 1, d), lambda i: (0, 0, 0))
    full = lambda shp: pl.BlockSpec(shp, lambda i: (0,) * len(shp))
    tile = pl.BlockSpec((bsz, TT, d), lambda i: (0, i, 0))
    return pl.pallas_call(
        _mix_fwd_body,
        out_shape=(jax.ShapeDtypeStruct((bsz, t, d), F32),) + (jax.ShapeDtypeStruct((bsz, t, d), BF16),) * 5,
        grid=(nt,),
        in_specs=[tile,
                  pl.BlockSpec((bsz, SUBLANES, d),
                               lambda i: (0, jnp.minimum((i + 1) * (TT // SUBLANES), nblk8 - 1), 0)),
                  full((1, d)), per_b, per_b,
                  pl.BlockSpec(memory_space=pl.ANY), full(cw.shape), full((1, d)), full(wgf.shape),
                  full(ba.shape), full(bx.shape), full(lam.shape), per_b],
        out_specs=(tile,) * 6,
        scratch_shapes=[pltpu.VMEM((bsz, SUBLANES, d), F32), pltpu.VMEM((bsz, 1, d), F32),
                        pltpu.VMEM(w_in.shape, BF16), pltpu.VMEM((d, d), F32), pltpu.SemaphoreType.DMA],
        compiler_params=pltpu.CompilerParams(
            dimension_semantics=("arbitrary",), vmem_limit_bytes=VMEM_LIMIT),
        name="mix_fwd",
    )(x, x, g, sh, sc, w_in, cw, cb, wgf, ba, bx, lam, h0f)


def _pool_consts():
    p = np.arange(POOL_CHUNK)
    pos, line = p % GRID_W, p // GRID_W
    mats, cnts = [], []
    for w in POOL_WINDOWS:
        lo = np.clip(pos - w // 2, 0, GRID_W)
        hi = np.clip(pos + w - w // 2, 0, GRID_W)
        m = (line[:, None] == line[None, :]) & (pos[None, :] >= lo[:, None]) & (pos[None, :] < hi[:, None])
        mats.append(m.astype(np.float32))
        cnts.append((hi - lo).astype(np.float32)[:, None])
    return jnp.asarray(np.stack(mats), BF16), jnp.asarray(np.stack(cnts), F32)


def _mix_bwd_body(hf_ref, xc_ref, gg_ref, xp_ref, mr_ref, mp_ref, x_ref,
                  wgb_ref, ba_ref, bx_ref, lam_ref, h0_ref,
                  pm_ref, pc_ref, pw_ref, ps_ref, wr_ref, wp_ref, wo_ref,
                  npost_ref, npre_ref, g1_ref, sh2_ref, sc2_ref, rw_ref, rb_ref,
                  xmid_ref, v_ref, code_ref, wts_ref, cnt_ref,
                  hb_s, h_s, cnt_s, w3_s, wtmp_s, wsem):
    i = pl.program_id(0)
    bsz = x_ref.shape[0]

    @pl.when(i == 0)
    def _():
        for k, w_hbm in enumerate((wr_ref, wp_ref, wo_ref)):
            _load_cast(w_hbm, w3_s.at[k], wtmp_s, wsem)
        h_s[...] = h0_ref[...]
        cnt_s[...] = jnp.zeros_like(cnt_s)

    for bi in range(bsz):
        _mix_bwd_row(bi, hf_ref, xc_ref, gg_ref, xp_ref, mr_ref, mp_ref, x_ref,
                     wgb_ref, ba_ref, bx_ref, lam_ref, pm_ref, pc_ref, pw_ref, ps_ref,
                     npost_ref, npre_ref, g1_ref, sh2_ref, sc2_ref, rw_ref, rb_ref,
                     xmid_ref, v_ref, code_ref, wts_ref, cnt_ref, hb_s, h_s, cnt_s, w3_s)


def _mix_bwd_row(bi, hf_ref, xc_ref, gg_ref, xp_ref, mr_ref, mp_ref, x_ref,
                 wgb_ref, ba_ref, bx_ref, lam_ref, pm_ref, pc_ref, pw_ref, ps_ref,
                 npost_ref, npre_ref, g1_ref, sh2_ref, sc2_ref, rw_ref, rb_ref,
                 xmid_ref, v_ref, code_ref, wts_ref, cnt_ref, hb_s, h_s, cnt_s, w3_s):
    tt, d = x_ref.shape[1], x_ref.shape[2]
    n_exp = rw_ref.shape[1]

    xc = xc_ref[bi].astype(F32)
    sp = _softplus(-lam_ref[1:2, :])
    a, b = _lru_coeffs(xc, wgb_ref, ba_ref[1:2, :], bx_ref[1:2, :], sp)

    def store(gi, hg):
        hb_s[bi, gi * SUBLANES:(gi + 1) * SUBLANES, :] = hg

    h_s[bi] = _scan_tile(a, b, h_s[bi], True, store)

    gg = gg_ref[bi].astype(F32)
    gelu = gg * _sigmoid(gg * (1.5957691216057308 + 0.07135481627260025 * (gg * gg)))
    y_rnn = (hf_ref[bi] + hb_s[bi]) * gelu

    xpb = xp_ref[bi]
    grp = d // len(POOL_WINDOWS)
    y_parts = []
    for gi in range(len(POOL_WINDOWS)):
        cols = slice(gi * grp, (gi + 1) * grp)
        rows = []
        for c0 in range(0, tt, POOL_CHUNK):
            xg = xpb[c0:c0 + POOL_CHUNK, cols]
            mean = _dot(pm_ref[gi], xg) / pc_ref[gi]
            rows.append((mean - xg.astype(F32)).astype(BF16))
        dg = rows[0] if len(rows) == 1 else jnp.concatenate(rows, axis=0)
        y_parts.append(_dot(dg, pw_ref[gi]) * ps_ref[:, cols])
    y_pool = jnp.concatenate(y_parts, axis=1)

    merged = (_sigmoid(mr_ref[bi].astype(F32)) * _dot(y_rnn.astype(BF16), w3_s[0])
              + _sigmoid(mp_ref[bi].astype(F32)) * _dot(y_pool.astype(BF16), w3_s[1]))
    m_lat = _dot(merged.astype(BF16), w3_s[2])
    x_mid = x_ref[bi] + g1_ref[bi] * _rmsnorm(m_lat, npost_ref[...])
    xmid_ref[bi] = x_mid
    v = _modulate(_rmsnorm(x_mid, npre_ref[...]), sh2_ref[bi], sc2_ref[bi])
    v_ref[bi] = v.reshape(tt, SUBLANES, LANES)

    logits = _dot3(v, rw_ref[...]) + rb_ref[...]
    lane = lax.broadcasted_iota(jnp.int32, (tt, n_exp), 1).astype(F32)
    work = logits
    vals, idxs, sels = [], [], []
    for _ in range(TOP_K):
        m = jnp.max(work, axis=-1, keepdims=True)
        idx = jnp.min(jnp.where(work == m, lane, float(n_exp)), axis=-1, keepdims=True)
        sel = lane == idx
        vals.append(m)
        idxs.append(idx)
        sels.append(sel)
        work = jnp.where(sel, -jnp.inf, work)
    exps = [jnp.exp(vk - vals[0]) for vk in vals]
    den = exps[0] + exps[1] + exps[2] + exps[3]
    anyf = jnp.zeros((tt, n_exp), F32)
    for sel in sels:
        anyf = anyf + sel.astype(F32)
    r_i = lax.broadcasted_iota(jnp.int32, (tt, tt), 0)
    c_i = lax.broadcasted_iota(jnp.int32, (tt, tt), 1)
    lower = (c_i < r_i).astype(BF16)
    before = _dot(lower, anyf.astype(BF16)) + cnt_s[...]
    cnt_new = cnt_s[...] + jnp.sum(anyf, axis=0, keepdims=True)
    cnt_s[...] = cnt_new
    cnt_ref[...] = cnt_new

    lane_o = lax.broadcasted_iota(jnp.int32, (tt, LANES), 1)
    code_o = jnp.zeros((tt, LANES), jnp.int32)
    wts_o = jnp.zeros((tt, LANES), F32)
    for k in range(TOP_K):
        rk = jnp.sum(jnp.where(sels[k], before, 0.0), axis=-1, keepdims=True)
        code = idxs[k].astype(jnp.int32) * RANK_RADIX + rk.astype(jnp.int32)
        code_o = jnp.where(lane_o == k, code, code_o)
        wts_o = jnp.where(lane_o == k, exps[k] / den, wts_o)
    code_ref[bi] = code_o
    wts_ref[bi] = wts_o


def _mix_bwd_call(hf, xc, gg, xp, mr, mp, x, wgb, ba, bx, lam, h0b, pm, pc, pw, ps, wr, wp, wo,
                  npost, npre, g1, sh2, sc2, rw, rb):
    bsz, t, d = x.shape
    nt = t // TT
    n_exp = rw.shape[1]
    rev = lambda i: (0, nt - 1 - i, 0)
    tile = pl.BlockSpec((bsz, TT, d), rev)
    small = pl.BlockSpec((bsz, TT, LANES), rev)
    per_b = pl.BlockSpec((bsz, 1, d), lambda i: (0, 0, 0))
    full = lambda shp: pl.BlockSpec(shp, lambda i: (0,) * len(shp))
    return pl.pallas_call(
        _mix_bwd_body,
        out_shape=(jax.ShapeDtypeStruct((bsz, t, d), F32),
                   jax.ShapeDtypeStruct((bsz, t, SUBLANES, LANES), F32),
                   jax.ShapeDtypeStruct((bsz, t, LANES), jnp.int32),
                   jax.ShapeDtypeStruct((bsz, t, LANES), F32),
                   jax.ShapeDtypeStruct((1, n_exp), F32)),
        grid=(nt,),
        in_specs=[tile] * 7 + [full(wgb.shape), full(ba.shape), full(bx.shape), full(lam.shape), per_b,
                               full(pm.shape), full(pc.shape), full(pw.shape), full((1, d)),
                               pl.BlockSpec(memory_space=pl.ANY), pl.BlockSpec(memory_space=pl.ANY),
                               pl.BlockSpec(memory_space=pl.ANY),
                               full((1, d)), full((1, d)), per_b, per_b, per_b,
                               full(rw.shape), full((1, n_exp))],
        out_specs=(tile, pl.BlockSpec((bsz, TT, SUBLANES, LANES), lambda i: (0, nt - 1 - i, 0, 0)),
                   small, small, full((1, n_exp))),
        scratch_shapes=[pltpu.VMEM((bsz, TT, d), F32), pltpu.VMEM((bsz, 1, d), F32), pltpu.VMEM((1, n_exp), F32),
                        pltpu.VMEM((3, d, d), BF16), pltpu.VMEM((d, d), F32), pltpu.SemaphoreType.DMA],
        compiler_params=pltpu.CompilerParams(
            dimension_semantics=("arbitrary",), vmem_limit_bytes=VMEM_LIMIT),
        name="mix_bwd",
    )(hf, xc, gg, xp, mr, mp, x, wgb, ba, bx, lam, h0b, pm, pc, pw, ps, wr, wp, wo,
      npost, npre, g1, sh2, sc2, rw, rb)


def _dispatch_body(n_exp, *refs):
    pos_refs = refs[:TOP_K]
    pad_ref, zero_ref, v_hbm, xs_hbm, vbuf, lsem, ssem = refs[TOP_K:]
    i = pl.program_id(0)
    steps = pl.num_programs(0)
    zr = zero_ref.shape[0]
    slot = lax.rem(i, DISPATCH_BUFS)

    def load(step):
        s = lax.rem(step, DISPATCH_BUFS)
        return pltpu.make_async_copy(v_hbm.at[pl.ds(step * TD, TD)], vbuf.at[s], lsem.at[s])

    def row_copy(s, src_row, dst_row):
        return pltpu.make_async_copy(vbuf.at[s, src_row], xs_hbm.at[dst_row], ssem.at[s])

    def zero_copy(s, start, rows):
        return pltpu.make_async_copy(zero_ref.at[pl.ds(0, rows)], xs_hbm.at[pl.ds(start, rows)], ssem.at[s])

    def pad_copies(step, fn):
        s = lax.rem(step, DISPATCH_BUFS)
        e = jnp.minimum(step, n_exp - 1)
        run_start, run_len = pad_ref[e], pad_ref[n_exp + e]
        size = zr
        while size >= 1:
            @pl.when((step < n_exp) & ((run_len & size) != 0))
            def _(size=size):
                fn(zero_copy(s, run_start + (run_len & ~(2 * size - 1)), size))
            size //= 2

        @pl.when(step < pad_ref[2 * n_exp + 1])
        def _():
            for c in range(TM // zr):
                fn(zero_copy(s, pad_ref[2 * n_exp] + step * TM + c * zr, zr))

    def wait_scatter(step):
        s = lax.rem(step, DISPATCH_BUFS)

        def body(t, carry):
            for _ in range(TOP_K):
                row_copy(s, 0, 0).wait()
            return carry

        lax.fori_loop(0, TD, body, 0, unroll=ISSUE_UNROLL)
        pad_copies(step, lambda c: c.wait())

    @pl.when(i == 0)
    def _():
        load(0).start()

    @pl.when(i >= 2)
    def _():
        wait_scatter(i - 2)

    @pl.when(i + 1 < steps)
    def _():
        load(i + 1).start()

    load(i).wait()

    def issue(t, carry):
        for k in range(TOP_K):
            row_copy(slot, t, pos_refs[k][t]).start(priority=k % DMA_PRIORITIES)
        return carry

    lax.fori_loop(0, TD, issue, 0, unroll=ISSUE_UNROLL)
    pad_copies(i, lambda c: c.start())

    @pl.when(i == steps - 1)
    def _():
        @pl.when(i >= 1)
        def _():
            wait_scatter(i - 1)

        wait_scatter(i)


def _pos_specs(steps, step_of):
    return [pl.BlockSpec((TD,), lambda i, k=k: (k * steps + step_of(i),), memory_space=pltpu.SMEM)
            for k in range(TOP_K)]


def _dispatch_call(pos_kmajor, pad_info, n_exp, v3, n_slots):
    n = v3.shape[0]
    slab = v3.shape[1:]
    steps = n // TD
    assert n_exp <= steps and TM % ZERO_ROWS == 0 and TM <= 2 * ZERO_ROWS
    return pl.pallas_call(
        functools.partial(_dispatch_body, n_exp),
        out_shape=jax.ShapeDtypeStruct((n_slots,) + slab, F32),
        grid=(steps,),
        in_specs=_pos_specs(steps, lambda i: i) + [
            pl.BlockSpec(pad_info.shape, lambda i: (0,), memory_space=pltpu.SMEM),
            pl.BlockSpec((ZERO_ROWS,) + slab, lambda i: (0, 0, 0)),
            pl.BlockSpec(memory_space=pl.ANY)],
        out_specs=pl.BlockSpec(memory_space=pl.ANY),
        scratch_shapes=[pltpu.VMEM((DISPATCH_BUFS, TD) + slab, F32),
                        pltpu.SemaphoreType.DMA((DISPATCH_BUFS,)), pltpu.SemaphoreType.DMA((DISPATCH_BUFS,))],
        compiler_params=pltpu.CompilerParams(dimension_semantics=("arbitrary",)),
        name="dispatch",
    )(*([pos_kmajor] * TOP_K), pad_info, jnp.zeros((ZERO_ROWS,) + slab, F32), v3)


def _expert_body(te_ref, blk_ref, nvalid_ref, slot_ref, nxt_ref,
                 xs_ref, w1_hbm, b1_ref, w2_hbm, b2_ref, y_ref, w1f_s, w2f_s, w1b_s, w2b_s, sem):
    j = pl.program_id(0)
    n_exp = w1_hbm.shape[0]
    ff = w2_hbm.shape[1]
    e = te_ref[j]
    prev = te_ref[jnp.maximum(j - 1, 0)]

    def weight_copies(expert, slot):
        return (pltpu.make_async_copy(w1_hbm.at[expert], w1f_s.at[slot], sem.at[slot, 0]),
                pltpu.make_async_copy(w2_hbm.at[expert], w2f_s.at[slot], sem.at[slot, 1]))

    @pl.when(j == 0)
    def _():
        for c in weight_copies(e, slot_ref[e]):
            c.start()

    @pl.when((j == 0) | (e != prev))
    def _():
        slot = slot_ref[e]
        for c in weight_copies(e, slot):
            c.wait()
        w1b_s[...] = w1f_s[slot].astype(BF16)
        w2b_s[...] = w2f_s[slot].astype(BF16)

        @pl.when(nxt_ref[e] < n_exp)
        def _():
            for c in weight_copies(nxt_ref[e], 1 - slot):
                c.start()

    @pl.when(j < nvalid_ref[0])
    def _():
        hm = xs_ref.shape[0] // 2
        halves = (slice(0, hm), slice(hm, 2 * hm))
        zs = [_dot(xs_ref[sl].reshape(hm, w1_hbm.shape[1]).astype(BF16), w1b_s[...]) + b1_ref[0]
              for sl in halves]
        for sl, z in zip(halves, zs):
            glu = jnp.minimum(z[:, :ff], SWIGLU_LIMIT)
            lin = jnp.clip(z[:, ff:], -SWIGLU_LIMIT, SWIGLU_LIMIT)
            act = glu * _sigmoid(SWIGLU_ALPHA * glu) * (lin + 1.0)
            y = _dot(act.astype(BF16), w2b_s[...]) + b2_ref[0]
            y_ref[sl] = y.reshape((hm,) + y_ref.shape[1:])

    @pl.when(j >= nvalid_ref[0])
    def _():
        y_ref[...] = jnp.zeros_like(y_ref)


def _expert_call(te, blk, nvalid, slot, nxt, xs, w1, b1, w2, b2):
    n_slots = xs.shape[0]
    slab = xs.shape[1:]
    n_exp, d, ff2 = w1.shape
    ff = w2.shape[1]
    grid_spec = pltpu.PrefetchScalarGridSpec(
        num_scalar_prefetch=5,
        grid=(n_slots // TM,),
        in_specs=[pl.BlockSpec((TM,) + slab, lambda j, te, blk, *_: (blk[j], 0, 0)),
                  pl.BlockSpec(memory_space=pl.ANY),
                  pl.BlockSpec((1, 1, ff2), lambda j, te, *_: (te[j], 0, 0)),
                  pl.BlockSpec(memory_space=pl.ANY),
                  pl.BlockSpec((1, 1, d), lambda j, te, *_: (te[j], 0, 0))],
        out_specs=pl.BlockSpec((TM,) + slab, lambda j, *_: (j, 0, 0)),
        scratch_shapes=[pltpu.VMEM((2, d, ff2), F32), pltpu.VMEM((2, ff, d), F32),
                        pltpu.VMEM((d, ff2), BF16), pltpu.VMEM((ff, d), BF16),
                        pltpu.SemaphoreType.DMA((2, 2))],
    )
    return pl.pallas_call(
        _expert_body,
        out_shape=jax.ShapeDtypeStruct((n_slots,) + slab, F32),
        grid_spec=grid_spec,
        compiler_params=pltpu.CompilerParams(
            dimension_semantics=("arbitrary",), vmem_limit_bytes=VMEM_LIMIT),
        name="experts",
    )(te, blk, nvalid, slot, nxt, xs, w1, b1, w2, b2)


def _combine_body(*refs):
    pos_refs, posn_refs = refs[:TOP_K], refs[TOP_K:2 * TOP_K]
    wts_ref, xmid_ref, g2_ref, npost_ref, y_hbm, o_ref, buf, sem = refs[2 * TOP_K:]
    i = pl.program_id(0)
    cur = lax.rem(i, 2)

    def row_copy(p_refs, slot, t, k):
        return pltpu.make_async_copy(y_hbm.at[p_refs[k][t]], buf.at[slot, k, t], sem.at[slot])

    def issue(p_refs, slot):
        def body(t, carry):
            for k in range(TOP_K):
                row_copy(p_refs, slot, t, k).start(priority=k % DMA_PRIORITIES)
            return carry

        lax.fori_loop(0, TD, body, 0, unroll=ISSUE_UNROLL)

    @pl.when(i == 0)
    def _():
        issue(pos_refs, 0)

    @pl.when(i + 1 < pl.num_programs(0))
    def _():
        issue(posn_refs, 1 - cur)

    def drain(t, carry):
        for k in range(TOP_K):
            row_copy(pos_refs, cur, t, k).wait()
        return carry

    lax.fori_loop(0, TD, drain, 0, unroll=ISSUE_UNROLL)

    w = wts_ref[...]
    f = w[:, 0:1] * buf[cur, 0].reshape(xmid_ref.shape)
    for k in range(1, TOP_K):
        f = f + w[:, k:k + 1] * buf[cur, k].reshape(xmid_ref.shape)
    o_ref[...] = xmid_ref[...] + g2_ref[0] * _rmsnorm(f, npost_ref[...])


def _combine_call(pos_kmajor, wts, xmid2, g2, npost, ys, t_per_batch):
    n, d = xmid2.shape
    steps = n // TD
    return pl.pallas_call(
        _combine_body,
        out_shape=jax.ShapeDtypeStruct((n, d), F32),
        grid=(steps,),
        in_specs=_pos_specs(steps, lambda i: i) + _pos_specs(steps, lambda i: jnp.minimum(i + 1, steps - 1)) + [
                  pl.BlockSpec((TD, LANES), lambda i: (i, 0)),
                  pl.BlockSpec((TD, d), lambda i: (i, 0)),
                  pl.BlockSpec((1, 1, d), lambda i: ((i * TD) // t_per_batch, 0, 0)),
                  pl.BlockSpec((1, d), lambda i: (0, 0)),
                  pl.BlockSpec(memory_space=pl.ANY)],
        out_specs=pl.BlockSpec((TD, d), lambda i: (i, 0)),
        scratch_shapes=[pltpu.VMEM((2, TOP_K, TD) + ys.shape[1:], F32), pltpu.SemaphoreType.DMA((2,))],
        compiler_params=pltpu.CompilerParams(dimension_semantics=("arbitrary",)),
        name="combine",
    )(*([pos_kmajor] * (2 * TOP_K)), wts, xmid2, g2, npost, ys)


def _count_le(ends, q):
    return jnp.sum((ends[None, :] <= q[:, None]).astype(jnp.int32), axis=1)


def _gate_weights(wa, wx):
    return jnp.concatenate([wa, wx], axis=-1).astype(BF16)


def kernel(x, c, ctx, c_ctx, w_mod, b_mod, norm_pre_mix, norm_post_mix, norm_pre_ffn, norm_post_ffn, w_in, conv_w, conv_b, lru_wa, lru_ba, lru_wx, lru_bx, lru_lambda, pool_w, pool_scale, w_rnn_proj, w_pool_proj, w_out, router_w, router_b, exp_w1, exp_b1, exp_w2, exp_b2):
    bsz, t, d = x.shape
    assert w_mod.shape[0] == 1, "single-layer block"
    assert t % TT == 0 and TT % POOL_CHUNK == 0 and POOL_CHUNK % GRID_W == 0 and t % TD == 0
    assert d == SUBLANES * LANES, "a token row must be exactly one (8, 128) slab"
    n = bsz * t
    n_exp = router_w.shape[-1]
    assert (n * TOP_K) % TM == 0 and n <= RANK_RADIX
    row = lambda a: a.reshape(1, -1)

    pad = jnp.zeros((SUBLANES - (bsz + 1) % SUBLANES, d), F32)
    cc = jnp.concatenate([c, c_ctx[None, :], pad], axis=0)
    mod = _mod_call(cc, w_mod[0], row(b_mod[0]))
    lat = lambda j: mod[:bsz, j * d:(j + 1) * d].reshape(bsz, 1, d)
    sh1, sc1, g1, sh2, sc2, g2 = (lat(j) for j in range(6))
    csh1, csc1 = mod[bsz:bsz + 1, 0:d], mod[bsz:bsz + 1, d:2 * d]

    wgf = _gate_weights(lru_wa[0, 0], lru_wx[0, 0])
    wgb = _gate_weights(lru_wa[0, 1], lru_wx[0, 1])
    ba, bx, lam = lru_ba[0], lru_bx[0], lru_lambda[0]
    cw, cb = conv_w[0], row(conv_b[0])
    npre_mix = row(norm_pre_mix[0])

    h0f, h0b = _ctx_call(ctx, npre_mix, csh1, csc1, w_in[0], cw, cb, wgf, wgb, ba, bx, lam)

    hf, xc, gg, xp, mr, mp = _mix_fwd_call(x, npre_mix, sh1, sc1, w_in[0], cw, cb, wgf, ba, bx, lam, h0f)

    pm, pc = _pool_consts()
    xmid, v, code_o, wts_o, cnt = _mix_bwd_call(
        hf, xc, gg, xp, mr, mp, x, wgb, ba, bx, lam, h0b,
        pm, pc, pool_w[0].astype(BF16), row(pool_scale[0]),
        w_rnn_proj[0], w_pool_proj[0], w_out[0],
        row(norm_post_mix[0]), row(norm_pre_ffn[0]), g1, sh2, sc2, router_w[0], row(router_b[0]))

    code = code_o.reshape(n, LANES)[:, :TOP_K].T
    idx, rank = code // RANK_RADIX, code % RANK_RADIX
    counts = cnt[0].astype(jnp.int32)
    tiles_e = (counts + TM - 1) // TM
    tile_end = jnp.cumsum(tiles_e)
    offs = (tile_end - tiles_e) * TM
    e_ids = jnp.arange(n_exp, dtype=jnp.int32)
    off_of = jnp.sum(jnp.where(idx[None] == e_ids[:, None, None], offs[:, None, None], 0), axis=0)
    pos_kmajor = (off_of + rank).reshape(TOP_K * n)
    n_tiles = (n * TOP_K) // TM + n_exp
    n_valid = tile_end[-1]
    jj = jnp.arange(n_tiles, dtype=jnp.int32)
    blk = jnp.minimum(jj, n_valid - 1)
    te = jnp.minimum(_count_le(tile_end, blk), n_exp - 1)

    pad_info = jnp.concatenate([offs + counts, tiles_e * TM - counts,
                                (n_valid * TM)[None], (n_tiles - n_valid)[None]]).astype(jnp.int32)
    xs = _dispatch_call(pos_kmajor, pad_info, n_exp, v.reshape((n,) + v.shape[2:]), n_tiles * TM)
    has = tiles_e > 0
    w_slot = ((jnp.cumsum(has.astype(jnp.int32)) - has.astype(jnp.int32)) % 2).astype(jnp.int32)
    later = (e_ids[None, :] > e_ids[:, None]) & has[None, :]
    w_next = jnp.min(jnp.where(later, e_ids[None, :], n_exp), axis=1).astype(jnp.int32)
    ys = _expert_call(te, blk, n_valid.reshape(1), w_slot, w_next, xs, exp_w1[0],
                      exp_b1[0].reshape(n_exp, 1, -1), exp_w2[0], exp_b2[0].reshape(n_exp, 1, -1))
    out = _combine_call(pos_kmajor, wts_o.reshape(n, LANES), xmid.reshape(n, d), g2, row(norm_post_ffn[0]), ys, t)
    return out.reshape(bsz, t, d)
```
